```python
import math
import jax, jax.numpy as jnp
from jax import lax
import numpy as np

D_MODEL = 4096
BATCH = 2
SEQ = 8192
DEPTH = 1

HEAD_DIM = 128
D_MIX = D_MODEL
H_A = D_MIX // (2 * HEAD_DIM)
H_B = D_MIX // (2 * HEAD_DIM)
D_A = H_A * HEAD_DIM
D_B = H_B * HEAD_DIM
D_FF = 4 * D_MODEL
PLE_DIM = 256
Q_BLOCK = 128
CHUNK = 64
N_IN = 3 * D_A + H_A + 4 * D_B
RMS_EPS = 1e-6

kernel_name = "fox_hgrn2_parallel_hybrid_layer"


def rms_norm(x, g, eps=RMS_EPS):
    xf = x.astype(jnp.float32)
    y = xf * lax.rsqrt(jnp.mean(xf * xf, axis=-1, keepdims=True) + eps)
    return (y * g.astype(jnp.float32)).astype(x.dtype)


def split_columns(proj):
    sizes = (D_A, D_A, D_A, H_A, D_B, D_B, D_B, D_B)
    outs, start = [], 0
    for s in sizes:
        outs.append(proj[..., start:start + s])
        start += s
    return outs


def fox_attention(q, k, v, f_logit):
    B, S, H, Dh = q.shape
    nb = S // Q_BLOCK
    scale = 1.0 / math.sqrt(Dh)
    F = jnp.cumsum(jax.nn.log_sigmoid(f_logit.astype(jnp.float32)), axis=1).transpose(0, 2, 1)
    qh = q.transpose(0, 2, 1, 3)
    kh = k.transpose(0, 2, 1, 3)
    vh = v.transpose(0, 2, 1, 3)
    q_blocks = qh.reshape(B, H, nb, Q_BLOCK, Dh).transpose(2, 0, 1, 3, 4)
    F_blocks = F.reshape(B, H, nb, Q_BLOCK).transpose(2, 0, 1, 3)
    pos_k = jnp.arange(S)

    def one_block(args):
        blk, qi, Fi = args
        pos_q = blk * Q_BLOCK + jnp.arange(Q_BLOCK)
        s = jnp.einsum('bhqd,bhkd->bhqk', qi, kh, preferred_element_type=jnp.float32) * scale
        s = s + Fi[..., :, None] - F[..., None, :]
        s = jnp.where(pos_k[None, :] <= pos_q[:, None], s, -jnp.inf)
        pr = jax.nn.softmax(s, axis=-1)
        return jnp.einsum('bhqk,bhkd->bhqd', pr.astype(vh.dtype), vh)

    out = lax.map(one_block, (jnp.arange(nb), q_blocks, F_blocks))
    return out.transpose(1, 0, 3, 2, 4).reshape(B, S, H, Dh)


def hgrn2_recurrence(q, k, v, logf):
    B, S, H, Dk = q.shape
    Dv = v.shape[-1]
    nc = S // CHUNK

    def to_chunks(t):
        return t.reshape(B, nc, CHUNK, H, t.shape[-1]).transpose(1, 0, 3, 2, 4)

    causal = jnp.tril(jnp.ones((CHUNK, CHUNK), dtype=bool))

    def step(state, inp):
        qi, ki, vi, gi = inp
        b = jnp.cumsum(gi, axis=-2)
        o_inter = jnp.einsum('bhtk,bhkv->bhtv', qi * jnp.exp(b), state)
        diff = b[:, :, :, None, :] - b[:, :, None, :, :]
        decay = jnp.exp(jnp.where(causal[:, :, None], diff, -jnp.inf))
        A = jnp.einsum('bhtk,bhsk,bhtsk->bhts', qi, ki, decay)
        o_intra = jnp.einsum('bhts,bhsv->bhtv', A, vi)
        b_last = b[:, :, -1:, :]
        new_state = jnp.exp(b_last[:, :, 0, :])[..., None] * state + \
            jnp.einsum('bhsk,bhsv->bhkv', ki * jnp.exp(b_last - b), vi)
        return new_state, o_inter + o_intra

    s0 = jnp.zeros((B, H, Dk, Dv), jnp.float32)
    _, o = lax.scan(step, s0, (to_chunks(q), to_chunks(k), to_chunks(v), to_chunks(logf)))
    return o.transpose(1, 0, 3, 2, 4).reshape(B, S, H, Dv)


def setup_inputs(seed: int = 0) -> dict:
    key = jax.random.key(seed)
    ks = jax.random.split(key, 17)
    f32 = jnp.float32

    def nrm(k, shape, scale):
        return jax.random.normal(k, shape, f32) * scale

    return {
        "x": nrm(ks[0], (BATCH, SEQ, D_MODEL), 1.0),
        "p": nrm(ks[1], (DEPTH, BATCH, SEQ, PLE_DIM), 1.0),
        "norm_mix_g": 1.0 + nrm(ks[2], (DEPTH, D_MODEL), 0.02),
        "w_in": nrm(ks[3], (DEPTH, D_MODEL, N_IN), D_MODEL ** -0.5),
        "fox_f_bias": 1.0 + nrm(ks[4], (DEPTH, H_A), 0.1),
        "fox_q_norm_g": 1.0 + nrm(ks[5], (DEPTH, HEAD_DIM), 0.02),
        "fox_k_norm_g": 1.0 + nrm(ks[6], (DEPTH, HEAD_DIM), 0.02),
        "hgrn_lb_logits": nrm(ks[7], (DEPTH + 1, D_B), 0.1),
        "hgrn_norm_g": 1.0 + nrm(ks[8], (DEPTH, HEAD_DIM), 0.02),
        "w_out": nrm(ks[9], (DEPTH, D_MIX, D_MODEL), D_MIX ** -0.5),
        "norm_mlp_g": 1.0 + nrm(ks[10], (DEPTH, D_MODEL), 0.02),
        "w_up": nrm(ks[11], (DEPTH, D_MODEL, D_FF), D_MODEL ** -0.5),
        "w_down": nrm(ks[12], (DEPTH, D_FF, D_MODEL), D_FF ** -0.5),
        "ple_norm_g": 1.0 + nrm(ks[13], (DEPTH, D_MODEL), 0.02),
        "w_ple_gate": nrm(ks[14], (DEPTH, D_MODEL, D_MODEL), D_MODEL ** -0.5),
        "w_ple_proj": nrm(ks[15], (DEPTH, PLE_DIM, D_MODEL), PLE_DIM ** -0.5),
        "ple_post_g": 1.0 + nrm(ks[16], (DEPTH, D_MODEL), 0.02),
    }


def reference(x, p, norm_mix_g, w_in, fox_f_bias, fox_q_norm_g, fox_k_norm_g, hgrn_lb_logits,
              hgrn_norm_g, w_out, norm_mlp_g, w_up, w_down, ple_norm_g, w_ple_gate,
              w_ple_proj, ple_post_g):
    B, S, _ = x.shape
    h = x
    lower_bounds = jnp.cumsum(jax.nn.softmax(hgrn_lb_logits.astype(jnp.float32), axis=0), axis=0)
    for i in range(DEPTH):
        u = rms_norm(h, norm_mix_g[i])
        proj = jnp.einsum('bsd,dn->bsn', u, w_in[i])
        q_a, k_a, v_a, f_a, q_b, f_b, i_b, g_b = split_columns(proj)

        qa = rms_norm(q_a.reshape(B, S, H_A, HEAD_DIM), fox_q_norm_g[i])
        ka = rms_norm(k_a.reshape(B, S, H_A, HEAD_DIM), fox_k_norm_g[i])
        va = v_a.reshape(B, S, H_A, HEAD_DIM)
        out_a = fox_attention(qa, ka, va, f_a + fox_f_bias[i]).reshape(B, S, D_A)

        lb = lower_bounds[i].reshape(H_B, HEAD_DIM)
        z = f_b.astype(jnp.float32).reshape(B, S, H_B, HEAD_DIM)
        logf = jnp.log(lb + (1.0 - lb) * jax.nn.sigmoid(z))
        kb = (1.0 - lb) * jax.nn.sigmoid(-z)
        qb = jax.nn.silu(q_b.astype(jnp.float32)).reshape(B, S, H_B, HEAD_DIM)
        vb = i_b.astype(jnp.float32).reshape(B, S, H_B, HEAD_DIM)
        ob = hgrn2_recurrence(qb, kb, vb, logf)
        ob = rms_norm(ob, hgrn_norm_g[i]).astype(h.dtype).reshape(B, S, D_B)
        out_b = ob * jax.nn.silu(g_b)

        mixed = jnp.concatenate([out_a.astype(h.dtype), out_b], axis=-1)
        h = h + jnp.einsum('bsm,md->bsd', mixed, w_out[i])

        um = rms_norm(h, norm_mlp_g[i])
        hid = jnp.square(jax.nn.relu(jnp.einsum('bsd,df->bsf', um, w_up[i])))
        h = h + jnp.einsum('bsf,fd->bsd', hid, w_down[i])

        gate = jax.nn.sigmoid(jnp.einsum('bsd,de->bse', rms_norm(h, ple_norm_g[i]), w_ple_gate[i]))
        e = rms_norm(jnp.einsum('bsq,qd->bsd', p[i].astype(h.dtype), w_ple_proj[i]), ple_post_g[i])
        h = h + gate * e
    return h
```

```python
import functools
import math

import jax
import jax.numpy as jnp
from jax import lax
from jax.experimental import pallas as pl
from jax.experimental.pallas import tpu as pltpu

F32, BF16 = jnp.float32, jnp.bfloat16
HEAD_DIM = 128
RMS_EPS = 1e-6
LOG2E = math.log2(math.e)
HGRN_CHUNK = 128
HGRN_SUB = 16
V7X_VMEM_LIMIT = 56 * 1024 * 1024

NT_DIMS = (((1,), (1,)), ((), ()))
TN_DIMS = (((0,), (0,)), ((), ()))


def _params(*semantics):
    return pltpu.CompilerParams(dimension_semantics=semantics, vmem_limit_bytes=V7X_VMEM_LIMIT)


def _tile(dim, pref):
    t = min(dim, pref)
    assert dim % t == 0, (dim, pref)
    return t


def _sigmoid(x):
    return 1.0 / (1.0 + jnp.exp(-x))


def _rmsnorm_kernel(x_ref, g_ref, o_ref):
    x = x_ref[...]
    ms = jnp.mean(x * x, axis=-1, keepdims=True)
    o_ref[...] = (x * lax.rsqrt(ms + RMS_EPS) * g_ref[...]).astype(o_ref.dtype)


def _rmsnorm(x, g, out_dtype=BF16):
    m, d = x.shape
    tm = _tile(m, 256)
    return pl.pallas_call(
        _rmsnorm_kernel,
        grid=(m // tm,),
        in_specs=[pl.BlockSpec((tm, d), lambda i: (i, 0)), pl.BlockSpec((1, d), lambda i: (0, 0))],
        out_specs=pl.BlockSpec((tm, d), lambda i: (i, 0)),
        out_shape=jax.ShapeDtypeStruct((m, d), out_dtype),
        compiler_params=_params("parallel"),
        name="rmsnorm",
    )(x, g.reshape(1, d).astype(F32))


def _mm_kernel(a_ref, w_ref, o_ref, *, act):
    acc = jnp.dot(a_ref[...], w_ref[...], preferred_element_type=F32)
    if act == "relu2":
        acc = jnp.square(jnp.maximum(acc, 0.0))
    o_ref[...] = acc.astype(o_ref.dtype)


def _matmul(a, w, out_dtype, act=None, tm=1024, tn=1024, name="matmul"):
    m, k = a.shape
    n = w.shape[1]
    tm, tn = _tile(m, tm), _tile(n, tn)
    return pl.pallas_call(
        functools.partial(_mm_kernel, act=act),
        grid=(m // tm, n // tn),
        in_specs=[pl.BlockSpec((tm, k), lambda i, j: (i, 0)), pl.BlockSpec((k, tn), lambda i, j: (0, j))],
        out_specs=pl.BlockSpec((tm, tn), lambda i, j: (i, j)),
        out_shape=jax.ShapeDtypeStruct((m, n), out_dtype),
        compiler_params=_params("parallel", "arbitrary"),
        name=name,
    )(a, w)


def _inproj_kernel(u_ref, w_ref, gq_ref, gk_ref, o_ref, *, n_qblk):
    j = pl.program_id(1)
    acc = jnp.dot(u_ref[...], w_ref[...], preferred_element_type=F32)
    tn = acc.shape[1]

    @pl.when(j < 2 * n_qblk)
    def _():
        g = jnp.where(j < n_qblk, gq_ref[...], gk_ref[...])
        for hh in range(tn // HEAD_DIM):
            sl = slice(hh * HEAD_DIM, (hh + 1) * HEAD_DIM)
            xs = acc[:, sl]
            ms = jnp.mean(xs * xs, axis=-1, keepdims=True)
            o_ref[:, sl] = (xs * lax.rsqrt(ms + RMS_EPS) * g).astype(o_ref.dtype)

    @pl.when(j >= 2 * n_qblk)
    def _():
        o_ref[...] = acc.astype(o_ref.dtype)


def _inproj(u, w, gq, gk, d_a, tm=1024, tn=1024):
    m, k = u.shape
    n = w.shape[1]
    tm, tn = _tile(m, tm), _tile(d_a, tn)
    return pl.pallas_call(
        functools.partial(_inproj_kernel, n_qblk=d_a // tn),
        grid=(m // tm, n // tn),
        in_specs=[pl.BlockSpec((tm, k), lambda i, j: (i, 0)), pl.BlockSpec((k, tn), lambda i, j: (0, j)),
                  pl.BlockSpec((1, HEAD_DIM), lambda i, j: (0, 0)), pl.BlockSpec((1, HEAD_DIM), lambda i, j: (0, 0))],
        out_specs=pl.BlockSpec((tm, tn), lambda i, j: (i, j)),
        out_shape=jax.ShapeDtypeStruct((m, n), BF16),
        compiler_params=_params("parallel", "arbitrary"),
        name="inproj",
    )(u, w, gq, gk)


def _outproj_kernel(a_ref, b_ref, wa_ref, wb_ref, x_ref, o_ref):
    acc = jnp.dot(a_ref[...], wa_ref[...], preferred_element_type=F32)
    acc = acc + jnp.dot(b_ref[...], wb_ref[...], preferred_element_type=F32)
    o_ref[...] = x_ref[...] + acc


def _outproj(a, b, w, x, tm=1024, tn=512):
    m, ka = a.shape
    kb = b.shape[1]
    n = w.shape[1]
    assert ka == kb and w.shape[0] == ka + kb
    tm, tn = _tile(m, tm), _tile(n, tn)
    return pl.pallas_call(
        _outproj_kernel,
        grid=(m // tm, n // tn),
        in_specs=[pl.BlockSpec((tm, ka), lambda i, j: (i, 0)), pl.BlockSpec((tm, kb), lambda i, j: (i, 0)),
                  pl.BlockSpec((ka, tn), lambda i, j: (0, j)), pl.BlockSpec((kb, tn), lambda i, j: (1, j)),
                  pl.BlockSpec((tm, tn), lambda i, j: (i, j))],
        out_specs=pl.BlockSpec((tm, tn), lambda i, j: (i, j)),
        out_shape=jax.ShapeDtypeStruct((m, n), F32),
        compiler_params=_params("parallel", "arbitrary"),
        name="outproj",
    )(a, b, w, w, x)


def _mm_res_kernel(a_ref, w_ref, r_ref, o_ref):
    k = pl.program_id(2)
    acc = jnp.dot(a_ref[...], w_ref[...], preferred_element_type=F32)

    @pl.when(k == 0)
    def _():
        o_ref[...] = r_ref[...] + acc

    @pl.when(k > 0)
    def _():
        o_ref[...] += acc


def _matmul_residual(a, w, r, tm=1024, tn=1024, tk=2048):
    m, kd = a.shape
    n = w.shape[1]
    tm, tn, tk = _tile(m, tm), _tile(n, tn), _tile(kd, tk)
    return pl.pallas_call(
        _mm_res_kernel,
        grid=(m // tm, n // tn, kd // tk),
        in_specs=[pl.BlockSpec((tm, tk), lambda i, j, k: (i, k)), pl.BlockSpec((tk, tn), lambda i, j, k: (k, j)),
                  pl.BlockSpec((tm, tn), lambda i, j, k: (i, j))],
        out_specs=pl.BlockSpec((tm, tn), lambda i, j, k: (i, j)),
        out_shape=jax.ShapeDtypeStruct((m, n), F32),
        compiler_params=_params("parallel", "parallel", "arbitrary"),
        name="mlp_down",
    )(a, w, r)


def _fcum_kernel(f_ref, bias_ref, o_ref, carry_ref):
    t = pl.program_id(0)

    @pl.when(t == 0)
    def _():
        carry_ref[...] = jnp.zeros_like(carry_ref)

    x = f_ref[...] + bias_ref[...]
    ls = jnp.minimum(x, 0.0) - jnp.log1p(jnp.exp(-jnp.abs(x)))
    tc = x.shape[1]
    r = lax.broadcasted_iota(jnp.int32, (tc, tc), 0)
    c = lax.broadcasted_iota(jnp.int32, (tc, tc), 1)
    tri = jnp.where(r <= c, 1.0, 0.0).astype(F32)
    cs = jnp.dot(ls, tri, preferred_element_type=F32, precision=lax.Precision.HIGHEST) + carry_ref[...]
    o_ref[...] = cs * LOG2E
    carry_ref[...] = cs[:, tc - 1:tc]


def _forget_cumsum(f_rows, bias_rows):
    bh, s = f_rows.shape
    tc = _tile(s, 512)
    return pl.pallas_call(
        _fcum_kernel,
        grid=(s // tc,),
        in_specs=[pl.BlockSpec((bh, tc), lambda t: (0, t)), pl.BlockSpec((bh, 1), lambda t: (0, 0))],
        out_specs=pl.BlockSpec((bh, tc), lambda t: (0, t)),
        out_shape=jax.ShapeDtypeStruct((bh, s), F32),
        scratch_shapes=[pltpu.VMEM((bh, 1), F32)],
        compiler_params=_params("arbitrary"),
        name="fox_forget_cumsum",
    )(f_rows, bias_rows)


def _attn_kernel(q_ref, k_ref, v_ref, f_ref, o_ref, *, blk):
    qi = pl.program_id(2)
    q = q_ref[...]

    def step(off, carry, masked):
        m, l, acc = carry
        k = k_ref[pl.ds(off, blk), :]
        v = v_ref[pl.ds(off, blk), :]
        s = lax.dot_general(q, k, NT_DIMS, preferred_element_type=F32) - f_ref[:, pl.ds(off, blk)]
        if masked:
            r = lax.broadcasted_iota(jnp.int32, (blk, blk), 0)
            c = lax.broadcasted_iota(jnp.int32, (blk, blk), 1)
            s = jnp.where(c <= r, s, -jnp.inf)
        m_new = jnp.maximum(m, jnp.max(s, axis=-1, keepdims=True))
        p = jnp.exp2(s - m_new)
        alpha = jnp.exp2(m - m_new)
        l = alpha * l + jnp.sum(p, axis=-1, keepdims=True)
        acc = alpha * acc + jnp.dot(p.astype(v.dtype), v, preferred_element_type=F32)
        return m_new, l, acc

    init = (jnp.full((blk, 1), -jnp.inf, F32), jnp.zeros((blk, 1), F32), jnp.zeros((blk, HEAD_DIM), F32))
    carry = lax.fori_loop(0, qi, lambda j, c: step(pl.multiple_of(j * blk, blk), c, False), init)
    _, l, acc = step(pl.multiple_of(qi * blk, blk), carry, True)
    o_ref[...] = (acc / l).astype(o_ref.dtype)


def _fox_attention(proj, f2_rows, batch, seq, n_heads, blk=512):
    blk = _tile(seq, blk)
    nq = seq // blk
    return pl.pallas_call(
        functools.partial(_attn_kernel, blk=blk),
        grid=(batch, n_heads, nq),
        in_specs=[pl.BlockSpec((blk, HEAD_DIM), lambda b, h, i: (b * nq + i, h)),
                  pl.BlockSpec((seq, HEAD_DIM), lambda b, h, i: (b, n_heads + h)),
                  pl.BlockSpec((seq, HEAD_DIM), lambda b, h, i: (b, 2 * n_heads + h)),
                  pl.BlockSpec((None, 1, seq), lambda b, h, i: (b * n_heads + h, 0, 0))],
        out_specs=pl.BlockSpec((blk, HEAD_DIM), lambda b, h, i: (b * nq + i, h)),
        out_shape=jax.ShapeDtypeStruct((batch * seq, n_heads * HEAD_DIM), BF16),
        compiler_params=_params("parallel", "parallel", "arbitrary"),
        name="fox_attention",
    )(proj, proj, proj, f2_rows)


def _hgrn_kernel(q_ref, z_ref, v_ref, gate_ref, lbl_ref, gn_ref, o_ref, st_ref, *, layer, heads, chunks):
    C, c = HGRN_CHUNK, HGRN_SUB
    t = pl.program_id(2)

    @pl.when(t == 0)
    def _():
        st_ref[...] = jnp.zeros_like(st_ref)

    logits = lbl_ref[...]
    e = jnp.exp(logits - jnp.max(logits, axis=0, keepdims=True))
    lb_all = jnp.sum(e[:layer + 1], axis=0, keepdims=True) / jnp.sum(e, axis=0, keepdims=True)

    row = lax.broadcasted_iota(jnp.int32, (C, C), 0)
    col = lax.broadcasted_iota(jnp.int32, (C, C), 1)
    causal = col <= row
    ltri = jnp.where(causal, 1.0, 0.0).astype(F32)
    lane = lax.broadcasted_iota(jnp.int32, (c, C), 1)

    def chunk_body(ci, carry):
        r0 = pl.multiple_of(ci * C, C)
        for hh in range(heads):
            cols = slice(hh * HEAD_DIM, (hh + 1) * HEAD_DIM)
            lb = lb_all[:, cols]
            q = q_ref[pl.ds(r0, C), cols].astype(F32)
            z = z_ref[pl.ds(r0, C), cols].astype(F32)
            v = v_ref[pl.ds(r0, C), cols]
            gate = gate_ref[pl.ds(r0, C), cols].astype(F32)

            ez = jnp.exp(-jnp.abs(z))
            inv = 1.0 / (1.0 + ez)
            sig_z = jnp.where(z >= 0, 1.0, ez) * inv
            sig_mz = jnp.where(z >= 0, ez, 1.0) * inv
            logf = jnp.log(lb + (1.0 - lb) * sig_z)
            kk = (1.0 - lb) * sig_mz
            qs = q * _sigmoid(q)
            b = jnp.dot(ltri, logf, preferred_element_type=F32, precision=lax.Precision.HIGHEST)
            b_last = b[C - 1:C, :]

            st = st_ref[hh]
            o = lax.dot_general((qs * jnp.exp(b)).astype(BF16), st.astype(BF16), NT_DIMS,
                                preferred_element_type=F32)

            blocks = []
            for i in range(C // c):
                rows = slice(i * c, (i + 1) * c)
                q_i, b_i = qs[rows], b[rows]
                blk = jnp.zeros((c, C), F32)
                for jj in range(c):
                    s = i * c + jj
                    w = q_i * kk[s:s + 1, :] * jnp.exp(jnp.minimum(b_i - b[s:s + 1, :], 0.0))
                    blk = jnp.where(lane == s, jnp.sum(w, axis=-1, keepdims=True), blk)
                if i > 0:
                    edge = b[i * c - 1:i * c, :]
                    q_t = (q_i * jnp.exp(b_i - edge)).astype(BF16)
                    k_t = (kk * jnp.exp(jnp.minimum(edge - b, 0.0))).astype(BF16)
                    off = lax.dot_general(q_t, k_t, NT_DIMS, preferred_element_type=F32)
                    blk = jnp.where(lane < i * c, off, blk)
                blocks.append(blk)
            a = jnp.where(causal, jnp.concatenate(blocks, axis=0), 0.0)
            o = o + jnp.dot(a.astype(BF16), v, preferred_element_type=F32)

            k_d = (kk * jnp.exp(b_last - b)).astype(BF16)
            st_ref[hh] = st * jnp.exp(b_last) + lax.dot_general(v, k_d, TN_DIMS, preferred_element_type=F32)

            ms = jnp.mean(o * o, axis=-1, keepdims=True)
            on = o * lax.rsqrt(ms + RMS_EPS) * gn_ref[...]
            o_ref[pl.ds(r0, C), cols] = (on * (gate * _sigmoid(gate))).astype(o_ref.dtype)
        return carry

    lax.fori_loop(0, chunks, chunk_body, 0)


def _hgrn2(proj, lb_logits, gn, layer, batch, seq, n_heads, col0, heads=2, rows=512):
    heads = _tile(n_heads, heads)
    rows = _tile(seq, rows)
    assert rows % HGRN_CHUNK == 0 and col0 % heads == 0
    nt = seq // rows
    w = heads * HEAD_DIM
    hblk = n_heads // heads

    def in_spec(group):
        return pl.BlockSpec((rows, w), lambda b, h, t: (b * nt + t, col0 // heads + group * hblk + h))

    return pl.pallas_call(
        functools.partial(_hgrn_kernel, layer=layer, heads=heads, chunks=rows // HGRN_CHUNK),
        grid=(batch, hblk, nt),
        in_specs=[in_spec(0), in_spec(1), in_spec(2), in_spec(3),
                  pl.BlockSpec((lb_logits.shape[0], w), lambda b, h, t: (0, h)),
                  pl.BlockSpec((1, HEAD_DIM), lambda b, h, t: (0, 0))],
        out_specs=pl.BlockSpec((rows, w), lambda b, h, t: (b * nt + t, h)),
        out_shape=jax.ShapeDtypeStruct((batch * seq, n_heads * HEAD_DIM), BF16),
        scratch_shapes=[pltpu.VMEM((heads, HEAD_DIM, HEAD_DIM), F32)],
        compiler_params=_params("parallel", "parallel", "arbitrary"),
        name="hgrn2",
    )(proj, proj, proj, proj, lb_logits, gn)


def _ple_embed_kernel(p_ref, w_ref, g_ref, o_ref):
    e = jnp.dot(p_ref[...].astype(BF16), w_ref[...], preferred_element_type=F32)
    ms = jnp.mean(e * e, axis=-1, keepdims=True)
    o_ref[...] = (e * lax.rsqrt(ms + RMS_EPS) * g_ref[...]).astype(o_ref.dtype)


def _ple_embed(p, w, g):
    m, kd = p.shape
    n = w.shape[1]
    tm = _tile(m, 256)
    return pl.pallas_call(
        _ple_embed_kernel,
        grid=(m // tm,),
        in_specs=[pl.BlockSpec((tm, kd), lambda i: (i, 0)), pl.BlockSpec((kd, n), lambda i: (0, 0)),
                  pl.BlockSpec((1, n), lambda i: (0, 0))],
        out_specs=pl.BlockSpec((tm, n), lambda i: (i, 0)),
        out_shape=jax.ShapeDtypeStruct((m, n), F32),
        compiler_params=_params("parallel"),
        name="ple_embed",
    )(p, w, g.reshape(1, n).astype(F32))


def _ple_gate_kernel(a_ref, w_ref, e_ref, h_ref, o_ref):
    gate = _sigmoid(jnp.dot(a_ref[...], w_ref[...], preferred_element_type=F32))
    o_ref[...] = h_ref[...] + gate * e_ref[...]


def _ple_gate(a, w, e, h, tm=1024, tn=512):
    m, kd = a.shape
    n = w.shape[1]
    tm, tn = _tile(m, tm), _tile(n, tn)
    return pl.pallas_call(
        _ple_gate_kernel,
        grid=(m // tm, n // tn),
        in_specs=[pl.BlockSpec((tm, kd), lambda i, j: (i, 0)), pl.BlockSpec((kd, tn), lambda i, j: (0, j)),
                  pl.BlockSpec((tm, tn), lambda i, j: (i, j)), pl.BlockSpec((tm, tn), lambda i, j: (i, j))],
        out_specs=pl.BlockSpec((tm, tn), lambda i, j: (i, j)),
        out_shape=jax.ShapeDtypeStruct((m, n), F32),
        compiler_params=_params("parallel", "arbitrary"),
        name="ple_gate",
    )(a, w, e, h)


def _layer(h, p, layer, batch, seq, norm_mix_g, w_in, fox_f_bias, fox_q_norm_g, fox_k_norm_g, hgrn_lb_logits,
           hgrn_norm_g, w_out, norm_mlp_g, w_up, w_down, ple_norm_g, w_ple_gate, w_ple_proj, ple_post_g):
    d = h.shape[1]
    d_a = d_b = d // 2
    n_a, n_b = d_a // HEAD_DIM, d_b // HEAD_DIM
    assert n_a <= HEAD_DIM and w_in.shape[1] == 3 * d_a + n_a + 4 * d_b

    w_main = jnp.concatenate([w_in[:, :3 * d_a], w_in[:, 3 * d_a + n_a:]], axis=1).astype(BF16)
    w_fa = jnp.pad(w_in[:, 3 * d_a:3 * d_a + n_a], ((0, 0), (0, HEAD_DIM - n_a))).astype(BF16)
    q_gain = (fox_q_norm_g.astype(F32) * (LOG2E / math.sqrt(HEAD_DIM))).reshape(1, HEAD_DIM)
    k_gain = fox_k_norm_g.astype(F32).reshape(1, HEAD_DIM)

    u = _rmsnorm(h, norm_mix_g)
    proj = _inproj(u, w_main, q_gain, k_gain, d_a)
    f_logit = _matmul(u, w_fa, F32, name="fox_forget_proj")

    f_rows = f_logit[:, :n_a].reshape(batch, seq, n_a).transpose(0, 2, 1).reshape(batch * n_a, seq)
    bias_rows = jnp.tile(fox_f_bias.astype(F32), batch).reshape(batch * n_a, 1)
    f2 = _forget_cumsum(f_rows, bias_rows).reshape(batch * n_a, 1, seq)
    out_a = _fox_attention(proj, f2, batch, seq, n_a)

    out_b = _hgrn2(proj, hgrn_lb_logits.astype(F32), hgrn_norm_g.astype(F32).reshape(1, HEAD_DIM), layer,
                   batch, seq, n_b, col0=3 * n_a)

    h = _outproj(out_a, out_b, w_out.astype(BF16), h)

    um = _rmsnorm(h, norm_mlp_g)
    hid = _matmul(um, w_up.astype(BF16), BF16, act="relu2", name="mlp_up")
    h = _matmul_residual(hid, w_down.astype(BF16), h)

    e = _ple_embed(p, w_ple_proj.astype(BF16), ple_post_g)
    return _ple_gate(_rmsnorm(h, ple_norm_g), w_ple_gate.astype(BF16), e, h)


def kernel(x, p, norm_mix_g, w_in, fox_f_bias, fox_q_norm_g, fox_k_norm_g, hgrn_lb_logits, hgrn_norm_g, w_out,
           norm_mlp_g, w_up, w_down, ple_norm_g, w_ple_gate, w_ple_proj, ple_post_g):
    batch, seq, d = x.shape
    h = x.reshape(batch * seq, d)
    for i in range(w_in.shape[0]):
        h = _layer(h, p[i].reshape(batch * seq, -1), i, batch, seq, norm_mix_g[i], w_in[i], fox_f_bias[i],
                   fox_q_norm_g[i], fox_k_norm_g[i], hgrn_lb_logits, hgrn_norm_g[i], w_out[i], norm_mlp_g[i],
                   w_up[i], w_down[i], ple_norm_g[i], w_ple_gate[i], w_ple_proj[i], ple_post_g[i])
    return h.reshape(batch, seq, d)
```

```python
import functools
import math

import jax
import jax.numpy as jnp
from jax import lax
from jax.experimental import pallas as pl
from jax.experimental.pallas import tpu as pltpu

F32, BF16 = jnp.float32, jnp.bfloat16
HEAD_DIM = 128
RMS_EPS = 1e-6
LOG2E = math.log2(math.e)
HGRN_CHUNK = 128
HGRN_SUB = 8
V7X_VMEM_LIMIT = 56 * 1024 * 1024
V7X_VMEM_LIMIT_HIGH = 60 * 1024 * 1024

NT_DIMS = (((1,), (1,)), ((), ()))
TN_DIMS = (((0,), (0,)), ((), ()))


def _params(*semantics, vmem_limit=V7X_VMEM_LIMIT):
    return pltpu.CompilerParams(dimension_semantics=semantics, vmem_limit_bytes=vmem_limit)


def _tile(dim, pref):
    t = min(dim, pref)
    assert dim % t == 0, (dim, pref)
    return t


def _sigmoid(x):
    return 1.0 / (1.0 + jnp.exp(-x))


def _rmsnorm_kernel(x_ref, g_ref, o_ref):
    x = x_ref[...]
    ms = jnp.mean(x * x, axis=-1, keepdims=True)
    o_ref[...] = (x * lax.rsqrt(ms + RMS_EPS) * g_ref[...]).astype(o_ref.dtype)


def _rmsnorm(x, g, out_dtype=BF16):
    m, d = x.shape
    tm = _tile(m, 256)
    return pl.pallas_call(
        _rmsnorm_kernel,
        grid=(m // tm,),
        in_specs=[pl.BlockSpec((tm, d), lambda i: (i, 0)), pl.BlockSpec((1, d), lambda i: (0, 0))],
        out_specs=pl.BlockSpec((tm, d), lambda i: (i, 0)),
        out_shape=jax.ShapeDtypeStruct((m, d), out_dtype),
        compiler_params=_params("parallel"),
        name="rmsnorm",
    )(x, g.reshape(1, d).astype(F32))


def _mm_kernel(a_ref, w_ref, o_ref, *, act):
    acc = jnp.dot(a_ref[...], w_ref[...], preferred_element_type=F32)
    if act == "relu2":
        acc = jnp.square(jnp.maximum(acc, 0.0))
    o_ref[...] = acc.astype(o_ref.dtype)


def _matmul(a, w, out_dtype, act=None, tm=1024, tn=1024, name="matmul"):
    m, k = a.shape
    n = w.shape[1]
    tm, tn = _tile(m, tm), _tile(n, tn)
    return pl.pallas_call(
        functools.partial(_mm_kernel, act=act),
        grid=(m // tm, n // tn),
        in_specs=[pl.BlockSpec((tm, k), lambda i, j: (i, 0)), pl.BlockSpec((k, tn), lambda i, j: (0, j))],
        out_specs=pl.BlockSpec((tm, tn), lambda i, j: (i, j)),
        out_shape=jax.ShapeDtypeStruct((m, n), out_dtype),
        compiler_params=_params("parallel", "arbitrary"),
        name=name,
    )(a, w)


def _inproj_kernel(u_ref, w_ref, wf_ref, gq_ref, gk_ref, o_ref, f_ref, *, n_qblk):
    j = pl.program_id(1)
    acc = jnp.dot(u_ref[...], w_ref[...], preferred_element_type=F32)
    tn = acc.shape[1]

    @pl.when(j == 0)
    def _():
        f_ref[...] = jnp.dot(u_ref[...], wf_ref[...], preferred_element_type=F32)

    @pl.when(j < 2 * n_qblk)
    def _():
        g = jnp.where(j < n_qblk, gq_ref[...], gk_ref[...])
        for hh in range(tn // HEAD_DIM):
            sl = slice(hh * HEAD_DIM, (hh + 1) * HEAD_DIM)
            xs = acc[:, sl]
            ms = jnp.mean(xs * xs, axis=-1, keepdims=True)
            o_ref[:, sl] = (xs * lax.rsqrt(ms + RMS_EPS) * g).astype(o_ref.dtype)

    @pl.when(j >= 2 * n_qblk)
    def _():
        o_ref[...] = acc.astype(o_ref.dtype)


def _inproj(u, w, wf, gq, gk, d_a, tm=1024, tn=1024):
    m, k = u.shape
    n, nf = w.shape[1], wf.shape[1]
    tm, tn = _tile(m, tm), _tile(d_a, tn)
    return pl.pallas_call(
        functools.partial(_inproj_kernel, n_qblk=d_a // tn),
        grid=(m // tm, n // tn),
        in_specs=[pl.BlockSpec((tm, k), lambda i, j: (i, 0)), pl.BlockSpec((k, tn), lambda i, j: (0, j)),
                  pl.BlockSpec((k, nf), lambda i, j: (0, 0)),
                  pl.BlockSpec((1, HEAD_DIM), lambda i, j: (0, 0)), pl.BlockSpec((1, HEAD_DIM), lambda i, j: (0, 0))],
        out_specs=[pl.BlockSpec((tm, tn), lambda i, j: (i, j)), pl.BlockSpec((tm, nf), lambda i, j: (i, 0))],
        out_shape=[jax.ShapeDtypeStruct((m, n), BF16), jax.ShapeDtypeStruct((m, nf), F32)],
        compiler_params=_params("parallel", "arbitrary"),
        name="inproj",
    )(u, w, wf, gq, gk)


def _outproj_kernel(a_ref, b_ref, wa_ref, wb_ref, x_ref, o_ref):
    acc = jnp.dot(a_ref[...], wa_ref[...], preferred_element_type=F32)
    acc = acc + jnp.dot(b_ref[...], wb_ref[...], preferred_element_type=F32)
    o_ref[...] = x_ref[...] + acc


def _outproj(a, b, w, x, tm=1024, tn=512):
    m, ka = a.shape
    kb = b.shape[1]
    n = w.shape[1]
    assert ka == kb and w.shape[0] == ka + kb
    tm, tn = _tile(m, tm), _tile(n, tn)
    return pl.pallas_call(
        _outproj_kernel,
        grid=(m // tm, n // tn),
        in_specs=[pl.BlockSpec((tm, ka), lambda i, j: (i, 0)), pl.BlockSpec((tm, kb), lambda i, j: (i, 0)),
                  pl.BlockSpec((ka, tn), lambda i, j: (0, j)), pl.BlockSpec((kb, tn), lambda i, j: (1, j)),
                  pl.BlockSpec((tm, tn), lambda i, j: (i, j))],
        out_specs=pl.BlockSpec((tm, tn), lambda i, j: (i, j)),
        out_shape=jax.ShapeDtypeStruct((m, n), F32),
        compiler_params=_params("parallel", "arbitrary"),
        name="outproj",
    )(a, b, w, w, x)


def _mm_res_kernel(a_ref, w_ref, r_ref, o_ref):
    k = pl.program_id(2)
    acc = jnp.dot(a_ref[...], w_ref[...], preferred_element_type=F32)

    @pl.when(k == 0)
    def _():
        o_ref[...] = r_ref[...] + acc

    @pl.when(k > 0)
    def _():
        o_ref[...] += acc


def _matmul_residual(a, w, r, tm=1024, tn=1024, tk=4096):
    m, kd = a.shape
    n = w.shape[1]
    tm, tn, tk = _tile(m, tm), _tile(n, tn), _tile(kd, tk)
    return pl.pallas_call(
        _mm_res_kernel,
        grid=(m // tm, n // tn, kd // tk),
        in_specs=[pl.BlockSpec((tm, tk), lambda i, j, k: (i, k)), pl.BlockSpec((tk, tn), lambda i, j, k: (k, j)),
                  pl.BlockSpec((tm, tn), lambda i, j, k: (i, j))],
        out_specs=pl.BlockSpec((tm, tn), lambda i, j, k: (i, j)),
        out_shape=jax.ShapeDtypeStruct((m, n), F32),
        compiler_params=_params("parallel", "parallel", "arbitrary", vmem_limit=V7X_VMEM_LIMIT_HIGH),
        name="mlp_down",
    )(a, w, r)


def _split3(x):
    hi = x.astype(BF16).astype(F32)
    r = x - hi
    mid = r.astype(BF16).astype(F32)
    return hi, mid, r - mid


def _fcum_kernel(f_ref, bias_ref, o_ref, carry_ref, *, n_heads):
    t = pl.program_id(1)

    @pl.when(t == 0)
    def _():
        carry_ref[...] = jnp.zeros_like(carry_ref)

    x = f_ref[...] + bias_ref[...]
    ls = jnp.minimum(x, 0.0) - jnp.log1p(jnp.exp(-jnp.abs(x)))
    tc = x.shape[0]
    r = lax.broadcasted_iota(jnp.int32, (tc, tc), 0)
    c = lax.broadcasted_iota(jnp.int32, (tc, tc), 1)
    tri = jnp.where(c <= r, 1.0, 0.0).astype(F32)
    cs = jnp.dot(tri, ls, preferred_element_type=F32, precision=lax.Precision.HIGHEST) + carry_ref[...]
    carry_ref[...] = cs[tc - 1:tc, :]
    lane = lax.broadcasted_iota(jnp.int32, (tc, HEAD_DIM), 1)
    for h in range(n_heads):
        hi, mid, lo = _split3(jnp.broadcast_to(cs[:, h:h + 1] * (-LOG2E), (tc, HEAD_DIM)))
        tile = jnp.where(lane == 0, hi, jnp.where(lane == 1, mid, jnp.where(lane == 2, lo, 0.0)))
        o_ref[:, h * HEAD_DIM:(h + 1) * HEAD_DIM] = tile.astype(o_ref.dtype)


def _forget_bias_tiles(f_logit, bias, batch, seq, n_heads):
    tc = _tile(seq, 512)
    nt = seq // tc
    return pl.pallas_call(
        functools.partial(_fcum_kernel, n_heads=n_heads),
        grid=(batch, nt),
        in_specs=[pl.BlockSpec((tc, HEAD_DIM), lambda b, t: (b * nt + t, 0)),
                  pl.BlockSpec((1, HEAD_DIM), lambda b, t: (0, 0))],
        out_specs=pl.BlockSpec((tc, n_heads * HEAD_DIM), lambda b, t: (b * nt + t, 0)),
        out_shape=jax.ShapeDtypeStruct((batch * seq, n_heads * HEAD_DIM), BF16),
        scratch_shapes=[pltpu.VMEM((1, HEAD_DIM), F32)],
        compiler_params=_params("parallel", "arbitrary"),
        name="fox_forget_cumsum",
    )(f_logit, bias)


def _attn_kernel(q_ref, k_ref, kb_ref, v_ref, o_ref, kp_ref, s_ref, m_ref, l_ref, acc_ref, *, blk, heads):
    qi = pl.program_id(2)
    hcols = [slice(hh * HEAD_DIM, (hh + 1) * HEAD_DIM) for hh in range(heads)]
    seq = k_ref.shape[0]

    @pl.when(qi == 0)
    def _():
        def copy(c, carry):
            rows = pl.ds(pl.multiple_of(c * blk, blk), blk)
            for hh, cols in enumerate(hcols):
                kp_ref[hh, rows, :HEAD_DIM] = k_ref[rows, cols]
                kp_ref[hh, rows, HEAD_DIM:] = kb_ref[rows, cols]
            return carry
        lax.fori_loop(0, seq // blk, copy, 0)

    lane = lax.broadcasted_iota(jnp.int32, (blk, HEAD_DIM), 1)
    q_ones = jnp.where(lane < 3, 1.0, 0.0).astype(BF16)
    key_pos = lax.broadcasted_iota(jnp.int32, (blk, blk), 0)
    qry_pos = lax.broadcasted_iota(jnp.int32, (blk, blk), 1)

    for hh, cols in enumerate(hcols):
        q2 = jnp.concatenate([q_ref[:, cols], q_ones], axis=1)

        def scores(j, slot):
            rows = pl.ds(pl.multiple_of(j * blk, blk), blk)
            s_ref[slot] = lax.dot_general(kp_ref[hh, rows, :], q2, NT_DIMS, preferred_element_type=F32)

        def update(j, slot, diagonal=False):
            rows = pl.ds(pl.multiple_of(j * blk, blk), blk)
            s = s_ref[slot]
            if diagonal:
                s = jnp.where(key_pos <= qry_pos, s, -jnp.inf)
            m = m_ref[...]
            m_new = jnp.maximum(m, jnp.max(s, axis=0, keepdims=True))
            p = jnp.exp2(s - m_new)
            alpha = jnp.exp2(m - m_new)
            m_ref[...] = m_new
            l_ref[...] = alpha * l_ref[...] + jnp.sum(p, axis=0, keepdims=True)
            pv = lax.dot_general(v_ref[rows, cols], p.astype(BF16), TN_DIMS, preferred_element_type=F32)
            acc_ref[...] = alpha * acc_ref[...] + pv

        def pair(k, carry):
            scores(2 * k + 1, 1)
            update(2 * k, 0)
            scores(2 * k + 2, 0)
            update(2 * k + 1, 1)
            return carry

        m_ref[...] = jnp.full(m_ref.shape, -jnp.inf, F32)
        l_ref[...] = jnp.zeros(l_ref.shape, F32)
        acc_ref[...] = jnp.zeros(acc_ref.shape, F32)
        scores(0, 0)
        lax.fori_loop(0, qi // 2, pair, 0)

        @pl.when(qi % 2 == 0)
        def _():
            update(qi, 0, diagonal=True)

        @pl.when(qi % 2 == 1)
        def _():
            scores(qi, 1)
            update(qi - 1, 0)
            update(qi, 1, diagonal=True)

        o_ref[:, cols] = (acc_ref[...] / l_ref[...]).T.astype(o_ref.dtype)


def _fox_attention(proj, kbias, batch, seq, n_heads, blk=512, heads=2):
    blk = _tile(seq, blk)
    heads = _tile(n_heads, heads)
    nq = seq // blk
    hblk = n_heads // heads
    w = heads * HEAD_DIM
    return pl.pallas_call(
        functools.partial(_attn_kernel, blk=blk, heads=heads),
        grid=(batch, hblk, nq),
        in_specs=[pl.BlockSpec((blk, w), lambda b, h, i: (b * nq + i, h)),
                  pl.BlockSpec((seq, w), lambda b, h, i: (b, hblk + h)),
                  pl.BlockSpec((seq, w), lambda b, h, i: (b, h)),
                  pl.BlockSpec((seq, w), lambda b, h, i: (b, 2 * hblk + h))],
        out_specs=pl.BlockSpec((blk, w), lambda b, h, i: (b * nq + i, h)),
        out_shape=jax.ShapeDtypeStruct((batch * seq, n_heads * HEAD_DIM), BF16),
        scratch_shapes=[pltpu.VMEM((heads, seq, 2 * HEAD_DIM), BF16), pltpu.VMEM((2, blk, blk), F32),
                        pltpu.VMEM((1, blk), F32), pltpu.VMEM((1, blk), F32), pltpu.VMEM((HEAD_DIM, blk), F32)],
        compiler_params=_params("parallel", "parallel", "arbitrary"),
        name="fox_attention",
    )(proj, proj, kbias, proj)


def _hgrn_kernel(q_ref, z_ref, v_ref, gate_ref, lbl_ref, gn_ref, o_ref, st_ref, *, layer, heads, chunks):
    C, c = HGRN_CHUNK, HGRN_SUB
    t = pl.program_id(2)

    @pl.when(t == 0)
    def _():
        st_ref[...] = jnp.zeros_like(st_ref)

    logits = lbl_ref[...]
    e = jnp.exp(logits - jnp.max(logits, axis=0, keepdims=True))
    lb_all = jnp.sum(e[:layer + 1], axis=0, keepdims=True) / jnp.sum(e, axis=0, keepdims=True)

    row = lax.broadcasted_iota(jnp.int32, (C, C), 0)
    col = lax.broadcasted_iota(jnp.int32, (C, C), 1)
    ltri = jnp.where(col <= row, 1.0, 0.0).astype(BF16)
    sizes = [c << l for l in range(1, (C // c).bit_length())]
    xor = row ^ col
    level = jnp.zeros((C, C), jnp.int32)
    for l, m in enumerate(sizes):
        level = jnp.where(xor >= m // 2, l + 1, level)
    level = jnp.where(col > row, -1, level)
    lane = lax.broadcasted_iota(jnp.int32, (c, C), 1)

    def chunk_body(ci, carry):
        r0 = pl.multiple_of(ci * C, C)
        for hh in range(heads):
            cols = slice(hh * HEAD_DIM, (hh + 1) * HEAD_DIM)
            lb = lb_all[:, cols]
            q = q_ref[pl.ds(r0, C), cols].astype(F32)
            z = z_ref[pl.ds(r0, C), cols].astype(F32)
            v = v_ref[pl.ds(r0, C), cols]
            gate = gate_ref[pl.ds(r0, C), cols].astype(F32)

            ez = jnp.exp(-jnp.abs(z))
            inv = 1.0 / (1.0 + ez)
            sig_z = jnp.where(z >= 0, 1.0, ez) * inv
            sig_mz = jnp.where(z >= 0, ez, 1.0) * inv
            kk = (1.0 - lb) * sig_mz
            qs = q * _sigmoid(q)
            log2f = jnp.concatenate(_split3(jnp.log2(lb + (1.0 - lb) * sig_z)), axis=1).astype(BF16)
            cs = jnp.dot(ltri, log2f, preferred_element_type=F32)
            b = cs[:, :HEAD_DIM] + cs[:, HEAD_DIM:2 * HEAD_DIM] + cs[:, 2 * HEAD_DIM:]
            b_last = b[C - 1:C, :]

            st = st_ref[hh]
            o = lax.dot_general((qs * jnp.exp2(b)).astype(BF16), st.astype(BF16), NT_DIMS,
                                preferred_element_type=F32)

            blocks = []
            for i in range(C // c):
                rows = slice(i * c, (i + 1) * c)
                q_i, b_i = qs[rows], b[rows]
                blk = jnp.zeros((c, C), F32)
                for s in range(i * c, (i + 1) * c):
                    w = q_i * kk[s:s + 1, :] * jnp.exp2(b_i - b[s:s + 1, :])
                    blk = jnp.where(lane == s, jnp.sum(w, axis=-1, keepdims=True), blk)
                blocks.append(blk)
            a = jnp.where(level == 0, jnp.concatenate(blocks, axis=0), 0.0)
            for l, m in enumerate(sizes):
                edge = jnp.concatenate(
                    [jnp.broadcast_to(b[j * m + m // 2 - 1:j * m + m // 2, :], (m, HEAD_DIM)) for j in range(C // m)],
                    axis=0)
                x = jnp.exp2(-jnp.abs(b - edge))
                pair = lax.dot_general((qs * x).astype(BF16), (kk * x).astype(BF16), NT_DIMS,
                                       preferred_element_type=F32)
                a = jnp.where(level == l + 1, pair, a)
            o = o + jnp.dot(a.astype(BF16), v, preferred_element_type=F32)

            k_d = (kk * jnp.exp2(b_last - b)).astype(BF16)
            st_ref[hh] = st * jnp.exp2(b_last) + lax.dot_general(v, k_d, TN_DIMS, preferred_element_type=F32)

            ms = jnp.mean(o * o, axis=-1, keepdims=True)
            on = o * lax.rsqrt(ms + RMS_EPS) * gn_ref[...]
            o_ref[pl.ds(r0, C), cols] = (on * (gate * _sigmoid(gate))).astype(o_ref.dtype)
        return carry

    lax.fori_loop(0, chunks, chunk_body, 0)


def _hgrn2(proj, lb_logits, gn, layer, batch, seq, n_heads, col0, heads=2, rows=512):
    heads = _tile(n_heads, heads)
    rows = _tile(seq, rows)
    assert rows % HGRN_CHUNK == 0 and col0 % heads == 0
    nt = seq // rows
    w = heads * HEAD_DIM
    hblk = n_heads // heads

    def in_spec(group):
        return pl.BlockSpec((rows, w), lambda b, h, t: (b * nt + t, col0 // heads + group * hblk + h))

    return pl.pallas_call(
        functools.partial(_hgrn_kernel, layer=layer, heads=heads, chunks=rows // HGRN_CHUNK),
        grid=(batch, hblk, nt),
        in_specs=[in_spec(0), in_spec(1), in_spec(2), in_spec(3),
                  pl.BlockSpec((lb_logits.shape[0], w), lambda b, h, t: (0, h)),
                  pl.BlockSpec((1, HEAD_DIM), lambda b, h, t: (0, 0))],
        out_specs=pl.BlockSpec((rows, w), lambda b, h, t: (b * nt + t, h)),
        out_shape=jax.ShapeDtypeStruct((batch * seq, n_heads * HEAD_DIM), BF16),
        scratch_shapes=[pltpu.VMEM((heads, HEAD_DIM, HEAD_DIM), F32)],
        compiler_params=_params("parallel", "parallel", "arbitrary"),
        name="hgrn2",
    )(proj, proj, proj, proj, lb_logits, gn)


def _ple_embed_kernel(p_ref, w_ref, g_ref, o_ref):
    e = jnp.dot(p_ref[...].astype(BF16), w_ref[...], preferred_element_type=F32)
    ms = jnp.mean(e * e, axis=-1, keepdims=True)
    o_ref[...] = (e * lax.rsqrt(ms + RMS_EPS) * g_ref[...]).astype(o_ref.dtype)


def _ple_embed(p, w, g):
    m, kd = p.shape
    n = w.shape[1]
    tm = _tile(m, 256)
    return pl.pallas_call(
        _ple_embed_kernel,
        grid=(m // tm,),
        in_specs=[pl.BlockSpec((tm, kd), lambda i: (i, 0)), pl.BlockSpec((kd, n), lambda i: (0, 0)),
                  pl.BlockSpec((1, n), lambda i: (0, 0))],
        out_specs=pl.BlockSpec((tm, n), lambda i: (i, 0)),
        out_shape=jax.ShapeDtypeStruct((m, n), BF16),
        compiler_params=_params("parallel"),
        name="ple_embed",
    )(p, w, g.reshape(1, n).astype(F32))


def _ple_gate_kernel(a_ref, w_ref, e_ref, h_ref, o_ref):
    gate = _sigmoid(jnp.dot(a_ref[...], w_ref[...], preferred_element_type=F32))
    o_ref[...] = h_ref[...] + gate * e_ref[...]


def _ple_gate(a, w, e, h, tm=1024, tn=512):
    m, kd = a.shape
    n = w.shape[1]
    tm, tn = _tile(m, tm), _tile(n, tn)
    return pl.pallas_call(
        _ple_gate_kernel,
        grid=(m // tm, n // tn),
        in_specs=[pl.BlockSpec((tm, kd), lambda i, j: (i, 0)), pl.BlockSpec((kd, tn), lambda i, j: (0, j)),
                  pl.BlockSpec((tm, tn), lambda i, j: (i, j)), pl.BlockSpec((tm, tn), lambda i, j: (i, j))],
        out_specs=pl.BlockSpec((tm, tn), lambda i, j: (i, j)),
        out_shape=jax.ShapeDtypeStruct((m, n), F32),
        compiler_params=_params("parallel", "arbitrary"),
        name="ple_gate",
    )(a, w, e, h)


def _layer(h, p, layer, batch, seq, norm_mix_g, w_in, fox_f_bias, fox_q_norm_g, fox_k_norm_g, hgrn_lb_logits,
           hgrn_norm_g, w_out, norm_mlp_g, w_up, w_down, ple_norm_g, w_ple_gate, w_ple_proj, ple_post_g):
    d = h.shape[1]
    d_a = d_b = d // 2
    n_a, n_b = d_a // HEAD_DIM, d_b // HEAD_DIM
    assert n_a <= HEAD_DIM and w_in.shape[1] == 3 * d_a + n_a + 4 * d_b

    w_main = jnp.concatenate([w_in[:, :3 * d_a], w_in[:, 3 * d_a + n_a:]], axis=1).astype(BF16)
    w_fa = jnp.pad(w_in[:, 3 * d_a:3 * d_a + n_a], ((0, 0), (0, HEAD_DIM - n_a))).astype(BF16)
    q_gain = (fox_q_norm_g.astype(F32) * (LOG2E / math.sqrt(HEAD_DIM))).reshape(1, HEAD_DIM)
    k_gain = fox_k_norm_g.astype(F32).reshape(1, HEAD_DIM)

    u = _rmsnorm(h, norm_mix_g)
    proj, f_logit = _inproj(u, w_main, w_fa, q_gain, k_gain, d_a)

    f_bias = jnp.pad(fox_f_bias.astype(F32), (0, HEAD_DIM - n_a)).reshape(1, HEAD_DIM)
    out_a = _fox_attention(proj, _forget_bias_tiles(f_logit, f_bias, batch, seq, n_a), batch, seq, n_a)

    out_b = _hgrn2(proj, hgrn_lb_logits.astype(F32), hgrn_norm_g.astype(F32).reshape(1, HEAD_DIM), layer,
                   batch, seq, n_b, col0=3 * n_a)

    h = _outproj(out_a, out_b, w_out.astype(BF16), h)

    um = _rmsnorm(h, norm_mlp_g)
    hid = _matmul(um, w_up.astype(BF16), BF16, act="relu2", name="mlp_up")
    h = _matmul_residual(hid, w_down.astype(BF16), h)

    e = _ple_embed(p, w_ple_proj.astype(BF16), ple_post_g)
    return _ple_gate(_rmsnorm(h, ple_norm_g), w_ple_gate.astype(BF16), e, h)


def kernel(x, p, norm_mix_g, w_in, fox_f_bias, fox_q_norm_g, fox_k_norm_g, hgrn_lb_logits, hgrn_norm_g, w_out,
           norm_mlp_g, w_up, w_down, ple_norm_g, w_ple_gate, w_ple_proj, ple_post_g):
    batch, seq, d = x.shape
    h = x.reshape(batch * seq, d)
    for i in range(w_in.shape[0]):
        h = _layer(h, p[i].reshape(batch * seq, -1), i, batch, seq, norm_mix_g[i], w_in[i], fox_f_bias[i],
                   fox_q_norm_g[i], fox_k_norm_g[i], hgrn_lb_logits, hgrn_norm_g[i], w_out[i], norm_mlp_g[i],
                   w_up[i], w_down[i], ple_norm_g[i], w_ple_gate[i], w_ple_proj[i], ple_post_g[i])
    return h.reshape(batch, seq, d)
```

```python
import functools
import math

import jax
import jax.numpy as jnp
from jax import lax
from jax.experimental import pallas as pl
from jax.experimental.pallas import tpu as pltpu

F32, BF16 = jnp.float32, jnp.bfloat16
HEAD_DIM = 128
RMS_EPS = 1e-6
LOG2E = math.log2(math.e)
HGRN_CHUNK = 128
HGRN_SUB = 8
V7X_VMEM_LIMIT = 56 * 1024 * 1024
V7X_VMEM_LIMIT_HIGH = 60 * 1024 * 1024

NT_DIMS = (((1,), (1,)), ((), ()))
TN_DIMS = (((0,), (0,)), ((), ()))


def _params(*semantics, vmem_limit=V7X_VMEM_LIMIT):
    return pltpu.CompilerParams(dimension_semantics=semantics, vmem_limit_bytes=vmem_limit)


def _tile(dim, pref):
    t = min(dim, pref)
    assert dim % t == 0, (dim, pref)
    return t


def _sigmoid(x):
    return 1.0 / (1.0 + jnp.exp(-x))


def _rmsnorm_kernel(x_ref, g_ref, o_ref):
    x = x_ref[...]
    ms = jnp.mean(x * x, axis=-1, keepdims=True)
    o_ref[...] = (x * lax.rsqrt(ms + RMS_EPS) * g_ref[...]).astype(o_ref.dtype)


def _rmsnorm(x, g, out_dtype=BF16):
    m, d = x.shape
    tm = _tile(m, 256)
    return pl.pallas_call(
        _rmsnorm_kernel,
        grid=(m // tm,),
        in_specs=[pl.BlockSpec((tm, d), lambda i: (i, 0)), pl.BlockSpec((1, d), lambda i: (0, 0))],
        out_specs=pl.BlockSpec((tm, d), lambda i: (i, 0)),
        out_shape=jax.ShapeDtypeStruct((m, d), out_dtype),
        compiler_params=_params("parallel"),
        name="rmsnorm",
    )(x, g.reshape(1, d).astype(F32))


def _mm_kernel(a_ref, w_ref, o_ref, *, act):
    acc = jnp.dot(a_ref[...], w_ref[...], preferred_element_type=F32)
    if act == "relu2":
        acc = jnp.square(jnp.maximum(acc, 0.0))
    o_ref[...] = acc.astype(o_ref.dtype)


def _matmul(a, w, out_dtype, act=None, tm=1024, tn=1024, name="matmul"):
    m, k = a.shape
    n = w.shape[1]
    tm, tn = _tile(m, tm), _tile(n, tn)
    return pl.pallas_call(
        functools.partial(_mm_kernel, act=act),
        grid=(m // tm, n // tn),
        in_specs=[pl.BlockSpec((tm, k), lambda i, j: (i, 0)), pl.BlockSpec((k, tn), lambda i, j: (0, j))],
        out_specs=pl.BlockSpec((tm, tn), lambda i, j: (i, j)),
        out_shape=jax.ShapeDtypeStruct((m, n), out_dtype),
        compiler_params=_params("parallel", "arbitrary"),
        name=name,
    )(a, w)


def _inproj_kernel(u_ref, w_ref, wf_ref, o_ref, f_ref):
    @pl.when(pl.program_id(1) == 0)
    def _():
        f_ref[...] = jnp.dot(u_ref[...], wf_ref[...], preferred_element_type=F32)

    o_ref[...] = jnp.dot(u_ref[...], w_ref[...], preferred_element_type=F32).astype(o_ref.dtype)


def _inproj(u, w, wf, tm=1024, tn=1024):
    m, k = u.shape
    n, nf = w.shape[1], wf.shape[1]
    tm, tn = _tile(m, tm), _tile(n, tn)
    return pl.pallas_call(
        _inproj_kernel,
        grid=(m // tm, n // tn),
        in_specs=[pl.BlockSpec((tm, k), lambda i, j: (i, 0)), pl.BlockSpec((k, tn), lambda i, j: (0, j)),
                  pl.BlockSpec((k, nf), lambda i, j: (0, 0))],
        out_specs=[pl.BlockSpec((tm, tn), lambda i, j: (i, j)), pl.BlockSpec((tm, nf), lambda i, j: (i, 0))],
        out_shape=[jax.ShapeDtypeStruct((m, n), BF16), jax.ShapeDtypeStruct((m, nf), F32)],
        compiler_params=_params("parallel", "arbitrary"),
        name="inproj_fox",
    )(u, w, wf)


def _outproj_kernel(a_ref, b_ref, wa_ref, wb_ref, x_ref, o_ref):
    acc = jnp.dot(a_ref[...], wa_ref[...], preferred_element_type=F32)
    acc = acc + jnp.dot(b_ref[...], wb_ref[...], preferred_element_type=F32)
    o_ref[...] = x_ref[...] + acc


def _outproj(a, b, w, x, tm=1024, tn=512):
    m, ka = a.shape
    kb = b.shape[1]
    n = w.shape[1]
    assert ka == kb and w.shape[0] == ka + kb
    tm, tn = _tile(m, tm), _tile(n, tn)
    return pl.pallas_call(
        _outproj_kernel,
        grid=(m // tm, n // tn),
        in_specs=[pl.BlockSpec((tm, ka), lambda i, j: (i, 0)), pl.BlockSpec((tm, kb), lambda i, j: (i, 0)),
                  pl.BlockSpec((ka, tn), lambda i, j: (0, j)), pl.BlockSpec((kb, tn), lambda i, j: (1, j)),
                  pl.BlockSpec((tm, tn), lambda i, j: (i, j))],
        out_specs=pl.BlockSpec((tm, tn), lambda i, j: (i, j)),
        out_shape=jax.ShapeDtypeStruct((m, n), F32),
        compiler_params=_params("parallel", "arbitrary"),
        name="outproj",
    )(a, b, w, w, x)


def _mm_res_kernel(a_ref, w_ref, r_ref, o_ref):
    k = pl.program_id(2)
    acc = jnp.dot(a_ref[...], w_ref[...], preferred_element_type=F32)

    @pl.when(k == 0)
    def _():
        o_ref[...] = r_ref[...] + acc

    @pl.when(k > 0)
    def _():
        o_ref[...] += acc


def _matmul_residual(a, w, r, tm=1024, tn=1024, tk=4096):
    m, kd = a.shape
    n = w.shape[1]
    tm, tn, tk = _tile(m, tm), _tile(n, tn), _tile(kd, tk)
    return pl.pallas_call(
        _mm_res_kernel,
        grid=(m // tm, n // tn, kd // tk),
        in_specs=[pl.BlockSpec((tm, tk), lambda i, j, k: (i, k)), pl.BlockSpec((tk, tn), lambda i, j, k: (k, j)),
                  pl.BlockSpec((tm, tn), lambda i, j, k: (i, j))],
        out_specs=pl.BlockSpec((tm, tn), lambda i, j, k: (i, j)),
        out_shape=jax.ShapeDtypeStruct((m, n), F32),
        compiler_params=_params("parallel", "parallel", "arbitrary", vmem_limit=V7X_VMEM_LIMIT_HIGH),
        name="mlp_down",
    )(a, w, r)


def _split3(x):
    hi = x.astype(BF16).astype(F32)
    r = x - hi
    mid = r.astype(BF16).astype(F32)
    return hi, mid, r - mid


def _fcum_kernel(f_ref, bias_ref, o_ref, carry_ref, *, n_heads):
    t = pl.program_id(1)

    @pl.when(t == 0)
    def _():
        carry_ref[...] = jnp.zeros_like(carry_ref)

    x = f_ref[...] + bias_ref[...]
    ls = jnp.minimum(x, 0.0) - jnp.log1p(jnp.exp(-jnp.abs(x)))
    tc = x.shape[0]
    r = lax.broadcasted_iota(jnp.int32, (tc, tc), 0)
    c = lax.broadcasted_iota(jnp.int32, (tc, tc), 1)
    tri = jnp.where(c <= r, 1.0, 0.0).astype(F32)
    cs = jnp.dot(tri, ls, preferred_element_type=F32, precision=lax.Precision.HIGHEST) + carry_ref[...]
    carry_ref[...] = cs[tc - 1:tc, :]
    lane = lax.broadcasted_iota(jnp.int32, (tc, HEAD_DIM), 1)
    for h in range(n_heads):
        hi, mid, lo = _split3(jnp.broadcast_to(cs[:, h:h + 1] * (-LOG2E), (tc, HEAD_DIM)))
        tile = jnp.where(lane == 0, hi, jnp.where(lane == 1, mid, jnp.where(lane == 2, lo, 0.0)))
        o_ref[:, h * HEAD_DIM:(h + 1) * HEAD_DIM] = tile.astype(o_ref.dtype)


def _forget_bias_tiles(f_logit, bias, batch, seq, n_heads):
    tc = _tile(seq, 512)
    nt = seq // tc
    return pl.pallas_call(
        functools.partial(_fcum_kernel, n_heads=n_heads),
        grid=(batch, nt),
        in_specs=[pl.BlockSpec((tc, HEAD_DIM), lambda b, t: (b * nt + t, 0)),
                  pl.BlockSpec((1, HEAD_DIM), lambda b, t: (0, 0))],
        out_specs=pl.BlockSpec((tc, n_heads * HEAD_DIM), lambda b, t: (b * nt + t, 0)),
        out_shape=jax.ShapeDtypeStruct((batch * seq, n_heads * HEAD_DIM), BF16),
        scratch_shapes=[pltpu.VMEM((1, HEAD_DIM), F32)],
        compiler_params=_params("parallel", "arbitrary"),
        name="fox_forget_cumsum",
    )(f_logit, bias)


def _head_rmsnorm(x, g):
    x = x.astype(F32)
    ms = jnp.mean(x * x, axis=-1, keepdims=True)
    return (x * lax.rsqrt(ms + RMS_EPS) * g).astype(BF16)


def _attn_kernel(q_ref, k_ref, kb_ref, v_ref, gq_ref, gk_ref, o_ref, kp_ref, s_ref, m_ref, l_ref, acc_ref, *,
                 blk, heads):
    qi = pl.program_id(2)
    hcols = [slice(hh * HEAD_DIM, (hh + 1) * HEAD_DIM) for hh in range(heads)]
    seq = k_ref.shape[0]

    @pl.when(qi == 0)
    def _():
        def fill(c, carry):
            rows = pl.ds(pl.multiple_of(c * blk, blk), blk)
            for hh, cols in enumerate(hcols):
                kp_ref[hh, rows, :HEAD_DIM] = _head_rmsnorm(k_ref[rows, cols], gk_ref[...])
                kp_ref[hh, rows, HEAD_DIM:] = kb_ref[rows, cols]
            return carry
        lax.fori_loop(0, seq // blk, fill, 0)

    lane = lax.broadcasted_iota(jnp.int32, (blk, HEAD_DIM), 1)
    q_ones = jnp.where(lane < 3, 1.0, 0.0).astype(BF16)
    key_pos = lax.broadcasted_iota(jnp.int32, (blk, blk), 0)
    qry_pos = lax.broadcasted_iota(jnp.int32, (blk, blk), 1)

    for hh, cols in enumerate(hcols):
        q2 = jnp.concatenate([_head_rmsnorm(q_ref[:, cols], gq_ref[...]), q_ones], axis=1)

        def scores(j, slot):
            rows = pl.ds(pl.multiple_of(j * blk, blk), blk)
            s_ref[slot] = lax.dot_general(kp_ref[hh, rows, :], q2, NT_DIMS, preferred_element_type=F32)

        def update(j, slot, diagonal=False):
            rows = pl.ds(pl.multiple_of(j * blk, blk), blk)
            s = s_ref[slot]
            if diagonal:
                s = jnp.where(key_pos <= qry_pos, s, -jnp.inf)
                m_new = jnp.max(s, axis=0, keepdims=True)
            else:
                m = m_ref[...]
                m_new = jnp.maximum(m, jnp.max(s, axis=0, keepdims=True))
            p = jnp.exp2(s - m_new)
            pv = lax.dot_general(v_ref[rows, cols], p.astype(BF16), TN_DIMS, preferred_element_type=F32)
            m_ref[...] = m_new
            if diagonal:
                l_ref[...] = jnp.sum(p, axis=0, keepdims=True)
                acc_ref[...] = pv
            else:
                alpha = jnp.exp2(m - m_new)
                l_ref[...] = alpha * l_ref[...] + jnp.sum(p, axis=0, keepdims=True)
                acc_ref[...] = alpha * acc_ref[...] + pv

        def pair(t0):
            scores(t0 + 1, 1)
            update(t0, 0)
            scores(t0 + 2, 0)
            update(t0 + 1, 1)

        def quad(k, carry):
            pair(4 * k)
            pair(4 * k + 2)
            return carry

        scores(qi, 1)
        scores(0, 0)
        update(qi, 1, diagonal=True)
        lax.fori_loop(0, qi // 4, quad, 0)

        @pl.when((qi // 2) % 2 == 1)
        def _():
            pair(4 * (qi // 4))

        @pl.when(qi % 2 == 1)
        def _():
            update(qi - 1, 0)

        o_ref[:, cols] = (acc_ref[...] / l_ref[...]).T.astype(o_ref.dtype)


def _fox_attention(proj, kbias, gq, gk, batch, seq, n_heads, blk=512, heads=2):
    blk = _tile(seq, blk)
    heads = _tile(n_heads, heads)
    nq = seq // blk
    hblk = n_heads // heads
    w = heads * HEAD_DIM
    gain = pl.BlockSpec((1, HEAD_DIM), lambda b, h, i: (0, 0))
    return pl.pallas_call(
        functools.partial(_attn_kernel, blk=blk, heads=heads),
        grid=(batch, hblk, nq),
        in_specs=[pl.BlockSpec((blk, w), lambda b, h, i: (b * nq + i, h)),
                  pl.BlockSpec((seq, w), lambda b, h, i: (b, hblk + h)),
                  pl.BlockSpec((seq, w), lambda b, h, i: (b, h)),
                  pl.BlockSpec((seq, w), lambda b, h, i: (b, 2 * hblk + h)), gain, gain],
        out_specs=pl.BlockSpec((blk, w), lambda b, h, i: (b * nq + i, h)),
        out_shape=jax.ShapeDtypeStruct((batch * seq, n_heads * HEAD_DIM), BF16),
        scratch_shapes=[pltpu.VMEM((heads, seq, 2 * HEAD_DIM), BF16), pltpu.VMEM((2, blk, blk), F32),
                        pltpu.VMEM((1, blk), F32), pltpu.VMEM((1, blk), F32), pltpu.VMEM((HEAD_DIM, blk), F32)],
        compiler_params=_params("parallel", "parallel", "arbitrary"),
        name="fox_attention",
    )(proj, proj, kbias, proj, gq, gk)


def _hgrn_kernel(q_ref, z_ref, v_ref, gate_ref, lbl_ref, gn_ref, o_ref, st_ref, rb_ref, *, layer, heads, chunks):
    C, c = HGRN_CHUNK, HGRN_SUB
    t = pl.program_id(2)

    @pl.when(t == 0)
    def _():
        st_ref[...] = jnp.zeros_like(st_ref)

    logits = lbl_ref[...]
    e = jnp.exp(logits - jnp.max(logits, axis=0, keepdims=True))
    lb_all = jnp.sum(e[:layer + 1], axis=0, keepdims=True) / jnp.sum(e, axis=0, keepdims=True)

    row = lax.broadcasted_iota(jnp.int32, (C, C), 0)
    col = lax.broadcasted_iota(jnp.int32, (C, C), 1)
    ltri = jnp.where(col <= row, 1.0, 0.0).astype(BF16)
    sizes = [c << l for l in range(1, (C // c).bit_length())]
    xor = row ^ col
    level = jnp.zeros((C, C), jnp.int32)
    for l, m in enumerate(sizes):
        level = jnp.where(xor >= m // 2, l + 1, level)
    level = jnp.where(col > row, -1, level)
    lane = lax.broadcasted_iota(jnp.int32, (c, C), 1)

    def chunk_body(ci, carry):
        r0 = pl.multiple_of(ci * C, C)
        for hh in range(heads):
            cols = slice(hh * HEAD_DIM, (hh + 1) * HEAD_DIM)
            lb = lb_all[:, cols]
            q = q_ref[pl.ds(r0, C), cols].astype(F32)
            z = z_ref[pl.ds(r0, C), cols].astype(F32)
            v = v_ref[pl.ds(r0, C), cols]
            gate = gate_ref[pl.ds(r0, C), cols].astype(F32)

            ez = jnp.exp(-jnp.abs(z))
            inv = 1.0 / (1.0 + ez)
            sig_z = jnp.where(z >= 0, 1.0, ez) * inv
            sig_mz = jnp.where(z >= 0, ez, 1.0) * inv
            kk = (1.0 - lb) * sig_mz
            qs = q * _sigmoid(q)
            log2f = jnp.concatenate(_split3(jnp.log2(lb + (1.0 - lb) * sig_z)), axis=1).astype(BF16)
            cs = jnp.dot(ltri, log2f, preferred_element_type=F32)
            b = cs[:, :HEAD_DIM] + cs[:, HEAD_DIM:2 * HEAD_DIM] + cs[:, 2 * HEAD_DIM:]
            b_last = b[C - 1:C, :]

            st = st_ref[hh]
            o = lax.dot_general((qs * jnp.exp2(b)).astype(BF16), st.astype(BF16), NT_DIMS,
                                preferred_element_type=F32)

            rb_ref[2 * hh] = kk
            rb_ref[2 * hh + 1] = b
            blocks = []
            for i in range(C // c):
                rows = slice(i * c, (i + 1) * c)
                q_i, b_i = qs[rows], b[rows]
                blk = jnp.zeros((c, C), F32)
                for s in range(i * c, (i + 1) * c):
                    w = q_i * rb_ref[2 * hh, s:s + 1, :] * jnp.exp2(b_i - rb_ref[2 * hh + 1, s:s + 1, :])
                    blk = jnp.where(lane == s, jnp.sum(w, axis=-1, keepdims=True), blk)
                blocks.append(blk)
            a = jnp.where(level == 0, jnp.concatenate(blocks, axis=0), 0.0)
            for l, m in enumerate(sizes):
                edge = jnp.concatenate(
                    [jnp.broadcast_to(b[j * m + m // 2 - 1:j * m + m // 2, :], (m, HEAD_DIM)) for j in range(C // m)],
                    axis=0)
                x = jnp.exp2(-jnp.abs(b - edge))
                pair = lax.dot_general((qs * x).astype(BF16), (kk * x).astype(BF16), NT_DIMS,
                                       preferred_element_type=F32)
                a = jnp.where(level == l + 1, pair, a)
            o = o + jnp.dot(a.astype(BF16), v, preferred_element_type=F32)

            k_d = (kk * jnp.exp2(b_last - b)).astype(BF16)
            st_ref[hh] = st * jnp.exp2(b_last) + lax.dot_general(v, k_d, TN_DIMS, preferred_element_type=F32)

            ms = jnp.mean(o * o, axis=-1, keepdims=True)
            on = o * lax.rsqrt(ms + RMS_EPS) * gn_ref[...]
            o_ref[pl.ds(r0, C), cols] = (on * (gate * _sigmoid(gate))).astype(o_ref.dtype)
        return carry

    lax.fori_loop(0, chunks, chunk_body, 0)


def _hgrn2(proj, lb_logits, gn, layer, batch, seq, n_heads, col0, heads=4, rows=512):
    heads = _tile(n_heads, heads)
    rows = _tile(seq, rows)
    assert rows % HGRN_CHUNK == 0 and col0 % heads == 0
    nt = seq // rows
    w = heads * HEAD_DIM
    hblk = n_heads // heads

    def in_spec(group):
        return pl.BlockSpec((rows, w), lambda b, h, t: (b * nt + t, col0 // heads + group * hblk + h))

    return pl.pallas_call(
        functools.partial(_hgrn_kernel, layer=layer, heads=heads, chunks=rows // HGRN_CHUNK),
        grid=(batch, hblk, nt),
        in_specs=[in_spec(0), in_spec(1), in_spec(2), in_spec(3),
                  pl.BlockSpec((lb_logits.shape[0], w), lambda b, h, t: (0, h)),
                  pl.BlockSpec((1, HEAD_DIM), lambda b, h, t: (0, 0))],
        out_specs=pl.BlockSpec((rows, w), lambda b, h, t: (b * nt + t, h)),
        out_shape=jax.ShapeDtypeStruct((batch * seq, n_heads * HEAD_DIM), BF16),
        scratch_shapes=[pltpu.VMEM((heads, HEAD_DIM, HEAD_DIM), F32),
                        pltpu.VMEM((2 * heads, HGRN_CHUNK, HEAD_DIM), F32)],
        compiler_params=_params("parallel", "parallel", "arbitrary"),
        name="hgrn2",
    )(proj, proj, proj, proj, lb_logits, gn)


def _ple_embed_kernel(p_ref, w_ref, g_ref, o_ref):
    e = jnp.dot(p_ref[...].astype(BF16), w_ref[...], preferred_element_type=F32)
    ms = jnp.mean(e * e, axis=-1, keepdims=True)
    o_ref[...] = (e * lax.rsqrt(ms + RMS_EPS) * g_ref[...]).astype(o_ref.dtype)


def _ple_embed(p, w, g):
    m, kd = p.shape
    n = w.shape[1]
    tm = _tile(m, 256)
    return pl.pallas_call(
        _ple_embed_kernel,
        grid=(m // tm,),
        in_specs=[pl.BlockSpec((tm, kd), lambda i: (i, 0)), pl.BlockSpec((kd, n), lambda i: (0, 0)),
                  pl.BlockSpec((1, n), lambda i: (0, 0))],
        out_specs=pl.BlockSpec((tm, n), lambda i: (i, 0)),
        out_shape=jax.ShapeDtypeStruct((m, n), BF16),
        compiler_params=_params("parallel"),
        name="ple_embed",
    )(p, w, g.reshape(1, n).astype(F32))


def _ple_gate_kernel(a_ref, w_ref, e_ref, h_ref, o_ref):
    gate = _sigmoid(jnp.dot(a_ref[...], w_ref[...], preferred_element_type=F32))
    o_ref[...] = h_ref[...] + gate * e_ref[...]


def _ple_gate(a, w, e, h, tm=1024, tn=512):
    m, kd = a.shape
    n = w.shape[1]
    tm, tn = _tile(m, tm), _tile(n, tn)
    return pl.pallas_call(
        _ple_gate_kernel,
        grid=(m // tm, n // tn),
        in_specs=[pl.BlockSpec((tm, kd), lambda i, j: (i, 0)), pl.BlockSpec((kd, tn), lambda i, j: (0, j)),
                  pl.BlockSpec((tm, tn), lambda i, j: (i, j)), pl.BlockSpec((tm, tn), lambda i, j: (i, j))],
        out_specs=pl.BlockSpec((tm, tn), lambda i, j: (i, j)),
        out_shape=jax.ShapeDtypeStruct((m, n), F32),
        compiler_params=_params("parallel", "arbitrary"),
        name="ple_gate",
    )(a, w, e, h)


def _layer(h, p, layer, batch, seq, norm_mix_g, w_in, fox_f_bias, fox_q_norm_g, fox_k_norm_g, hgrn_lb_logits,
           hgrn_norm_g, w_out, norm_mlp_g, w_up, w_down, ple_norm_g, w_ple_gate, w_ple_proj, ple_post_g):
    d = h.shape[1]
    d_a = d_b = d // 2
    n_a, n_b = d_a // HEAD_DIM, d_b // HEAD_DIM
    assert n_a <= HEAD_DIM and w_in.shape[1] == 3 * d_a + n_a + 4 * d_b

    w_fox = w_in[:, :3 * d_a].astype(BF16)
    w_fa = jnp.pad(w_in[:, 3 * d_a:3 * d_a + n_a], ((0, 0), (0, HEAD_DIM - n_a))).astype(BF16)
    w_hgrn = w_in[:, 3 * d_a + n_a:].astype(BF16)
    q_gain = (fox_q_norm_g.astype(F32) * (LOG2E / math.sqrt(HEAD_DIM))).reshape(1, HEAD_DIM)
    k_gain = fox_k_norm_g.astype(F32).reshape(1, HEAD_DIM)

    u = _rmsnorm(h, norm_mix_g)
    proj_a, f_logit = _inproj(u, w_fox, w_fa)
    proj_b = _matmul(u, w_hgrn, BF16, name="inproj_hgrn")

    f_bias = jnp.pad(fox_f_bias.astype(F32), (0, HEAD_DIM - n_a)).reshape(1, HEAD_DIM)
    kbias = _forget_bias_tiles(f_logit, f_bias, batch, seq, n_a)
    out_a = _fox_attention(proj_a, kbias, q_gain, k_gain, batch, seq, n_a)

    out_b = _hgrn2(proj_b, hgrn_lb_logits.astype(F32), hgrn_norm_g.astype(F32).reshape(1, HEAD_DIM), layer,
                   batch, seq, n_b, col0=0)

    h = _outproj(out_a, out_b, w_out.astype(BF16), h)

    um = _rmsnorm(h, norm_mlp_g)
    hid = _matmul(um, w_up.astype(BF16), BF16, act="relu2", name="mlp_up")
    h = _matmul_residual(hid, w_down.astype(BF16), h)

    e = _ple_embed(p, w_ple_proj.astype(BF16), ple_post_g)
    return _ple_gate(_rmsnorm(h, ple_norm_g), w_ple_gate.astype(BF16), e, h)


def kernel(x, p, norm_mix_g, w_in, fox_f_bias, fox_q_norm_g, fox_k_norm_g, hgrn_lb_logits, hgrn_norm_g, w_out,
           norm_mlp_g, w_up, w_down, ple_norm_g, w_ple_gate, w_ple_proj, ple_post_g):
    batch, seq, d = x.shape
    h = x.reshape(batch * seq, d)
    for i in range(w_in.shape[0]):
        h = _layer(h, p[i].reshape(batch * seq, -1), i, batch, seq, norm_mix_g[i], w_in[i], fox_f_bias[i],
                   fox_q_norm_g[i], fox_k_norm_g[i], hgrn_lb_logits, hgrn_norm_g[i], w_out[i], norm_mlp_g[i],
                   w_up[i], w_down[i], ple_norm_g[i], w_ple_gate[i], w_ple_proj[i], ple_post_g[i])
    return h.reshape(batch, seq, d)
```

```python
import functools
import math

import jax
import jax.numpy as jnp
from jax import lax
from jax.experimental import pallas as pl
from jax.experimental.pallas import tpu as pltpu

F32, BF16 = jnp.float32, jnp.bfloat16
HEAD_DIM = 128
RMS_EPS = 1e-6
LOG2E = math.log2(math.e)
HGRN_CHUNK = 128
HGRN_SUB = 8
V7X_VMEM_LIMIT = 56 * 1024 * 1024
V7X_VMEM_LIMIT_HIGH = 60 * 1024 * 1024

NT_DIMS = (((1,), (1,)), ((), ()))
TN_DIMS = (((0,), (0,)), ((), ()))


def _params(*semantics, vmem_limit=V7X_VMEM_LIMIT):
    return pltpu.CompilerParams(dimension_semantics=semantics, vmem_limit_bytes=vmem_limit)


def _tile(dim, pref):
    t = min(dim, pref)
    assert dim % t == 0, (dim, pref)
    return t


def _sigmoid(x):
    return 1.0 / (1.0 + jnp.exp(-x))


def _rmsnorm_kernel(x_ref, g_ref, o_ref):
    x = x_ref[...]
    ms = jnp.mean(x * x, axis=-1, keepdims=True)
    o_ref[...] = (x * lax.rsqrt(ms + RMS_EPS) * g_ref[...]).astype(o_ref.dtype)


def _rmsnorm(x, g, out_dtype=BF16):
    m, d = x.shape
    tm = _tile(m, 256)
    return pl.pallas_call(
        _rmsnorm_kernel,
        grid=(m // tm,),
        in_specs=[pl.BlockSpec((tm, d), lambda i: (i, 0)), pl.BlockSpec((1, d), lambda i: (0, 0))],
        out_specs=pl.BlockSpec((tm, d), lambda i: (i, 0)),
        out_shape=jax.ShapeDtypeStruct((m, d), out_dtype),
        compiler_params=_params("parallel"),
        name="rmsnorm",
    )(x, g.reshape(1, d).astype(F32))


def _row_scale(part_ref, d):
    return lax.rsqrt(jnp.sum(part_ref[...], axis=-1, keepdims=True) * (1.0 / d) + RMS_EPS)


def _fold_sq(x):
    sq = x * x
    out = sq[:, :HEAD_DIM]
    for t in range(1, x.shape[1] // HEAD_DIM):
        out = out + sq[:, t * HEAD_DIM:(t + 1) * HEAD_DIM]
    return out


def _mm_kernel(a_ref, w_ref, o_ref, *, act):
    acc = jnp.dot(a_ref[...], w_ref[...], preferred_element_type=F32)
    if act == "relu2":
        acc = jnp.square(jnp.maximum(acc, 0.0))
    o_ref[...] = acc.astype(o_ref.dtype)


def _mm_scaled_kernel(a_ref, w_ref, part_ref, o_ref, rs_ref, *, act):
    @pl.when(pl.program_id(1) == 0)
    def _():
        rs_ref[...] = _row_scale(part_ref, a_ref.shape[1])

    acc = jnp.dot(a_ref[...], w_ref[...], preferred_element_type=F32) * rs_ref[...]
    if act == "relu2":
        acc = jnp.square(jnp.maximum(acc, 0.0))
    o_ref[...] = acc.astype(o_ref.dtype)


def _matmul(a, w, out_dtype, act=None, part=None, tm=1024, tn=1024, name="matmul"):
    m, k = a.shape
    n = w.shape[1]
    tm, tn = _tile(m, tm), _tile(n, tn)
    in_specs = [pl.BlockSpec((tm, k), lambda i, j: (i, 0)), pl.BlockSpec((k, tn), lambda i, j: (0, j))]
    if part is None:
        body, operands, scratch = functools.partial(_mm_kernel, act=act), (a, w), []
    else:
        body, operands = functools.partial(_mm_scaled_kernel, act=act), (a, w, part)
        in_specs.append(pl.BlockSpec((tm, part.shape[1]), lambda i, j: (i, 0)))
        scratch = [pltpu.VMEM((tm, 1), F32)]
    return pl.pallas_call(
        body,
        grid=(m // tm, n // tn),
        in_specs=in_specs,
        out_specs=pl.BlockSpec((tm, tn), lambda i, j: (i, j)),
        out_shape=jax.ShapeDtypeStruct((m, n), out_dtype),
        scratch_shapes=scratch,
        compiler_params=_params("parallel", "arbitrary"),
        name=name,
    )(*operands)


def _cast_cols_kernel(*refs, shift):
    if shift:
        a_ref, b_ref, o_ref = refs
        x = jnp.concatenate([a_ref[...], b_ref[...]], axis=1)
        o_ref[...] = x[:, shift:shift + o_ref.shape[1]].astype(o_ref.dtype)
    else:
        a_ref, o_ref = refs
        o_ref[...] = a_ref[...].astype(o_ref.dtype)


def _cast_cols(w, col0, ncols, tk=512, tn=1024):
    k = w.shape[0]
    base, shift = divmod(col0, HEAD_DIM)
    tk, tn = _tile(k, tk), math.gcd(math.gcd(ncols, base * HEAD_DIM), tn)
    assert tn % HEAD_DIM == 0
    j0 = base * HEAD_DIM // tn
    in_specs, operands = [pl.BlockSpec((tk, tn), lambda i, j: (i, j0 + j))], [w]
    if shift:
        per = tn // HEAD_DIM
        in_specs.append(pl.BlockSpec((tk, HEAD_DIM), lambda i, j: (i, (j0 + j + 1) * per)))
        operands.append(w)
    return pl.pallas_call(
        functools.partial(_cast_cols_kernel, shift=shift),
        grid=(k // tk, ncols // tn),
        in_specs=in_specs,
        out_specs=pl.BlockSpec((tk, tn), lambda i, j: (i, j)),
        out_shape=jax.ShapeDtypeStruct((k, ncols), BF16),
        compiler_params=_params("parallel", "parallel"),
        name="cast_weight_cols",
    )(*operands)


def _inproj_kernel(u_ref, w_ref, wf_ref, o_ref, f_ref):
    @pl.when(pl.program_id(1) == 0)
    def _():
        f_ref[...] = jnp.dot(u_ref[...], wf_ref[...], preferred_element_type=F32)

    o_ref[...] = jnp.dot(u_ref[...], w_ref[...], preferred_element_type=F32).astype(o_ref.dtype)


def _inproj(u, w, wf, tm=1024, tn=1024):
    m, k = u.shape
    n, nf = w.shape[1], wf.shape[1]
    tm, tn = _tile(m, tm), _tile(n, tn)
    return pl.pallas_call(
        _inproj_kernel,
        grid=(m // tm, n // tn),
        in_specs=[pl.BlockSpec((tm, k), lambda i, j: (i, 0)), pl.BlockSpec((k, tn), lambda i, j: (0, j)),
                  pl.BlockSpec((k, nf), lambda i, j: (0, 0))],
        out_specs=[pl.BlockSpec((tm, tn), lambda i, j: (i, j)), pl.BlockSpec((tm, nf), lambda i, j: (i, 0))],
        out_shape=[jax.ShapeDtypeStruct((m, n), BF16), jax.ShapeDtypeStruct((m, nf), F32)],
        compiler_params=_params("parallel", "arbitrary"),
        name="inproj_fox",
    )(u, w, wf)


def _outproj_kernel(a_ref, b_ref, wa_ref, wb_ref, x_ref, o_ref, ob_ref, part_ref):
    acc = jnp.dot(a_ref[...], wa_ref[...], preferred_element_type=F32)
    acc = acc + jnp.dot(b_ref[...], wb_ref[...], preferred_element_type=F32)
    h = x_ref[...] + acc
    o_ref[...] = h
    ob_ref[...] = h.astype(ob_ref.dtype)
    part_ref[...] = _fold_sq(h)


def _outproj(a, b, w, x, tm=1024, tn=512):
    m, ka = a.shape
    kb = b.shape[1]
    n = w.shape[1]
    assert ka == kb and w.shape[0] == ka + kb
    tm, tn = _tile(m, tm), _tile(n, tn)
    tile = pl.BlockSpec((tm, tn), lambda i, j: (i, j))
    return pl.pallas_call(
        _outproj_kernel,
        grid=(m // tm, n // tn),
        in_specs=[pl.BlockSpec((tm, ka), lambda i, j: (i, 0)), pl.BlockSpec((tm, kb), lambda i, j: (i, 0)),
                  pl.BlockSpec((ka, tn), lambda i, j: (0, j)), pl.BlockSpec((kb, tn), lambda i, j: (1, j)), tile],
        out_specs=[tile, tile, pl.BlockSpec((tm, HEAD_DIM), lambda i, j: (i, j))],
        out_shape=[jax.ShapeDtypeStruct((m, n), F32), jax.ShapeDtypeStruct((m, n), BF16),
                   jax.ShapeDtypeStruct((m, n // tn * HEAD_DIM), F32)],
        compiler_params=_params("parallel", "arbitrary"),
        name="outproj",
    )(a, b, w, w, x)


def _mm_res_kernel(a_ref, w_ref, r_ref, o_ref):
    k = pl.program_id(2)
    acc = jnp.dot(a_ref[...], w_ref[...], preferred_element_type=F32)

    @pl.when(k == 0)
    def _():
        o_ref[...] = r_ref[...] + acc

    @pl.when(k > 0)
    def _():
        o_ref[...] += acc


def _matmul_residual(a, w, r, tm=1024, tn=1024, tk=4096):
    m, kd = a.shape
    n = w.shape[1]
    tm, tn, tk = _tile(m, tm), _tile(n, tn), _tile(kd, tk)
    return pl.pallas_call(
        _mm_res_kernel,
        grid=(m // tm, n // tn, kd // tk),
        in_specs=[pl.BlockSpec((tm, tk), lambda i, j, k: (i, k)), pl.BlockSpec((tk, tn), lambda i, j, k: (k, j)),
                  pl.BlockSpec((tm, tn), lambda i, j, k: (i, j))],
        out_specs=pl.BlockSpec((tm, tn), lambda i, j, k: (i, j)),
        out_shape=jax.ShapeDtypeStruct((m, n), F32),
        compiler_params=_params("parallel", "parallel", "arbitrary", vmem_limit=V7X_VMEM_LIMIT_HIGH),
        name="mlp_down",
    )(a, w, r)


def _split3(x):
    hi = x.astype(BF16).astype(F32)
    r = x - hi
    mid = r.astype(BF16).astype(F32)
    return hi, mid, r - mid


def _fcum_kernel(f_ref, bias_ref, o_ref, carry_ref, *, n_heads):
    t = pl.program_id(1)

    @pl.when(t == 0)
    def _():
        carry_ref[...] = jnp.zeros_like(carry_ref)

    x = f_ref[...] + bias_ref[...]
    ls = jnp.minimum(x, 0.0) - jnp.log1p(jnp.exp(-jnp.abs(x)))
    tc = x.shape[0]
    r = lax.broadcasted_iota(jnp.int32, (tc, tc), 0)
    c = lax.broadcasted_iota(jnp.int32, (tc, tc), 1)
    tri = jnp.where(c <= r, 1.0, 0.0).astype(F32)
    cs = jnp.dot(tri, ls, preferred_element_type=F32, precision=lax.Precision.HIGHEST) + carry_ref[...]
    carry_ref[...] = cs[tc - 1:tc, :]
    lane = lax.broadcasted_iota(jnp.int32, (tc, HEAD_DIM), 1)
    for h in range(n_heads):
        hi, mid, lo = _split3(jnp.broadcast_to(cs[:, h:h + 1] * (-LOG2E), (tc, HEAD_DIM)))
        tile = jnp.where(lane == 0, hi, jnp.where(lane == 1, mid, jnp.where(lane == 2, lo, 0.0)))
        o_ref[:, h * HEAD_DIM:(h + 1) * HEAD_DIM] = tile.astype(o_ref.dtype)


def _forget_bias_tiles(f_logit, bias, batch, seq, n_heads):
    tc = _tile(seq, 512)
    nt = seq // tc
    return pl.pallas_call(
        functools.partial(_fcum_kernel, n_heads=n_heads),
        grid=(batch, nt),
        in_specs=[pl.BlockSpec((tc, HEAD_DIM), lambda b, t: (b * nt + t, 0)),
                  pl.BlockSpec((1, HEAD_DIM), lambda b, t: (0, 0))],
        out_specs=pl.BlockSpec((tc, n_heads * HEAD_DIM), lambda b, t: (b * nt + t, 0)),
        out_shape=jax.ShapeDtypeStruct((batch * seq, n_heads * HEAD_DIM), BF16),
        scratch_shapes=[pltpu.VMEM((1, HEAD_DIM), F32)],
        compiler_params=_params("parallel", "arbitrary"),
        name="fox_forget_cumsum",
    )(f_logit, bias)


def _head_rmsnorm(x, g):
    x = x.astype(F32)
    ms = jnp.mean(x * x, axis=-1, keepdims=True)
    return (x * lax.rsqrt(ms + RMS_EPS) * g).astype(BF16)


def _attn_kernel(q_ref, k_ref, kb_ref, v_ref, gq_ref, gk_ref, o_ref, kp_ref, s_ref, m_ref, l_ref, acc_ref, *,
                 blk, heads):
    qi = pl.program_id(2)
    hcols = [slice(hh * HEAD_DIM, (hh + 1) * HEAD_DIM) for hh in range(heads)]
    seq = k_ref.shape[0]

    @pl.when(qi == 0)
    def _():
        def fill(c, carry):
            rows = pl.ds(pl.multiple_of(c * blk, blk), blk)
            for hh, cols in enumerate(hcols):
                kp_ref[hh, rows, :HEAD_DIM] = _head_rmsnorm(k_ref[rows, cols], gk_ref[...])
                kp_ref[hh, rows, HEAD_DIM:] = kb_ref[rows, cols]
            return carry
        lax.fori_loop(0, seq // blk, fill, 0)

    lane = lax.broadcasted_iota(jnp.int32, (blk, HEAD_DIM), 1)
    q_ones = jnp.where(lane < 3, 1.0, 0.0).astype(BF16)
    key_pos = lax.broadcasted_iota(jnp.int32, (blk, blk), 0)
    qry_pos = lax.broadcasted_iota(jnp.int32, (blk, blk), 1)

    for hh, cols in enumerate(hcols):
        q2 = jnp.concatenate([_head_rmsnorm(q_ref[:, cols], gq_ref[...]), q_ones], axis=1)

        def scores(j, slot):
            rows = pl.ds(pl.multiple_of(j * blk, blk), blk)
            s_ref[slot] = lax.dot_general(kp_ref[hh, rows, :], q2, NT_DIMS, preferred_element_type=F32)

        def update(j, slot, diagonal=False):
            rows = pl.ds(pl.multiple_of(j * blk, blk), blk)
            s = s_ref[slot]
            if diagonal:
                s = jnp.where(key_pos <= qry_pos, s, -jnp.inf)
                m_new = jnp.max(s, axis=0, keepdims=True)
            else:
                m = m_ref[...]
                m_new = jnp.maximum(m, jnp.max(s, axis=0, keepdims=True))
            p = jnp.exp2(s - m_new)
            pv = lax.dot_general(v_ref[rows, cols], p.astype(BF16), TN_DIMS, preferred_element_type=F32)
            m_ref[...] = m_new
            if diagonal:
                l_ref[...] = jnp.sum(p, axis=0, keepdims=True)
                acc_ref[...] = pv
            else:
                alpha = jnp.exp2(m - m_new)
                l_ref[...] = alpha * l_ref[...] + jnp.sum(p, axis=0, keepdims=True)
                acc_ref[...] = alpha * acc_ref[...] + pv

        def pair(t0):
            scores(t0 + 1, 1)
            update(t0, 0)
            scores(t0 + 2, 0)
            update(t0 + 1, 1)

        def quad(k, carry):
            pair(4 * k)
            pair(4 * k + 2)
            return carry

        scores(qi, 1)
        scores(0, 0)
        update(qi, 1, diagonal=True)
        lax.fori_loop(0, qi // 4, quad, 0)

        @pl.when((qi // 2) % 2 == 1)
        def _():
            pair(4 * (qi // 4))

        @pl.when(qi % 2 == 1)
        def _():
            update(qi - 1, 0)

        o_ref[:, cols] = (acc_ref[...] / l_ref[...]).T.astype(o_ref.dtype)


def _fox_attention(proj, kbias, gq, gk, batch, seq, n_heads, blk=512, heads=2):
    blk = _tile(seq, blk)
    heads = _tile(n_heads, heads)
    nq = seq // blk
    hblk = n_heads // heads
    w = heads * HEAD_DIM
    gain = pl.BlockSpec((1, HEAD_DIM), lambda b, h, i: (0, 0))
    return pl.pallas_call(
        functools.partial(_attn_kernel, blk=blk, heads=heads),
        grid=(batch, hblk, nq),
        in_specs=[pl.BlockSpec((blk, w), lambda b, h, i: (b * nq + i, h)),
                  pl.BlockSpec((seq, w), lambda b, h, i: (b, hblk + h)),
                  pl.BlockSpec((seq, w), lambda b, h, i: (b, h)),
                  pl.BlockSpec((seq, w), lambda b, h, i: (b, 2 * hblk + h)), gain, gain],
        out_specs=pl.BlockSpec((blk, w), lambda b, h, i: (b * nq + i, h)),
        out_shape=jax.ShapeDtypeStruct((batch * seq, n_heads * HEAD_DIM), BF16),
        scratch_shapes=[pltpu.VMEM((heads, seq, 2 * HEAD_DIM), BF16), pltpu.VMEM((2, blk, blk), F32),
                        pltpu.VMEM((1, blk), F32), pltpu.VMEM((1, blk), F32), pltpu.VMEM((HEAD_DIM, blk), F32)],
        compiler_params=_params("parallel", "parallel", "arbitrary"),
        name="fox_attention",
    )(proj, proj, kbias, proj, gq, gk)


def _hgrn_kernel(q_ref, z_ref, v_ref, gate_ref, lbl_ref, gn_ref, o_ref, st_ref, rb_ref, *, layer, heads, chunks):
    C, c = HGRN_CHUNK, HGRN_SUB
    t = pl.program_id(2)

    @pl.when(t == 0)
    def _():
        st_ref[...] = jnp.zeros_like(st_ref)

    logits = lbl_ref[...]
    e = jnp.exp(logits - jnp.max(logits, axis=0, keepdims=True))
    lb_all = jnp.sum(e[:layer + 1], axis=0, keepdims=True) / jnp.sum(e, axis=0, keepdims=True)

    row = lax.broadcasted_iota(jnp.int32, (C, C), 0)
    col = lax.broadcasted_iota(jnp.int32, (C, C), 1)
    ltri = jnp.where(col <= row, 1.0, 0.0).astype(BF16)
    sizes = [c << l for l in range(1, (C // c).bit_length())]
    xor = row ^ col
    level = jnp.zeros((C, C), jnp.int32)
    for l, m in enumerate(sizes):
        level = jnp.where(xor >= m // 2, l + 1, level)
    level = jnp.where(col > row, -1, level)
    lane = lax.broadcasted_iota(jnp.int32, (c, C), 1)

    def chunk_body(ci, carry):
        r0 = pl.multiple_of(ci * C, C)
        for hh in range(heads):
            cols = slice(hh * HEAD_DIM, (hh + 1) * HEAD_DIM)
            lb = lb_all[:, cols]
            q = q_ref[pl.ds(r0, C), cols].astype(F32)
            z = z_ref[pl.ds(r0, C), cols].astype(F32)
            v = v_ref[pl.ds(r0, C), cols]
            gate = gate_ref[pl.ds(r0, C), cols].astype(F32)

            ez = jnp.exp(-jnp.abs(z))
            inv = 1.0 / (1.0 + ez)
            sig_z = jnp.where(z >= 0, 1.0, ez) * inv
            sig_mz = jnp.where(z >= 0, ez, 1.0) * inv
            kk = (1.0 - lb) * sig_mz
            qs = q * _sigmoid(q)
            log2f = jnp.concatenate(_split3(jnp.log2(lb + (1.0 - lb) * sig_z)), axis=1).astype(BF16)
            cs = jnp.dot(ltri, log2f, preferred_element_type=F32)
            b = cs[:, :HEAD_DIM] + cs[:, HEAD_DIM:2 * HEAD_DIM] + cs[:, 2 * HEAD_DIM:]
            b_last = b[C - 1:C, :]

            st = st_ref[hh]
            o = lax.dot_general((qs * jnp.exp2(b)).astype(BF16), st.astype(BF16), NT_DIMS,
                                preferred_element_type=F32)

            rb_ref[2 * hh] = kk
            rb_ref[2 * hh + 1] = b
            blocks = []
            for i in range(C // c):
                rows = slice(i * c, (i + 1) * c)
                q_i, b_i = qs[rows], b[rows]
                blk = jnp.zeros((c, C), F32)
                for s in range(i * c, (i + 1) * c):
                    w = q_i * rb_ref[2 * hh, s:s + 1, :] * jnp.exp2(b_i - rb_ref[2 * hh + 1, s:s + 1, :])
                    blk = jnp.where(lane == s, jnp.sum(w, axis=-1, keepdims=True), blk)
                blocks.append(blk)
            a = jnp.where(level == 0, jnp.concatenate(blocks, axis=0), 0.0)
            for l, m in enumerate(sizes):
                edge = jnp.concatenate(
                    [jnp.broadcast_to(b[j * m + m // 2 - 1:j * m + m // 2, :], (m, HEAD_DIM)) for j in range(C // m)],
                    axis=0)
                x = jnp.exp2(-jnp.abs(b - edge))
                pair = lax.dot_general((qs * x).astype(BF16), (kk * x).astype(BF16), NT_DIMS,
                                       preferred_element_type=F32)
                a = jnp.where(level == l + 1, pair, a)
            o = o + jnp.dot(a.astype(BF16), v, preferred_element_type=F32)

            k_d = (kk * jnp.exp2(b_last - b)).astype(BF16)
            st_ref[hh] = st * jnp.exp2(b_last) + lax.dot_general(v, k_d, TN_DIMS, preferred_element_type=F32)

            ms = jnp.mean(o * o, axis=-1, keepdims=True)
            on = o * lax.rsqrt(ms + RMS_EPS) * gn_ref[...]
            o_ref[pl.ds(r0, C), cols] = (on * (gate * _sigmoid(gate))).astype(o_ref.dtype)
        return carry

    lax.fori_loop(0, chunks, chunk_body, 0)


def _hgrn2(proj, lb_logits, gn, layer, batch, seq, n_heads, col0, heads=4, rows=512):
    heads = _tile(n_heads, heads)
    rows = _tile(seq, rows)
    assert rows % HGRN_CHUNK == 0 and col0 % heads == 0
    nt = seq // rows
    w = heads * HEAD_DIM
    hblk = n_heads // heads

    def in_spec(group):
        return pl.BlockSpec((rows, w), lambda b, h, t: (b * nt + t, col0 // heads + group * hblk + h))

    return pl.pallas_call(
        functools.partial(_hgrn_kernel, layer=layer, heads=heads, chunks=rows // HGRN_CHUNK),
        grid=(batch, hblk, nt),
        in_specs=[in_spec(0), in_spec(1), in_spec(2), in_spec(3),
                  pl.BlockSpec((lb_logits.shape[0], w), lambda b, h, t: (0, h)),
                  pl.BlockSpec((1, HEAD_DIM), lambda b, h, t: (0, 0))],
        out_specs=pl.BlockSpec((rows, w), lambda b, h, t: (b * nt + t, h)),
        out_shape=jax.ShapeDtypeStruct((batch * seq, n_heads * HEAD_DIM), BF16),
        scratch_shapes=[pltpu.VMEM((heads, HEAD_DIM, HEAD_DIM), F32),
                        pltpu.VMEM((2 * heads, HGRN_CHUNK, HEAD_DIM), F32)],
        compiler_params=_params("parallel", "parallel", "arbitrary"),
        name="hgrn2",
    )(proj, proj, proj, proj, lb_logits, gn)


def _ple_embed_kernel(p_ref, w_ref, g_ref, o_ref):
    e = jnp.dot(p_ref[...].astype(BF16), w_ref[...], preferred_element_type=F32)
    ms = jnp.mean(e * e, axis=-1, keepdims=True)
    o_ref[...] = (e * lax.rsqrt(ms + RMS_EPS) * g_ref[...]).astype(o_ref.dtype)


def _ple_embed(p, w, g):
    m, kd = p.shape
    n = w.shape[1]
    tm = _tile(m, 256)
    return pl.pallas_call(
        _ple_embed_kernel,
        grid=(m // tm,),
        in_specs=[pl.BlockSpec((tm, kd), lambda i: (i, 0)), pl.BlockSpec((kd, n), lambda i: (0, 0)),
                  pl.BlockSpec((1, n), lambda i: (0, 0))],
        out_specs=pl.BlockSpec((tm, n), lambda i: (i, 0)),
        out_shape=jax.ShapeDtypeStruct((m, n), BF16),
        compiler_params=_params("parallel"),
        name="ple_embed",
    )(p, w, g.reshape(1, n).astype(F32))


def _ple_gate_kernel(a_ref, w_ref, e_ref, h_ref, o_ref):
    gate = _sigmoid(jnp.dot(a_ref[...], w_ref[...], preferred_element_type=F32))
    o_ref[...] = h_ref[...] + gate * e_ref[...]


def _ple_gate(a, w, e, h, tm=1024, tn=512):
    m, kd = a.shape
    n = w.shape[1]
    tm, tn = _tile(m, tm), _tile(n, tn)
    return pl.pallas_call(
        _ple_gate_kernel,
        grid=(m // tm, n // tn),
        in_specs=[pl.BlockSpec((tm, kd), lambda i, j: (i, 0)), pl.BlockSpec((kd, tn), lambda i, j: (0, j)),
                  pl.BlockSpec((tm, tn), lambda i, j: (i, j)), pl.BlockSpec((tm, tn), lambda i, j: (i, j))],
        out_specs=pl.BlockSpec((tm, tn), lambda i, j: (i, j)),
        out_shape=jax.ShapeDtypeStruct((m, n), F32),
        compiler_params=_params("parallel", "arbitrary"),
        name="ple_gate",
    )(a, w, e, h)


def _layer(h, p, layer, batch, seq, norm_mix_g, w_in, fox_f_bias, fox_q_norm_g, fox_k_norm_g, hgrn_lb_logits,
           hgrn_norm_g, w_out, norm_mlp_g, w_up, w_down, ple_norm_g, w_ple_gate, w_ple_proj, ple_post_g):
    d = h.shape[1]
    d_a = d_b = d // 2
    n_a, n_b = d_a // HEAD_DIM, d_b // HEAD_DIM
    assert n_a <= HEAD_DIM and w_in.shape[1] == 3 * d_a + n_a + 4 * d_b

    w_fox = _cast_cols(w_in, 0, 3 * d_a)
    w_fa = jnp.pad(w_in[:, 3 * d_a:3 * d_a + n_a], ((0, 0), (0, HEAD_DIM - n_a))).astype(BF16)
    w_hgrn = _cast_cols(w_in, 3 * d_a + n_a, 4 * d_b)
    q_gain = (fox_q_norm_g.astype(F32) * (LOG2E / math.sqrt(HEAD_DIM))).reshape(1, HEAD_DIM)
    k_gain = fox_k_norm_g.astype(F32).reshape(1, HEAD_DIM)

    u = _rmsnorm(h, norm_mix_g)
    proj_a, f_logit = _inproj(u, w_fox, w_fa)
    proj_b = _matmul(u, w_hgrn, BF16, name="inproj_hgrn")

    f_bias = jnp.pad(fox_f_bias.astype(F32), (0, HEAD_DIM - n_a)).reshape(1, HEAD_DIM)
    kbias = _forget_bias_tiles(f_logit, f_bias, batch, seq, n_a)
    out_a = _fox_attention(proj_a, kbias, q_gain, k_gain, batch, seq, n_a)

    out_b = _hgrn2(proj_b, hgrn_lb_logits.astype(F32), hgrn_norm_g.astype(F32).reshape(1, HEAD_DIM), layer,
                   batch, seq, n_b, col0=0)

    h, hb, part = _outproj(out_a, out_b, w_out.astype(BF16), h)
    w_up_g = (norm_mlp_g.astype(F32)[:, None] * w_up).astype(BF16)
    hid = _matmul(hb, w_up_g, BF16, act="relu2", part=part, name="mlp_up")
    h = _matmul_residual(hid, w_down.astype(BF16), h)

    e = _ple_embed(p, w_ple_proj.astype(BF16), ple_post_g)
    return _ple_gate(_rmsnorm(h, ple_norm_g), w_ple_gate.astype(BF16), e, h)


def kernel(x, p, norm_mix_g, w_in, fox_f_bias, fox_q_norm_g, fox_k_norm_g, hgrn_lb_logits, hgrn_norm_g, w_out,
           norm_mlp_g, w_up, w_down, ple_norm_g, w_ple_gate, w_ple_proj, ple_post_g):
    batch, seq, d = x.shape
    h = x.reshape(batch * seq, d)
    for i in range(w_in.shape[0]):
        h = _layer(h, p[i].reshape(batch * seq, -1), i, batch, seq, norm_mix_g[i], w_in[i], fox_f_bias[i],
                   fox_q_norm_g[i], fox_k_norm_g[i], hgrn_lb_logits, hgrn_norm_g[i], w_out[i], norm_mlp_g[i],
                   w_up[i], w_down[i], ple_norm_g[i], w_ple_gate[i], w_ple_proj[i], ple_post_g[i])
    return h.reshape(batch, seq, d)
```

```python
import functools
import math

import jax
import jax.numpy as jnp
from jax import lax
from jax.experimental import pallas as pl
from jax.experimental.pallas import tpu as pltpu

F32, BF16 = jnp.float32, jnp.bfloat16
HEAD_DIM = 128
RMS_EPS = 1e-6
LOG2E = math.log2(math.e)
HGRN_CHUNK = 128
HGRN_SUB = 8
V7X_VMEM_LIMIT = 56 * 1024 * 1024
V7X_VMEM_LIMIT_HIGH = 60 * 1024 * 1024

NT_DIMS = (((1,), (1,)), ((), ()))
TN_DIMS = (((0,), (0,)), ((), ()))


def _params(*semantics, vmem_limit=V7X_VMEM_LIMIT):
    return pltpu.CompilerParams(dimension_semantics=semantics, vmem_limit_bytes=vmem_limit)


def _tile(dim, pref):
    t = min(dim, pref)
    assert dim % t == 0, (dim, pref)
    return t


def _sigmoid(x):
    return 1.0 / (1.0 + jnp.exp(-x))


def _rmsnorm_kernel(x_ref, g_ref, o_ref):
    x = x_ref[...]
    ms = jnp.mean(x * x, axis=-1, keepdims=True)
    o_ref[...] = (x * lax.rsqrt(ms + RMS_EPS) * g_ref[...]).astype(o_ref.dtype)


def _rmsnorm(x, g, out_dtype=BF16):
    m, d = x.shape
    tm = _tile(m, 256)
    return pl.pallas_call(
        _rmsnorm_kernel,
        grid=(m // tm,),
        in_specs=[pl.BlockSpec((tm, d), lambda i: (i, 0)), pl.BlockSpec((1, d), lambda i: (0, 0))],
        out_specs=pl.BlockSpec((tm, d), lambda i: (i, 0)),
        out_shape=jax.ShapeDtypeStruct((m, d), out_dtype),
        compiler_params=_params("parallel"),
        name="rmsnorm",
    )(x, g.reshape(1, d).astype(F32))


def _row_scale(part_ref, d):
    return lax.rsqrt(jnp.sum(part_ref[...], axis=-1, keepdims=True) * (1.0 / d) + RMS_EPS)


def _fold_sq(x):
    sq = x * x
    out = sq[:, :HEAD_DIM]
    for t in range(1, x.shape[1] // HEAD_DIM):
        out = out + sq[:, t * HEAD_DIM:(t + 1) * HEAD_DIM]
    return out


def _mm_kernel(a_ref, w_ref, o_ref, *, act):
    acc = jnp.dot(a_ref[...], w_ref[...], preferred_element_type=F32)
    if act == "relu2":
        acc = jnp.square(jnp.maximum(acc, 0.0))
    o_ref[...] = acc.astype(o_ref.dtype)


def _mm_scaled_kernel(a_ref, w_ref, part_ref, o_ref, rs_ref, *, act):
    @pl.when(pl.program_id(1) == 0)
    def _():
        rs_ref[...] = _row_scale(part_ref, a_ref.shape[1])

    acc = jnp.dot(a_ref[...], w_ref[...], preferred_element_type=F32) * rs_ref[...]
    if act == "relu2":
        acc = jnp.square(jnp.maximum(acc, 0.0))
    o_ref[...] = acc.astype(o_ref.dtype)


def _matmul(a, w, out_dtype, act=None, part=None, tm=1024, tn=1024, name="matmul"):
    m, k = a.shape
    n = w.shape[1]
    tm, tn = _tile(m, tm), _tile(n, tn)
    in_specs = [pl.BlockSpec((tm, k), lambda i, j: (i, 0)), pl.BlockSpec((k, tn), lambda i, j: (0, j))]
    if part is None:
        body, operands, scratch = functools.partial(_mm_kernel, act=act), (a, w), []
    else:
        body, operands = functools.partial(_mm_scaled_kernel, act=act), (a, w, part)
        in_specs.append(pl.BlockSpec((tm, part.shape[1]), lambda i, j: (i, 0)))
        scratch = [pltpu.VMEM((tm, 1), F32)]
    return pl.pallas_call(
        body,
        grid=(m // tm, n // tn),
        in_specs=in_specs,
        out_specs=pl.BlockSpec((tm, tn), lambda i, j: (i, j)),
        out_shape=jax.ShapeDtypeStruct((m, n), out_dtype),
        scratch_shapes=scratch,
        compiler_params=_params("parallel", "arbitrary"),
        name=name,
    )(*operands)


def _cast_cols_kernel(*refs, shift):
    if shift:
        a_ref, b_ref, o_ref = refs
        x = jnp.concatenate([a_ref[...], b_ref[...]], axis=1)
        o_ref[...] = x[:, shift:shift + o_ref.shape[1]].astype(o_ref.dtype)
    else:
        a_ref, o_ref = refs
        o_ref[...] = a_ref[...].astype(o_ref.dtype)


def _cast_cols(w, col0, ncols, tk=512, tn=1024):
    k = w.shape[0]
    base, shift = divmod(col0, HEAD_DIM)
    tk, tn = _tile(k, tk), math.gcd(math.gcd(ncols, base * HEAD_DIM), tn)
    assert tn % HEAD_DIM == 0
    j0 = base * HEAD_DIM // tn
    in_specs, operands = [pl.BlockSpec((tk, tn), lambda i, j: (i, j0 + j))], [w]
    if shift:
        per = tn // HEAD_DIM
        in_specs.append(pl.BlockSpec((tk, HEAD_DIM), lambda i, j: (i, (j0 + j + 1) * per)))
        operands.append(w)
    return pl.pallas_call(
        functools.partial(_cast_cols_kernel, shift=shift),
        grid=(k // tk, ncols // tn),
        in_specs=in_specs,
        out_specs=pl.BlockSpec((tk, tn), lambda i, j: (i, j)),
        out_shape=jax.ShapeDtypeStruct((k, ncols), BF16),
        compiler_params=_params("parallel", "parallel"),
        name="cast_weight_cols",
    )(*operands)


def _inproj_kernel(u_ref, w_ref, wf_ref, o_ref, f_ref):
    @pl.when(pl.program_id(1) == 0)
    def _():
        f_ref[...] = jnp.dot(u_ref[...], wf_ref[...], preferred_element_type=F32)

    o_ref[...] = jnp.dot(u_ref[...], w_ref[...], preferred_element_type=F32).astype(o_ref.dtype)


def _inproj(u, w, wf, tm=1024, tn=1024):
    m, k = u.shape
    n, nf = w.shape[1], wf.shape[1]
    tm, tn = _tile(m, tm), _tile(n, tn)
    return pl.pallas_call(
        _inproj_kernel,
        grid=(m // tm, n // tn),
        in_specs=[pl.BlockSpec((tm, k), lambda i, j: (i, 0)), pl.BlockSpec((k, tn), lambda i, j: (0, j)),
                  pl.BlockSpec((k, nf), lambda i, j: (0, 0))],
        out_specs=[pl.BlockSpec((tm, tn), lambda i, j: (i, j)), pl.BlockSpec((tm, nf), lambda i, j: (i, 0))],
        out_shape=[jax.ShapeDtypeStruct((m, n), BF16), jax.ShapeDtypeStruct((m, nf), F32)],
        compiler_params=_params("parallel", "arbitrary"),
        name="inproj_fox",
    )(u, w, wf)


def _outproj_kernel(a_ref, b_ref, wa_ref, wb_ref, x_ref, o_ref, ob_ref, part_ref):
    acc = jnp.dot(a_ref[...], wa_ref[...], preferred_element_type=F32)
    acc = acc + jnp.dot(b_ref[...], wb_ref[...], preferred_element_type=F32)
    h = x_ref[...] + acc
    o_ref[...] = h
    ob_ref[...] = h.astype(ob_ref.dtype)
    part_ref[...] = _fold_sq(h)


def _outproj(a, b, w, x, tm=1024, tn=512):
    m, ka = a.shape
    kb = b.shape[1]
    n = w.shape[1]
    assert ka == kb and w.shape[0] == ka + kb
    tm, tn = _tile(m, tm), _tile(n, tn)
    tile = pl.BlockSpec((tm, tn), lambda i, j: (i, j))
    return pl.pallas_call(
        _outproj_kernel,
        grid=(m // tm, n // tn),
        in_specs=[pl.BlockSpec((tm, ka), lambda i, j: (i, 0)), pl.BlockSpec((tm, kb), lambda i, j: (i, 0)),
                  pl.BlockSpec((ka, tn), lambda i, j: (0, j)), pl.BlockSpec((kb, tn), lambda i, j: (1, j)), tile],
        out_specs=[tile, tile, pl.BlockSpec((tm, HEAD_DIM), lambda i, j: (i, j))],
        out_shape=[jax.ShapeDtypeStruct((m, n), F32), jax.ShapeDtypeStruct((m, n), BF16),
                   jax.ShapeDtypeStruct((m, n // tn * HEAD_DIM), F32)],
        compiler_params=_params("parallel", "arbitrary"),
        name="outproj",
    )(a, b, w, w, x)


def _mm_res_kernel(a_ref, w_ref, r_ref, o_ref):
    k = pl.program_id(2)
    acc = jnp.dot(a_ref[...], w_ref[...], preferred_element_type=F32)

    @pl.when(k == 0)
    def _():
        o_ref[...] = r_ref[...] + acc

    @pl.when(k > 0)
    def _():
        o_ref[...] += acc


def _matmul_residual(a, w, r, tm=1024, tn=1024, tk=4096):
    m, kd = a.shape
    n = w.shape[1]
    tm, tn, tk = _tile(m, tm), _tile(n, tn), _tile(kd, tk)
    return pl.pallas_call(
        _mm_res_kernel,
        grid=(m // tm, n // tn, kd // tk),
        in_specs=[pl.BlockSpec((tm, tk), lambda i, j, k: (i, k)), pl.BlockSpec((tk, tn), lambda i, j, k: (k, j)),
                  pl.BlockSpec((tm, tn), lambda i, j, k: (i, j))],
        out_specs=pl.BlockSpec((tm, tn), lambda i, j, k: (i, j)),
        out_shape=jax.ShapeDtypeStruct((m, n), F32),
        compiler_params=_params("parallel", "parallel", "arbitrary", vmem_limit=V7X_VMEM_LIMIT_HIGH),
        name="mlp_down",
    )(a, w, r)


def _split3(x):
    hi = x.astype(BF16).astype(F32)
    r = x - hi
    mid = r.astype(BF16).astype(F32)
    return hi, mid, r - mid


def _fcum_kernel(f_ref, bias_ref, o_ref, carry_ref, *, n_heads):
    t = pl.program_id(1)

    @pl.when(t == 0)
    def _():
        carry_ref[...] = jnp.zeros_like(carry_ref)

    x = f_ref[...] + bias_ref[...]
    ls = jnp.minimum(x, 0.0) - jnp.log1p(jnp.exp(-jnp.abs(x)))
    tc = x.shape[0]
    r = lax.broadcasted_iota(jnp.int32, (tc, tc), 0)
    c = lax.broadcasted_iota(jnp.int32, (tc, tc), 1)
    tri = jnp.where(c <= r, 1.0, 0.0).astype(F32)
    cs = jnp.dot(tri, ls, preferred_element_type=F32, precision=lax.Precision.HIGHEST) + carry_ref[...]
    carry_ref[...] = cs[tc - 1:tc, :]
    lane = lax.broadcasted_iota(jnp.int32, (tc, HEAD_DIM), 1)
    for h in range(n_heads):
        hi, mid, lo = _split3(jnp.broadcast_to(cs[:, h:h + 1] * (-LOG2E), (tc, HEAD_DIM)))
        tile = jnp.where(lane == 0, hi, jnp.where(lane == 1, mid, jnp.where(lane == 2, lo, 0.0)))
        o_ref[:, h * HEAD_DIM:(h + 1) * HEAD_DIM] = tile.astype(o_ref.dtype)


def _forget_bias_tiles(f_logit, bias, batch, seq, n_heads):
    tc = _tile(seq, 512)
    nt = seq // tc
    return pl.pallas_call(
        functools.partial(_fcum_kernel, n_heads=n_heads),
        grid=(batch, nt),
        in_specs=[pl.BlockSpec((tc, HEAD_DIM), lambda b, t: (b * nt + t, 0)),
                  pl.BlockSpec((1, HEAD_DIM), lambda b, t: (0, 0))],
        out_specs=pl.BlockSpec((tc, n_heads * HEAD_DIM), lambda b, t: (b * nt + t, 0)),
        out_shape=jax.ShapeDtypeStruct((batch * seq, n_heads * HEAD_DIM), BF16),
        scratch_shapes=[pltpu.VMEM((1, HEAD_DIM), F32)],
        compiler_params=_params("parallel", "arbitrary"),
        name="fox_forget_cumsum",
    )(f_logit, bias)


def _head_rmsnorm(x, g):
    x = x.astype(F32)
    ms = jnp.mean(x * x, axis=-1, keepdims=True)
    return (x * lax.rsqrt(ms + RMS_EPS) * g).astype(BF16)


def _attn_kernel(q_ref, k_ref, kb_ref, v_ref, gq_ref, gk_ref, o_ref, kp_ref, s_ref, m_ref, l_ref, acc_ref, *,
                 blk, heads):
    qi = pl.program_id(2)
    hcols = [slice(hh * HEAD_DIM, (hh + 1) * HEAD_DIM) for hh in range(heads)]
    seq = k_ref.shape[0]

    @pl.when(qi == 0)
    def _():
        def fill(c, carry):
            rows = pl.ds(pl.multiple_of(c * blk, blk), blk)
            for hh, cols in enumerate(hcols):
                kp_ref[hh, rows, :HEAD_DIM] = _head_rmsnorm(k_ref[rows, cols], gk_ref[...])
                kp_ref[hh, rows, HEAD_DIM:] = kb_ref[rows, cols]
            return carry
        lax.fori_loop(0, seq // blk, fill, 0)

    lane = lax.broadcasted_iota(jnp.int32, (blk, HEAD_DIM), 1)
    q_ones = jnp.where(lane < 3, 1.0, 0.0).astype(BF16)
    key_pos = lax.broadcasted_iota(jnp.int32, (blk, blk), 0)
    qry_pos = lax.broadcasted_iota(jnp.int32, (blk, blk), 1)

    for hh, cols in enumerate(hcols):
        q2 = jnp.concatenate([_head_rmsnorm(q_ref[:, cols], gq_ref[...]), q_ones], axis=1)

        def tile_rows(t):
            j = jnp.maximum(qi - t, 0)
            return pl.ds(pl.multiple_of(j * blk, blk), blk)

        def scores(t, slot):
            s_ref[slot] = lax.dot_general(kp_ref[hh, tile_rows(t), :], q2, NT_DIMS, preferred_element_type=F32)

        def update(t, slot):
            s = s_ref[slot]
            m = m_ref[...]
            m_new = jnp.maximum(m, jnp.max(s, axis=0, keepdims=True))
            p = jnp.exp2(s - m_new)
            alpha = jnp.exp2(m - m_new)
            pv = lax.dot_general(v_ref[tile_rows(t), cols], p.astype(BF16), TN_DIMS, preferred_element_type=F32)
            m_ref[...] = m_new
            l_ref[...] = alpha * l_ref[...] + jnp.sum(p, axis=0, keepdims=True)
            acc_ref[...] = alpha * acc_ref[...] + pv

        def pair(t0):
            scores(t0 + 1, 1)
            update(t0, 0)
            scores(t0 + 2, 0)
            update(t0 + 1, 1)

        def quad(k, carry):
            pair(4 * k)
            pair(4 * k + 2)
            return carry

        n = qi + 1
        m_ref[...] = jnp.full(m_ref.shape, -jnp.inf, F32)
        l_ref[...] = jnp.zeros(l_ref.shape, F32)
        acc_ref[...] = jnp.zeros(acc_ref.shape, F32)
        scores(0, 0)
        s_ref[0] = jnp.where(key_pos <= qry_pos, s_ref[0], -jnp.inf)
        lax.fori_loop(0, n // 4, quad, 0)

        @pl.when((n // 2) % 2 == 1)
        def _():
            pair(4 * (n // 4))

        @pl.when(n % 2 == 1)
        def _():
            update(n - 1, 0)

        o_ref[:, cols] = (acc_ref[...] / l_ref[...]).T.astype(o_ref.dtype)


def _fox_attention(proj, kbias, gq, gk, batch, seq, n_heads, blk=512, heads=2):
    blk = _tile(seq, blk)
    heads = _tile(n_heads, heads)
    nq = seq // blk
    hblk = n_heads // heads
    w = heads * HEAD_DIM
    gain = pl.BlockSpec((1, HEAD_DIM), lambda b, h, i: (0, 0))
    return pl.pallas_call(
        functools.partial(_attn_kernel, blk=blk, heads=heads),
        grid=(batch, hblk, nq),
        in_specs=[pl.BlockSpec((blk, w), lambda b, h, i: (b * nq + i, h)),
                  pl.BlockSpec((seq, w), lambda b, h, i: (b, hblk + h)),
                  pl.BlockSpec((seq, w), lambda b, h, i: (b, h)),
                  pl.BlockSpec((seq, w), lambda b, h, i: (b, 2 * hblk + h)), gain, gain],
        out_specs=pl.BlockSpec((blk, w), lambda b, h, i: (b * nq + i, h)),
        out_shape=jax.ShapeDtypeStruct((batch * seq, n_heads * HEAD_DIM), BF16),
        scratch_shapes=[pltpu.VMEM((heads, seq, 2 * HEAD_DIM), BF16), pltpu.VMEM((2, blk, blk), F32),
                        pltpu.VMEM((1, blk), F32), pltpu.VMEM((1, blk), F32), pltpu.VMEM((HEAD_DIM, blk), F32)],
        compiler_params=_params("parallel", "parallel", "arbitrary"),
        name="fox_attention",
    )(proj, proj, kbias, proj, gq, gk)


def _hgrn_kernel(q_ref, z_ref, v_ref, gate_ref, lbl_ref, gn_ref, o_ref, st_ref, rb_ref, *, layer, heads, chunks):
    C, c = HGRN_CHUNK, HGRN_SUB
    t = pl.program_id(2)

    @pl.when(t == 0)
    def _():
        st_ref[...] = jnp.zeros_like(st_ref)

    logits = lbl_ref[...]
    e = jnp.exp(logits - jnp.max(logits, axis=0, keepdims=True))
    lb_all = jnp.sum(e[:layer + 1], axis=0, keepdims=True) / jnp.sum(e, axis=0, keepdims=True)

    row = lax.broadcasted_iota(jnp.int32, (C, C), 0)
    col = lax.broadcasted_iota(jnp.int32, (C, C), 1)
    ltri = jnp.where(col <= row, 1.0, 0.0).astype(BF16)
    sizes = [c << l for l in range(1, (C // c).bit_length())]
    xor = row ^ col
    level = jnp.zeros((C, C), jnp.int32)
    for l, m in enumerate(sizes):
        level = jnp.where(xor >= m // 2, l + 1, level)
    level = jnp.where(col > row, -1, level)
    lane = lax.broadcasted_iota(jnp.int32, (c, C), 1)

    def chunk_body(ci, carry):
        r0 = pl.multiple_of(ci * C, C)
        for hh in range(heads):
            cols = slice(hh * HEAD_DIM, (hh + 1) * HEAD_DIM)
            lb = lb_all[:, cols]
            q = q_ref[pl.ds(r0, C), cols].astype(F32)
            z = z_ref[pl.ds(r0, C), cols].astype(F32)
            v = v_ref[pl.ds(r0, C), cols]
            gate = gate_ref[pl.ds(r0, C), cols].astype(F32)

            ez = jnp.exp(-jnp.abs(z))
            inv = 1.0 / (1.0 + ez)
            sig_z = jnp.where(z >= 0, 1.0, ez) * inv
            sig_mz = jnp.where(z >= 0, ez, 1.0) * inv
            kk = (1.0 - lb) * sig_mz
            qs = q * _sigmoid(q)
            log2f = jnp.concatenate(_split3(jnp.log2(lb + (1.0 - lb) * sig_z)), axis=1).astype(BF16)
            cs = jnp.dot(ltri, log2f, preferred_element_type=F32)
            b = cs[:, :HEAD_DIM] + cs[:, HEAD_DIM:2 * HEAD_DIM] + cs[:, 2 * HEAD_DIM:]
            b_last = b[C - 1:C, :]

            st = st_ref[hh]
            o = lax.dot_general((qs * jnp.exp2(b)).astype(BF16), st.astype(BF16), NT_DIMS,
                                preferred_element_type=F32)

            rb_ref[2 * hh] = kk
            rb_ref[2 * hh + 1] = b
            blocks = []
            for i in range(C // c):
                rows = slice(i * c, (i + 1) * c)
                q_i, b_i = qs[rows], b[rows]
                blk = jnp.zeros((c, C), F32)
                for s in range(i * c, (i + 1) * c):
                    w = q_i * rb_ref[2 * hh, s:s + 1, :] * jnp.exp2(b_i - rb_ref[2 * hh + 1, s:s + 1, :])
                    blk = jnp.where(lane == s, jnp.sum(w, axis=-1, keepdims=True), blk)
                blocks.append(blk)
            a = jnp.where(level == 0, jnp.concatenate(blocks, axis=0), 0.0)
            for l, m in enumerate(sizes):
                edge = jnp.concatenate(
                    [jnp.broadcast_to(b[j * m + m // 2 - 1:j * m + m // 2, :], (m, HEAD_DIM)) for j in range(C // m)],
                    axis=0)
                x = jnp.exp2(-jnp.abs(b - edge))
                pair = lax.dot_general((qs * x).astype(BF16), (kk * x).astype(BF16), NT_DIMS,
                                       preferred_element_type=F32)
                a = jnp.where(level == l + 1, pair, a)
            o = o + jnp.dot(a.astype(BF16), v, preferred_element_type=F32)

            k_d = (kk * jnp.exp2(b_last - b)).astype(BF16)
            st_ref[hh] = st * jnp.exp2(b_last) + lax.dot_general(v, k_d, TN_DIMS, preferred_element_type=F32)

            ms = jnp.mean(o * o, axis=-1, keepdims=True)
            on = o * lax.rsqrt(ms + RMS_EPS) * gn_ref[...]
            o_ref[pl.ds(r0, C), cols] = (on * (gate * _sigmoid(gate))).astype(o_ref.dtype)
        return carry

    lax.fori_loop(0, chunks, chunk_body, 0)


def _hgrn2(proj, lb_logits, gn, layer, batch, seq, n_heads, col0, heads=4, rows=512):
    heads = _tile(n_heads, heads)
    rows = _tile(seq, rows)
    assert rows % HGRN_CHUNK == 0 and col0 % heads == 0
    nt = seq // rows
    w = heads * HEAD_DIM
    hblk = n_heads // heads

    def in_spec(group):
        return pl.BlockSpec((rows, w), lambda b, h, t: (b * nt + t, col0 // heads + group * hblk + h))

    return pl.pallas_call(
        functools.partial(_hgrn_kernel, layer=layer, heads=heads, chunks=rows // HGRN_CHUNK),
        grid=(batch, hblk, nt),
        in_specs=[in_spec(0), in_spec(1), in_spec(2), in_spec(3),
                  pl.BlockSpec((lb_logits.shape[0], w), lambda b, h, t: (0, h)),
                  pl.BlockSpec((1, HEAD_DIM), lambda b, h, t: (0, 0))],
        out_specs=pl.BlockSpec((rows, w), lambda b, h, t: (b * nt + t, h)),
        out_shape=jax.ShapeDtypeStruct((batch * seq, n_heads * HEAD_DIM), BF16),
        scratch_shapes=[pltpu.VMEM((heads, HEAD_DIM, HEAD_DIM), F32),
                        pltpu.VMEM((2 * heads, HGRN_CHUNK, HEAD_DIM), F32)],
        compiler_params=_params("parallel", "parallel", "arbitrary"),
        name="hgrn2",
    )(proj, proj, proj, proj, lb_logits, gn)


def _ple_embed_kernel(p_ref, w_ref, g_ref, o_ref):
    e = jnp.dot(p_ref[...].astype(BF16), w_ref[...], preferred_element_type=F32)
    ms = jnp.mean(e * e, axis=-1, keepdims=True)
    o_ref[...] = (e * lax.rsqrt(ms + RMS_EPS) * g_ref[...]).astype(o_ref.dtype)


def _ple_embed(p, w, g):
    m, kd = p.shape
    n = w.shape[1]
    tm = _tile(m, 256)
    return pl.pallas_call(
        _ple_embed_kernel,
        grid=(m // tm,),
        in_specs=[pl.BlockSpec((tm, kd), lambda i: (i, 0)), pl.BlockSpec((kd, n), lambda i: (0, 0)),
                  pl.BlockSpec((1, n), lambda i: (0, 0))],
        out_specs=pl.BlockSpec((tm, n), lambda i: (i, 0)),
        out_shape=jax.ShapeDtypeStruct((m, n), BF16),
        compiler_params=_params("parallel"),
        name="ple_embed",
    )(p, w, g.reshape(1, n).astype(F32))


def _ple_gate_kernel(a_ref, w_ref, e_ref, h_ref, o_ref):
    gate = _sigmoid(jnp.dot(a_ref[...], w_ref[...], preferred_element_type=F32))
    o_ref[...] = h_ref[...] + gate * e_ref[...]


def _ple_gate(a, w, e, h, tm=1024, tn=512):
    m, kd = a.shape
    n = w.shape[1]
    tm, tn = _tile(m, tm), _tile(n, tn)
    return pl.pallas_call(
        _ple_gate_kernel,
        grid=(m // tm, n // tn),
        in_specs=[pl.BlockSpec((tm, kd), lambda i, j: (i, 0)), pl.BlockSpec((kd, tn), lambda i, j: (0, j)),
                  pl.BlockSpec((tm, tn), lambda i, j: (i, j)), pl.BlockSpec((tm, tn), lambda i, j: (i, j))],
        out_specs=pl.BlockSpec((tm, tn), lambda i, j: (i, j)),
        out_shape=jax.ShapeDtypeStruct((m, n), F32),
        compiler_params=_params("parallel", "arbitrary"),
        name="ple_gate",
    )(a, w, e, h)


def _layer(h, p, layer, batch, seq, norm_mix_g, w_in, fox_f_bias, fox_q_norm_g, fox_k_norm_g, hgrn_lb_logits,
           hgrn_norm_g, w_out, norm_mlp_g, w_up, w_down, ple_norm_g, w_ple_gate, w_ple_proj, ple_post_g):
    d = h.shape[1]
    d_a = d_b = d // 2
    n_a, n_b = d_a // HEAD_DIM, d_b // HEAD_DIM
    assert n_a <= HEAD_DIM and w_in.shape[1] == 3 * d_a + n_a + 4 * d_b

    w_fox = _cast_cols(w_in, 0, 3 * d_a)
    w_fa = _cast_cols(w_in, 3 * d_a, HEAD_DIM)
    w_hgrn = _cast_cols(w_in, 3 * d_a + n_a, 4 * d_b)
    q_gain = (fox_q_norm_g.astype(F32) * (LOG2E / math.sqrt(HEAD_DIM))).reshape(1, HEAD_DIM)
    k_gain = fox_k_norm_g.astype(F32).reshape(1, HEAD_DIM)

    u = _rmsnorm(h, norm_mix_g)
    proj_a, f_logit = _inproj(u, w_fox, w_fa)
    proj_b = _matmul(u, w_hgrn, BF16, name="inproj_hgrn")

    f_bias = jnp.pad(fox_f_bias.astype(F32), (0, HEAD_DIM - n_a)).reshape(1, HEAD_DIM)
    kbias = _forget_bias_tiles(f_logit, f_bias, batch, seq, n_a)
    out_a = _fox_attention(proj_a, kbias, q_gain, k_gain, batch, seq, n_a)

    out_b = _hgrn2(proj_b, hgrn_lb_logits.astype(F32), hgrn_norm_g.astype(F32).reshape(1, HEAD_DIM), layer,
                   batch, seq, n_b, col0=0)

    h, hb, part = _outproj(out_a, out_b, w_out.astype(BF16), h)
    w_up_g = (norm_mlp_g.astype(F32)[:, None] * w_up).astype(BF16)
    hid = _matmul(hb, w_up_g, BF16, act="relu2", part=part, name="mlp_up")
    h = _matmul_residual(hid, w_down.astype(BF16), h)

    e = _ple_embed(p, w_ple_proj.astype(BF16), ple_post_g)
    return _ple_gate(_rmsnorm(h, ple_norm_g), w_ple_gate.astype(BF16), e, h)


def kernel(x, p, norm_mix_g, w_in, fox_f_bias, fox_q_norm_g, fox_k_norm_g, hgrn_lb_logits, hgrn_norm_g, w_out,
           norm_mlp_g, w_up, w_down, ple_norm_g, w_ple_gate, w_ple_proj, ple_post_g):
    batch, seq, d = x.shape
    h = x.reshape(batch * seq, d)
    for i in range(w_in.shape[0]):
        h = _layer(h, p[i].reshape(batch * seq, -1), i, batch, seq, norm_mix_g[i], w_in[i], fox_f_bias[i],
                   fox_q_norm_g[i], fox_k_norm_g[i], hgrn_lb_logits, hgrn_norm_g[i], w_out[i], norm_mlp_g[i],
                   w_up[i], w_down[i], ple_norm_g[i], w_ple_gate[i], w_ple_proj[i], ple_post_g[i])
    return h.reshape(batch, seq, d)
```

```python
import functools
import math

import jax
import jax.numpy as jnp
from jax import lax
from jax.experimental import pallas as pl
from jax.experimental.pallas import tpu as pltpu

F32, BF16 = jnp.float32, jnp.bfloat16
HEAD_DIM = 128
RMS_EPS = 1e-6
LOG2E = math.log2(math.e)
HGRN_CHUNK = 128
HGRN_SUB = 8
V7X_VMEM_LIMIT = 56 * 1024 * 1024
V7X_VMEM_LIMIT_HIGH = 60 * 1024 * 1024

NT_DIMS = (((1,), (1,)), ((), ()))
TN_DIMS = (((0,), (0,)), ((), ()))


def _params(*semantics, vmem_limit=V7X_VMEM_LIMIT):
    return pltpu.CompilerParams(dimension_semantics=semantics, vmem_limit_bytes=vmem_limit)


def _tile(dim, pref):
    t = min(dim, pref)
    assert dim % t == 0, (dim, pref)
    return t


def _sigmoid(x):
    return 1.0 / (1.0 + jnp.exp(-x))


def _rmsnorm_kernel(x_ref, g_ref, o_ref):
    x = x_ref[...]
    ms = jnp.mean(x * x, axis=-1, keepdims=True)
    o_ref[...] = (x * lax.rsqrt(ms + RMS_EPS) * g_ref[...]).astype(o_ref.dtype)


def _rmsnorm(x, g, out_dtype=BF16):
    m, d = x.shape
    tm = _tile(m, 256)
    return pl.pallas_call(
        _rmsnorm_kernel,
        grid=(m // tm,),
        in_specs=[pl.BlockSpec((tm, d), lambda i: (i, 0)), pl.BlockSpec((1, d), lambda i: (0, 0))],
        out_specs=pl.BlockSpec((tm, d), lambda i: (i, 0)),
        out_shape=jax.ShapeDtypeStruct((m, d), out_dtype),
        compiler_params=_params("parallel"),
        name="rmsnorm",
    )(x, g.reshape(1, d).astype(F32))


def _row_scale(part_ref, d):
    return lax.rsqrt(jnp.sum(part_ref[...], axis=-1, keepdims=True) * (1.0 / d) + RMS_EPS)


def _fold_sq(x):
    sq = x * x
    out = sq[:, :HEAD_DIM]
    for t in range(1, x.shape[1] // HEAD_DIM):
        out = out + sq[:, t * HEAD_DIM:(t + 1) * HEAD_DIM]
    return out


def _mm_kernel(a_ref, w_ref, o_ref, *, act, w_dims):
    acc = lax.dot_general(a_ref[...], w_ref[...], w_dims, preferred_element_type=F32)
    if act == "relu2":
        acc = jnp.square(jnp.maximum(acc, 0.0))
    o_ref[...] = acc.astype(o_ref.dtype)


def _mm_scaled_kernel(a_ref, w_ref, part_ref, o_ref, rs_ref, *, act):
    @pl.when(pl.program_id(1) == 0)
    def _():
        rs_ref[...] = _row_scale(part_ref, a_ref.shape[1])

    acc = jnp.dot(a_ref[...], w_ref[...], preferred_element_type=F32) * rs_ref[...]
    if act == "relu2":
        acc = jnp.square(jnp.maximum(acc, 0.0))
    o_ref[...] = acc.astype(o_ref.dtype)


def _matmul(a, w, out_dtype, act=None, part=None, w_transposed=False, tm=1024, tn=1024, name="matmul"):
    m, k = a.shape
    n = w.shape[0] if w_transposed else w.shape[1]
    tm, tn = _tile(m, tm), _tile(n, tn)
    w_spec = pl.BlockSpec((tn, k), lambda i, j: (j, 0)) if w_transposed else pl.BlockSpec((k, tn), lambda i, j: (0, j))
    in_specs = [pl.BlockSpec((tm, k), lambda i, j: (i, 0)), w_spec]
    if part is None:
        w_dims = NT_DIMS if w_transposed else (((1,), (0,)), ((), ()))
        body, operands, scratch = functools.partial(_mm_kernel, act=act, w_dims=w_dims), (a, w), []
    else:
        assert not w_transposed
        body, operands = functools.partial(_mm_scaled_kernel, act=act), (a, w, part)
        in_specs.append(pl.BlockSpec((tm, part.shape[1]), lambda i, j: (i, 0)))
        scratch = [pltpu.VMEM((tm, 1), F32)]
    return pl.pallas_call(
        body,
        grid=(m // tm, n // tn),
        in_specs=in_specs,
        out_specs=pl.BlockSpec((tm, tn), lambda i, j: (i, j)),
        out_shape=jax.ShapeDtypeStruct((m, n), out_dtype),
        scratch_shapes=scratch,
        compiler_params=_params("parallel", "arbitrary"),
        name=name,
    )(*operands)


def _cast_rows_kernel(w_ref, o_ref):
    o_ref[...] = w_ref[...].astype(o_ref.dtype)


def _cast_rows(w, row0, nrows, tr=512):
    k = w.shape[1]
    tr = math.gcd(nrows, tr)
    align = math.gcd(math.gcd(row0, tr), 64)
    return pl.pallas_call(
        _cast_rows_kernel,
        grid=(nrows // tr,),
        in_specs=[pl.BlockSpec((pl.Element(tr), pl.Element(k)),
                               lambda i: (pl.multiple_of(row0 + i * tr, align), 0))],
        out_specs=pl.BlockSpec((tr, k), lambda i: (i, 0)),
        out_shape=jax.ShapeDtypeStruct((nrows, k), BF16),
        compiler_params=_params("parallel"),
        name="cast_weight_rows",
    )(w)


def _inproj_kernel(u_ref, wt_ref, wft_ref, o_ref, f_ref):
    @pl.when(pl.program_id(1) == 0)
    def _():
        f_ref[...] = lax.dot_general(u_ref[...], wft_ref[...], NT_DIMS, preferred_element_type=F32)

    o_ref[...] = lax.dot_general(u_ref[...], wt_ref[...], NT_DIMS, preferred_element_type=F32).astype(o_ref.dtype)


def _inproj(u, wt, wft, tm=1024, tn=1024):
    m, k = u.shape
    n, nf = wt.shape[0], wft.shape[0]
    tm, tn = _tile(m, tm), _tile(n, tn)
    return pl.pallas_call(
        _inproj_kernel,
        grid=(m // tm, n // tn),
        in_specs=[pl.BlockSpec((tm, k), lambda i, j: (i, 0)), pl.BlockSpec((tn, k), lambda i, j: (j, 0)),
                  pl.BlockSpec((nf, k), lambda i, j: (0, 0))],
        out_specs=[pl.BlockSpec((tm, tn), lambda i, j: (i, j)), pl.BlockSpec((tm, nf), lambda i, j: (i, 0))],
        out_shape=[jax.ShapeDtypeStruct((m, n), BF16), jax.ShapeDtypeStruct((m, nf), F32)],
        compiler_params=_params("parallel", "arbitrary"),
        name="inproj_fox",
    )(u, wt, wft)


def _outproj_kernel(a_ref, b_ref, wa_ref, wb_ref, x_ref, o_ref, ob_ref, part_ref):
    acc = jnp.dot(a_ref[...], wa_ref[...], preferred_element_type=F32)
    acc = acc + jnp.dot(b_ref[...], wb_ref[...], preferred_element_type=F32)
    h = x_ref[...] + acc
    o_ref[...] = h
    ob_ref[...] = h.astype(ob_ref.dtype)
    part_ref[...] = _fold_sq(h)


def _outproj(a, b, w, x, tm=1024, tn=512):
    m, ka = a.shape
    kb = b.shape[1]
    n = w.shape[1]
    assert ka == kb and w.shape[0] == ka + kb
    tm, tn = _tile(m, tm), _tile(n, tn)
    tile = pl.BlockSpec((tm, tn), lambda i, j: (i, j))
    return pl.pallas_call(
        _outproj_kernel,
        grid=(m // tm, n // tn),
        in_specs=[pl.BlockSpec((tm, ka), lambda i, j: (i, 0)), pl.BlockSpec((tm, kb), lambda i, j: (i, 0)),
                  pl.BlockSpec((ka, tn), lambda i, j: (0, j)), pl.BlockSpec((kb, tn), lambda i, j: (1, j)), tile],
        out_specs=[tile, tile, pl.BlockSpec((tm, HEAD_DIM), lambda i, j: (i, j))],
        out_shape=[jax.ShapeDtypeStruct((m, n), F32), jax.ShapeDtypeStruct((m, n), BF16),
                   jax.ShapeDtypeStruct((m, n // tn * HEAD_DIM), F32)],
        compiler_params=_params("parallel", "arbitrary"),
        name="outproj",
    )(a, b, w, w, x)


def _mm_res_kernel(a_ref, w_ref, r_ref, o_ref):
    k = pl.program_id(2)
    acc = jnp.dot(a_ref[...], w_ref[...], preferred_element_type=F32)

    @pl.when(k == 0)
    def _():
        o_ref[...] = r_ref[...] + acc

    @pl.when(k > 0)
    def _():
        o_ref[...] += acc


def _matmul_residual(a, w, r, tm=1024, tn=1024, tk=4096):
    m, kd = a.shape
    n = w.shape[1]
    tm, tn, tk = _tile(m, tm), _tile(n, tn), _tile(kd, tk)
    return pl.pallas_call(
        _mm_res_kernel,
        grid=(m // tm, n // tn, kd // tk),
        in_specs=[pl.BlockSpec((tm, tk), lambda i, j, k: (i, k)), pl.BlockSpec((tk, tn), lambda i, j, k: (k, j)),
                  pl.BlockSpec((tm, tn), lambda i, j, k: (i, j))],
        out_specs=pl.BlockSpec((tm, tn), lambda i, j, k: (i, j)),
        out_shape=jax.ShapeDtypeStruct((m, n), F32),
        compiler_params=_params("parallel", "parallel", "arbitrary", vmem_limit=V7X_VMEM_LIMIT_HIGH),
        name="mlp_down",
    )(a, w, r)


def _split3(x):
    hi = x.astype(BF16).astype(F32)
    r = x - hi
    mid = r.astype(BF16).astype(F32)
    return hi, mid, r - mid


def _fcum_kernel(f_ref, bias_ref, o_ref, carry_ref, *, n_heads):
    t = pl.program_id(1)

    @pl.when(t == 0)
    def _():
        carry_ref[...] = jnp.zeros_like(carry_ref)

    x = f_ref[...] + bias_ref[...]
    ls = jnp.minimum(x, 0.0) - jnp.log1p(jnp.exp(-jnp.abs(x)))
    tc = x.shape[0]
    r = lax.broadcasted_iota(jnp.int32, (tc, tc), 0)
    c = lax.broadcasted_iota(jnp.int32, (tc, tc), 1)
    tri = jnp.where(c <= r, 1.0, 0.0).astype(F32)
    cs = jnp.dot(tri, ls, preferred_element_type=F32, precision=lax.Precision.HIGHEST) + carry_ref[...]
    carry_ref[...] = cs[tc - 1:tc, :]
    lane = lax.broadcasted_iota(jnp.int32, (tc, HEAD_DIM), 1)
    for h in range(n_heads):
        hi, mid, lo = _split3(jnp.broadcast_to(cs[:, h:h + 1] * (-LOG2E), (tc, HEAD_DIM)))
        tile = jnp.where(lane == 0, hi, jnp.where(lane == 1, mid, jnp.where(lane == 2, lo, 0.0)))
        o_ref[:, h * HEAD_DIM:(h + 1) * HEAD_DIM] = tile.astype(o_ref.dtype)


def _forget_bias_tiles(f_logit, bias, batch, seq, n_heads):
    tc = _tile(seq, 512)
    nt = seq // tc
    return pl.pallas_call(
        functools.partial(_fcum_kernel, n_heads=n_heads),
        grid=(batch, nt),
        in_specs=[pl.BlockSpec((tc, HEAD_DIM), lambda b, t: (b * nt + t, 0)),
                  pl.BlockSpec((1, HEAD_DIM), lambda b, t: (0, 0))],
        out_specs=pl.BlockSpec((tc, n_heads * HEAD_DIM), lambda b, t: (b * nt + t, 0)),
        out_shape=jax.ShapeDtypeStruct((batch * seq, n_heads * HEAD_DIM), BF16),
        scratch_shapes=[pltpu.VMEM((1, HEAD_DIM), F32)],
        compiler_params=_params("parallel", "arbitrary"),
        name="fox_forget_cumsum",
    )(f_logit, bias)


def _head_rmsnorm(x, g):
    x = x.astype(F32)
    ms = jnp.mean(x * x, axis=-1, keepdims=True)
    return (x * lax.rsqrt(ms + RMS_EPS) * g).astype(BF16)


def _attn_kernel(q_ref, k_ref, kb_ref, v_ref, gq_ref, gk_ref, o_ref, kp_ref, s_ref, m_ref, l_ref, acc_ref, *,
                 blk, heads):
    qi = pl.program_id(2)
    hcols = [slice(hh * HEAD_DIM, (hh + 1) * HEAD_DIM) for hh in range(heads)]
    seq = k_ref.shape[0]

    @pl.when(qi == 0)
    def _():
        def fill(c, carry):
            rows = pl.ds(pl.multiple_of(c * blk, blk), blk)
            for hh, cols in enumerate(hcols):
                kp_ref[hh, rows, :HEAD_DIM] = _head_rmsnorm(k_ref[rows, cols], gk_ref[...])
                kp_ref[hh, rows, HEAD_DIM:] = kb_ref[rows, cols]
            return carry
        lax.fori_loop(0, seq // blk, fill, 0)

    lane = lax.broadcasted_iota(jnp.int32, (blk, HEAD_DIM), 1)
    q_ones = jnp.where(lane < 3, 1.0, 0.0).astype(BF16)
    key_pos = lax.broadcasted_iota(jnp.int32, (blk, blk), 0)
    qry_pos = lax.broadcasted_iota(jnp.int32, (blk, blk), 1)

    for hh, cols in enumerate(hcols):
        q2 = jnp.concatenate([_head_rmsnorm(q_ref[:, cols], gq_ref[...]), q_ones], axis=1)

        def tile_rows(t):
            j = jnp.maximum(qi - t, 0)
            return pl.ds(pl.multiple_of(j * blk, blk), blk)

        def scores(t, slot):
            s_ref[slot] = lax.dot_general(kp_ref[hh, tile_rows(t), :], q2, NT_DIMS, preferred_element_type=F32)

        def update(t, slot):
            s = s_ref[slot]
            m = m_ref[...]
            m_new = jnp.maximum(m, jnp.max(s, axis=0, keepdims=True))
            p = jnp.exp2(s - m_new)
            alpha = jnp.exp2(m - m_new)
            pv = lax.dot_general(v_ref[tile_rows(t), cols], p.astype(BF16), TN_DIMS, preferred_element_type=F32)
            m_ref[...] = m_new
            l_ref[...] = alpha * l_ref[...] + jnp.sum(p, axis=0, keepdims=True)
            acc_ref[...] = alpha * acc_ref[...] + pv

        def pair(t0):
            scores(t0 + 1, 1)
            update(t0, 0)
            scores(t0 + 2, 0)
            update(t0 + 1, 1)

        def quad(k, carry):
            pair(4 * k)
            pair(4 * k + 2)
            return carry

        n = qi + 1
        m_ref[...] = jnp.full(m_ref.shape, -jnp.inf, F32)
        l_ref[...] = jnp.zeros(l_ref.shape, F32)
        acc_ref[...] = jnp.zeros(acc_ref.shape, F32)
        scores(0, 0)
        s_ref[0] = jnp.where(key_pos <= qry_pos, s_ref[0], -jnp.inf)
        lax.fori_loop(0, n // 4, quad, 0)

        @pl.when((n // 2) % 2 == 1)
        def _():
            pair(4 * (n // 4))

        @pl.when(n % 2 == 1)
        def _():
            update(n - 1, 0)

        o_ref[:, cols] = (acc_ref[...] / l_ref[...]).T.astype(o_ref.dtype)


def _fox_attention(proj, kbias, gq, gk, batch, seq, n_heads, blk=512, heads=2):
    blk = _tile(seq, blk)
    heads = _tile(n_heads, heads)
    nq = seq // blk
    hblk = n_heads // heads
    w = heads * HEAD_DIM
    gain = pl.BlockSpec((1, HEAD_DIM), lambda b, h, i: (0, 0))
    return pl.pallas_call(
        functools.partial(_attn_kernel, blk=blk, heads=heads),
        grid=(batch, hblk, nq),
        in_specs=[pl.BlockSpec((blk, w), lambda b, h, i: (b * nq + i, h)),
                  pl.BlockSpec((seq, w), lambda b, h, i: (b, hblk + h)),
                  pl.BlockSpec((seq, w), lambda b, h, i: (b, h)),
                  pl.BlockSpec((seq, w), lambda b, h, i: (b, 2 * hblk + h)), gain, gain],
        out_specs=pl.BlockSpec((blk, w), lambda b, h, i: (b * nq + i, h)),
        out_shape=jax.ShapeDtypeStruct((batch * seq, n_heads * HEAD_DIM), BF16),
        scratch_shapes=[pltpu.VMEM((heads, seq, 2 * HEAD_DIM), BF16), pltpu.VMEM((2, blk, blk), F32),
                        pltpu.VMEM((1, blk), F32), pltpu.VMEM((1, blk), F32), pltpu.VMEM((HEAD_DIM, blk), F32)],
        compiler_params=_params("parallel", "parallel", "arbitrary"),
        name="fox_attention",
    )(proj, proj, kbias, proj, gq, gk)


def _hgrn_kernel(q_ref, z_ref, v_ref, gate_ref, lbl_ref, gn_ref, o_ref, st_ref, rb_ref, *, layer, heads, chunks):
    C, c = HGRN_CHUNK, HGRN_SUB
    t = pl.program_id(2)

    @pl.when(t == 0)
    def _():
        st_ref[...] = jnp.zeros_like(st_ref)

    logits = lbl_ref[...]
    e = jnp.exp(logits - jnp.max(logits, axis=0, keepdims=True))
    lb_all = jnp.sum(e[:layer + 1], axis=0, keepdims=True) / jnp.sum(e, axis=0, keepdims=True)

    row = lax.broadcasted_iota(jnp.int32, (C, C), 0)
    col = lax.broadcasted_iota(jnp.int32, (C, C), 1)
    ltri = jnp.where(col <= row, 1.0, 0.0).astype(BF16)
    sizes = [c << l for l in range(1, (C // c).bit_length())]
    xor = row ^ col
    level = jnp.zeros((C, C), jnp.int32)
    for l, m in enumerate(sizes):
        level = jnp.where(xor >= m // 2, l + 1, level)
    level = jnp.where(col > row, -1, level)
    lane = lax.broadcasted_iota(jnp.int32, (c, C), 1)

    def chunk_body(ci, carry):
        r0 = pl.multiple_of(ci * C, C)
        for hh in range(heads):
            cols = slice(hh * HEAD_DIM, (hh + 1) * HEAD_DIM)
            lb = lb_all[:, cols]
            q = q_ref[pl.ds(r0, C), cols].astype(F32)
            z = z_ref[pl.ds(r0, C), cols].astype(F32)
            v = v_ref[pl.ds(r0, C), cols]
            gate = gate_ref[pl.ds(r0, C), cols].astype(F32)

            ez = jnp.exp(-jnp.abs(z))
            inv = 1.0 / (1.0 + ez)
            sig_z = jnp.where(z >= 0, 1.0, ez) * inv
            sig_mz = jnp.where(z >= 0, ez, 1.0) * inv
            kk = (1.0 - lb) * sig_mz
            qs = q * _sigmoid(q)
            log2f = jnp.concatenate(_split3(jnp.log2(lb + (1.0 - lb) * sig_z)), axis=1).astype(BF16)
            cs = jnp.dot(ltri, log2f, preferred_element_type=F32)
            b = cs[:, :HEAD_DIM] + cs[:, HEAD_DIM:2 * HEAD_DIM] + cs[:, 2 * HEAD_DIM:]
            b_last = b[C - 1:C, :]

            st = st_ref[hh]
            o = lax.dot_general((qs * jnp.exp2(b)).astype(BF16), st.astype(BF16), NT_DIMS,
                                preferred_element_type=F32)

            rb_ref[2 * hh] = kk
            rb_ref[2 * hh + 1] = b
            blocks = []
            for i in range(C // c):
                rows = slice(i * c, (i + 1) * c)
                q_i, b_i = qs[rows], b[rows]
                blk = jnp.zeros((c, C), F32)
                for s in range(i * c, (i + 1) * c):
                    w = q_i * rb_ref[2 * hh, s:s + 1, :] * jnp.exp2(b_i - rb_ref[2 * hh + 1, s:s + 1, :])
                    blk = jnp.where(lane == s, jnp.sum(w, axis=-1, keepdims=True), blk)
                blocks.append(blk)
            a = jnp.where(level == 0, jnp.concatenate(blocks, axis=0), 0.0)
            for l, m in enumerate(sizes):
                edge = jnp.concatenate(
                    [jnp.broadcast_to(b[j * m + m // 2 - 1:j * m + m // 2, :], (m, HEAD_DIM)) for j in range(C // m)],
                    axis=0)
                x = jnp.exp2(-jnp.abs(b - edge))
                pair = lax.dot_general((qs * x).astype(BF16), (kk * x).astype(BF16), NT_DIMS,
                                       preferred_element_type=F32)
                a = jnp.where(level == l + 1, pair, a)
            o = o + jnp.dot(a.astype(BF16), v, preferred_element_type=F32)

            k_d = (kk * jnp.exp2(b_last - b)).astype(BF16)
            st_ref[hh] = st * jnp.exp2(b_last) + lax.dot_general(v, k_d, TN_DIMS, preferred_element_type=F32)

            ms = jnp.mean(o * o, axis=-1, keepdims=True)
            on = o * lax.rsqrt(ms + RMS_EPS) * gn_ref[...]
            o_ref[pl.ds(r0, C), cols] = (on * (gate * _sigmoid(gate))).astype(o_ref.dtype)
        return carry

    lax.fori_loop(0, chunks, chunk_body, 0)


def _hgrn2(proj, lb_logits, gn, layer, batch, seq, n_heads, col0, heads=4, rows=512):
    heads = _tile(n_heads, heads)
    rows = _tile(seq, rows)
    assert rows % HGRN_CHUNK == 0 and col0 % heads == 0
    nt = seq // rows
    w = heads * HEAD_DIM
    hblk = n_heads // heads

    def in_spec(group):
        return pl.BlockSpec((rows, w), lambda b, h, t: (b * nt + t, col0 // heads + group * hblk + h))

    return pl.pallas_call(
        functools.partial(_hgrn_kernel, layer=layer, heads=heads, chunks=rows // HGRN_CHUNK),
        grid=(batch, hblk, nt),
        in_specs=[in_spec(0), in_spec(1), in_spec(2), in_spec(3),
                  pl.BlockSpec((lb_logits.shape[0], w), lambda b, h, t: (0, h)),
                  pl.BlockSpec((1, HEAD_DIM), lambda b, h, t: (0, 0))],
        out_specs=pl.BlockSpec((rows, w), lambda b, h, t: (b * nt + t, h)),
        out_shape=jax.ShapeDtypeStruct((batch * seq, n_heads * HEAD_DIM), BF16),
        scratch_shapes=[pltpu.VMEM((heads, HEAD_DIM, HEAD_DIM), F32),
                        pltpu.VMEM((2 * heads, HGRN_CHUNK, HEAD_DIM), F32)],
        compiler_params=_params("parallel", "parallel", "arbitrary"),
        name="hgrn2",
    )(proj, proj, proj, proj, lb_logits, gn)


def _ple_embed_kernel(p_ref, w_ref, g_ref, o_ref):
    e = jnp.dot(p_ref[...].astype(BF16), w_ref[...], preferred_element_type=F32)
    ms = jnp.mean(e * e, axis=-1, keepdims=True)
    o_ref[...] = (e * lax.rsqrt(ms + RMS_EPS) * g_ref[...]).astype(o_ref.dtype)


def _ple_embed(p, w, g):
    m, kd = p.shape
    n = w.shape[1]
    tm = _tile(m, 256)
    return pl.pallas_call(
        _ple_embed_kernel,
        grid=(m // tm,),
        in_specs=[pl.BlockSpec((tm, kd), lambda i: (i, 0)), pl.BlockSpec((kd, n), lambda i: (0, 0)),
                  pl.BlockSpec((1, n), lambda i: (0, 0))],
        out_specs=pl.BlockSpec((tm, n), lambda i: (i, 0)),
        out_shape=jax.ShapeDtypeStruct((m, n), BF16),
        compiler_params=_params("parallel"),
        name="ple_embed",
    )(p, w, g.reshape(1, n).astype(F32))


def _ple_gate_kernel(a_ref, w_ref, e_ref, h_ref, o_ref):
    gate = _sigmoid(jnp.dot(a_ref[...], w_ref[...], preferred_element_type=F32))
    o_ref[...] = h_ref[...] + gate * e_ref[...]


def _ple_gate(a, w, e, h, tm=1024, tn=512):
    m, kd = a.shape
    n = w.shape[1]
    tm, tn = _tile(m, tm), _tile(n, tn)
    return pl.pallas_call(
        _ple_gate_kernel,
        grid=(m // tm, n // tn),
        in_specs=[pl.BlockSpec((tm, kd), lambda i, j: (i, 0)), pl.BlockSpec((kd, tn), lambda i, j: (0, j)),
                  pl.BlockSpec((tm, tn), lambda i, j: (i, j)), pl.BlockSpec((tm, tn), lambda i, j: (i, j))],
        out_specs=pl.BlockSpec((tm, tn), lambda i, j: (i, j)),
        out_shape=jax.ShapeDtypeStruct((m, n), F32),
        compiler_params=_params("parallel", "arbitrary"),
        name="ple_gate",
    )(a, w, e, h)


def _layer(h, p, layer, batch, seq, norm_mix_g, w_in, fox_f_bias, fox_q_norm_g, fox_k_norm_g, hgrn_lb_logits,
           hgrn_norm_g, w_out, norm_mlp_g, w_up, w_down, ple_norm_g, w_ple_gate, w_ple_proj, ple_post_g):
    d = h.shape[1]
    d_a = d_b = d // 2
    n_a, n_b = d_a // HEAD_DIM, d_b // HEAD_DIM
    assert n_a <= HEAD_DIM and w_in.shape[1] == 3 * d_a + n_a + 4 * d_b

    w_in_t = jnp.swapaxes(w_in, 0, 1)
    w_fox = _cast_rows(w_in_t, 0, 3 * d_a)
    w_fa = _cast_rows(w_in_t, 3 * d_a, HEAD_DIM)
    w_hgrn = _cast_rows(w_in_t, 3 * d_a + n_a, 4 * d_b)
    q_gain = (fox_q_norm_g.astype(F32) * (LOG2E / math.sqrt(HEAD_DIM))).reshape(1, HEAD_DIM)
    k_gain = fox_k_norm_g.astype(F32).reshape(1, HEAD_DIM)

    u = _rmsnorm(h, norm_mix_g)
    proj_a, f_logit = _inproj(u, w_fox, w_fa)
    proj_b = _matmul(u, w_hgrn, BF16, w_transposed=True, name="inproj_hgrn")

    f_bias = jnp.pad(fox_f_bias.astype(F32), (0, HEAD_DIM - n_a)).reshape(1, HEAD_DIM)
    kbias = _forget_bias_tiles(f_logit, f_bias, batch, seq, n_a)
    out_a = _fox_attention(proj_a, kbias, q_gain, k_gain, batch, seq, n_a)

    out_b = _hgrn2(proj_b, hgrn_lb_logits.astype(F32), hgrn_norm_g.astype(F32).reshape(1, HEAD_DIM), layer,
                   batch, seq, n_b, col0=0)

    h, hb, part = _outproj(out_a, out_b, w_out.astype(BF16), h)
    w_up_g = (norm_mlp_g.astype(F32)[:, None] * w_up).astype(BF16)
    hid = _matmul(hb, w_up_g, BF16, act="relu2", part=part, name="mlp_up")
    h = _matmul_residual(hid, w_down.astype(BF16), h)

    e = _ple_embed(p, w_ple_proj.astype(BF16), ple_post_g)
    return _ple_gate(_rmsnorm(h, ple_norm_g), w_ple_gate.astype(BF16), e, h)


def kernel(x, p, norm_mix_g, w_in, fox_f_bias, fox_q_norm_g, fox_k_norm_g, hgrn_lb_logits, hgrn_norm_g, w_out,
           norm_mlp_g, w_up, w_down, ple_norm_g, w_ple_gate, w_ple_proj, ple_post_g):
    batch, seq, d = x.shape
    h = x.reshape(batch * seq, d)
    for i in range(w_in.shape[0]):
        h = _layer(h, p[i].reshape(batch * seq, -1), i, batch, seq, norm_mix_g[i], w_in[i], fox_f_bias[i],
                   fox_q_norm_g[i], fox_k_norm_g[i], hgrn_lb_logits, hgrn_norm_g[i], w_out[i], norm_mlp_g[i],
                   w_up[i], w_down[i], ple_norm_g[i], w_ple_gate[i], w_ple_proj[i], ple_post_g[i])
    return h.reshape(batch, seq, d)
```

```python
import functools
import math

import jax
import jax.numpy as jnp
from jax import lax
from jax.experimental import pallas as pl
from jax.experimental.pallas import tpu as pltpu

F32, BF16 = jnp.float32, jnp.bfloat16
HEAD_DIM = 128
RMS_EPS = 1e-6
LOG2E = math.log2(math.e)
HGRN_CHUNK = 128
HGRN_SUB = 8
V7X_VMEM_LIMIT = 56 * 1024 * 1024
V7X_VMEM_LIMIT_HIGH = 60 * 1024 * 1024

NT_DIMS = (((1,), (1,)), ((), ()))
TN_DIMS = (((0,), (0,)), ((), ()))


def _params(*semantics, vmem_limit=V7X_VMEM_LIMIT):
    return pltpu.CompilerParams(dimension_semantics=semantics, vmem_limit_bytes=vmem_limit)


def _tile(dim, pref):
    t = min(dim, pref)
    assert dim % t == 0, (dim, pref)
    return t


def _sigmoid(x):
    return 1.0 / (1.0 + jnp.exp(-x))


def _rmsnorm_kernel(x_ref, g_ref, o_ref):
    x = x_ref[...]
    ms = jnp.mean(x * x, axis=-1, keepdims=True)
    o_ref[...] = (x * lax.rsqrt(ms + RMS_EPS) * g_ref[...]).astype(o_ref.dtype)


def _rmsnorm(x, g, out_dtype=BF16):
    m, d = x.shape
    tm = _tile(m, 256)
    return pl.pallas_call(
        _rmsnorm_kernel,
        grid=(m // tm,),
        in_specs=[pl.BlockSpec((tm, d), lambda i: (i, 0)), pl.BlockSpec((1, d), lambda i: (0, 0))],
        out_specs=pl.BlockSpec((tm, d), lambda i: (i, 0)),
        out_shape=jax.ShapeDtypeStruct((m, d), out_dtype),
        compiler_params=_params("parallel"),
        name="rmsnorm",
    )(x, g.reshape(1, d).astype(F32))


def _row_scale(part_ref, d):
    return lax.rsqrt(jnp.sum(part_ref[...], axis=-1, keepdims=True) * (1.0 / d) + RMS_EPS)


def _fold_sq(x):
    sq = x * x
    out = sq[:, :HEAD_DIM]
    for t in range(1, x.shape[1] // HEAD_DIM):
        out = out + sq[:, t * HEAD_DIM:(t + 1) * HEAD_DIM]
    return out


def _mm_kernel(*refs, act, w_dims, scaled, n_side):
    a_ref, w_ref = refs[:2]
    side_in = refs[2 + scaled:2 + scaled + n_side]
    o_ref = refs[2 + scaled + n_side]
    side_out = refs[3 + scaled + n_side:3 + scaled + 2 * n_side]
    for src, dst in zip(side_in, side_out):
        dst[...] = src[...].astype(dst.dtype)

    if scaled:
        part_ref, rs_ref = refs[2], refs[-1]

        @pl.when(pl.program_id(1) == 0)
        def _():
            rs_ref[...] = _row_scale(part_ref, a_ref.shape[1])

    acc = lax.dot_general(a_ref[...], w_ref[...], w_dims, preferred_element_type=F32)
    if scaled:
        acc = acc * rs_ref[...]
    if act == "relu2":
        acc = jnp.square(jnp.maximum(acc, 0.0))
    o_ref[...] = acc.astype(o_ref.dtype)


def _matmul(a, w, out_dtype, act=None, part=None, w_transposed=False, side=(), tm=1024, tn=1024,
            vmem_limit=V7X_VMEM_LIMIT, name="matmul"):
    m, k = a.shape
    n = w.shape[0] if w_transposed else w.shape[1]
    tm, tn = _tile(m, tm), _tile(n, tn)
    ni, nj = m // tm, n // tn
    w_spec = pl.BlockSpec((tn, k), lambda i, j: (j, 0)) if w_transposed else pl.BlockSpec((k, tn), lambda i, j: (0, j))
    in_specs, operands = [pl.BlockSpec((tm, k), lambda i, j: (i, 0)), w_spec], [a, w]
    scratch = []
    if part is not None:
        in_specs.append(pl.BlockSpec((tm, part.shape[1]), lambda i, j: (i, 0)))
        operands.append(part)
        scratch.append(pltpu.VMEM((tm, 1), F32))
    out_specs = [pl.BlockSpec((tm, tn), lambda i, j: (i, j))]
    out_shape = [jax.ShapeDtypeStruct((m, n), out_dtype)]
    for sw in side:
        slab = pl.BlockSpec((sw.shape[0] // (ni * nj), sw.shape[1]), lambda i, j: (i * nj + j, 0))
        assert sw.shape[0] % (ni * nj) == 0
        in_specs.append(slab)
        operands.append(sw)
        out_specs.append(slab)
        out_shape.append(jax.ShapeDtypeStruct(sw.shape, BF16))
    w_dims = NT_DIMS if w_transposed else (((1,), (0,)), ((), ()))
    outs = pl.pallas_call(
        functools.partial(_mm_kernel, act=act, w_dims=w_dims, scaled=part is not None, n_side=len(side)),
        grid=(ni, nj),
        in_specs=in_specs,
        out_specs=out_specs,
        out_shape=out_shape,
        scratch_shapes=scratch,
        compiler_params=_params("parallel", "arbitrary", vmem_limit=vmem_limit),
        name=name,
    )(*operands)
    return outs if side else outs[0]


def _cast_rows_kernel(w_ref, o_ref):
    o_ref[...] = w_ref[...].astype(o_ref.dtype)


def _cast_rows(w, row0, nrows, tr=512):
    k = w.shape[1]
    tr = math.gcd(nrows, tr)
    align = math.gcd(math.gcd(row0, tr), 64)
    return pl.pallas_call(
        _cast_rows_kernel,
        grid=(nrows // tr,),
        in_specs=[pl.BlockSpec((pl.Element(tr), pl.Element(k)),
                               lambda i: (pl.multiple_of(row0 + i * tr, align), 0))],
        out_specs=pl.BlockSpec((tr, k), lambda i: (i, 0)),
        out_shape=jax.ShapeDtypeStruct((nrows, k), BF16),
        compiler_params=_params("parallel"),
        name="cast_weight_rows",
    )(w)


def _inproj_kernel(u_ref, wt_ref, wft_ref, o_ref, f_ref):
    @pl.when(pl.program_id(1) == 0)
    def _():
        f_ref[...] = lax.dot_general(u_ref[...], wft_ref[...], NT_DIMS, preferred_element_type=F32)

    o_ref[...] = lax.dot_general(u_ref[...], wt_ref[...], NT_DIMS, preferred_element_type=F32).astype(o_ref.dtype)


def _inproj(u, wt, wft, tm=1024, tn=1024):
    m, k = u.shape
    n, nf = wt.shape[0], wft.shape[0]
    tm, tn = _tile(m, tm), _tile(n, tn)
    return pl.pallas_call(
        _inproj_kernel,
        grid=(m // tm, n // tn),
        in_specs=[pl.BlockSpec((tm, k), lambda i, j: (i, 0)), pl.BlockSpec((tn, k), lambda i, j: (j, 0)),
                  pl.BlockSpec((nf, k), lambda i, j: (0, 0))],
        out_specs=[pl.BlockSpec((tm, tn), lambda i, j: (i, j)), pl.BlockSpec((tm, nf), lambda i, j: (i, 0))],
        out_shape=[jax.ShapeDtypeStruct((m, n), BF16), jax.ShapeDtypeStruct((m, nf), F32)],
        compiler_params=_params("parallel", "arbitrary"),
        name="inproj_fox",
    )(u, wt, wft)


def _outproj_kernel(a_ref, b_ref, wa_ref, wb_ref, x_ref, g_ref, o_ref, ob_ref, part_ref):
    acc = jnp.dot(a_ref[...], wa_ref[...], preferred_element_type=F32)
    acc = acc + jnp.dot(b_ref[...], wb_ref[...], preferred_element_type=F32)
    h = x_ref[...] + acc
    o_ref[...] = h
    ob_ref[...] = (h * g_ref[...]).astype(ob_ref.dtype)
    part_ref[...] = _fold_sq(h)


def _outproj(a, b, w, x, g, tm=1024, tn=512):
    m, ka = a.shape
    kb = b.shape[1]
    n = w.shape[1]
    assert ka == kb and w.shape[0] == ka + kb
    tm, tn = _tile(m, tm), _tile(n, tn)
    tile = pl.BlockSpec((tm, tn), lambda i, j: (i, j))
    return pl.pallas_call(
        _outproj_kernel,
        grid=(m // tm, n // tn),
        in_specs=[pl.BlockSpec((tm, ka), lambda i, j: (i, 0)), pl.BlockSpec((tm, kb), lambda i, j: (i, 0)),
                  pl.BlockSpec((ka, tn), lambda i, j: (0, j)), pl.BlockSpec((kb, tn), lambda i, j: (1, j)), tile,
                  pl.BlockSpec((1, tn), lambda i, j: (0, j))],
        out_specs=[tile, tile, pl.BlockSpec((tm, HEAD_DIM), lambda i, j: (i, j))],
        out_shape=[jax.ShapeDtypeStruct((m, n), F32), jax.ShapeDtypeStruct((m, n), BF16),
                   jax.ShapeDtypeStruct((m, n // tn * HEAD_DIM), F32)],
        compiler_params=_params("parallel", "arbitrary"),
        name="outproj",
    )(a, b, w, w, x, g.reshape(1, n).astype(F32))


def _mm_res_kernel(a_ref, w_ref, r_ref, o_ref):
    @pl.when(pl.program_id(2) == 0)
    def _():
        o_ref[...] = r_ref[...]

    o_ref[...] += jnp.dot(a_ref[...], w_ref[...], preferred_element_type=F32)


def _matmul_residual(a, w, r, tm=1024, tn=1024, tk=4096):
    m, kd = a.shape
    n = w.shape[1]
    tm, tn, tk = _tile(m, tm), _tile(n, tn), _tile(kd, tk)
    return pl.pallas_call(
        _mm_res_kernel,
        grid=(m // tm, n // tn, kd // tk),
        in_specs=[pl.BlockSpec((tm, tk), lambda i, j, k: (i, k)), pl.BlockSpec((tk, tn), lambda i, j, k: (k, j)),
                  pl.BlockSpec((tm, tn), lambda i, j, k: (i, j))],
        out_specs=pl.BlockSpec((tm, tn), lambda i, j, k: (i, j)),
        out_shape=jax.ShapeDtypeStruct((m, n), F32),
        compiler_params=_params("parallel", "parallel", "arbitrary", vmem_limit=V7X_VMEM_LIMIT_HIGH),
        name="mlp_down",
    )(a, w, r)


def _split3(x):
    hi = x.astype(BF16).astype(F32)
    r = x - hi
    mid = r.astype(BF16).astype(F32)
    return hi, mid, r - mid


def _fcum_kernel(f_ref, bias_ref, o_ref, carry_ref, *, n_heads):
    t = pl.program_id(1)

    @pl.when(t == 0)
    def _():
        carry_ref[...] = jnp.zeros_like(carry_ref)

    x = f_ref[...] + bias_ref[...]
    ls = jnp.minimum(x, 0.0) - jnp.log1p(jnp.exp(-jnp.abs(x)))
    tc = x.shape[0]
    r = lax.broadcasted_iota(jnp.int32, (tc, tc), 0)
    c = lax.broadcasted_iota(jnp.int32, (tc, tc), 1)
    tri = jnp.where(c <= r, 1.0, 0.0).astype(F32)
    cs = jnp.dot(tri, ls, preferred_element_type=F32, precision=lax.Precision.HIGHEST) + carry_ref[...]
    carry_ref[...] = cs[tc - 1:tc, :]
    lane = lax.broadcasted_iota(jnp.int32, (tc, HEAD_DIM), 1)
    for h in range(n_heads):
        hi, mid, lo = _split3(jnp.broadcast_to(cs[:, h:h + 1] * (-LOG2E), (tc, HEAD_DIM)))
        tile = jnp.where(lane == 0, hi, jnp.where(lane == 1, mid, jnp.where(lane == 2, lo, 0.0)))
        o_ref[:, h * HEAD_DIM:(h + 1) * HEAD_DIM] = tile.astype(o_ref.dtype)


def _forget_bias_tiles(f_logit, bias, batch, seq, n_heads):
    tc = _tile(seq, 512)
    nt = seq // tc
    return pl.pallas_call(
        functools.partial(_fcum_kernel, n_heads=n_heads),
        grid=(batch, nt),
        in_specs=[pl.BlockSpec((tc, HEAD_DIM), lambda b, t: (b * nt + t, 0)),
                  pl.BlockSpec((1, HEAD_DIM), lambda b, t: (0, 0))],
        out_specs=pl.BlockSpec((tc, n_heads * HEAD_DIM), lambda b, t: (b * nt + t, 0)),
        out_shape=jax.ShapeDtypeStruct((batch * seq, n_heads * HEAD_DIM), BF16),
        scratch_shapes=[pltpu.VMEM((1, HEAD_DIM), F32)],
        compiler_params=_params("parallel", "arbitrary"),
        name="fox_forget_cumsum",
    )(f_logit, bias)


def _head_rmsnorm(x, g):
    x = x.astype(F32)
    ms = jnp.mean(x * x, axis=-1, keepdims=True)
    return (x * lax.rsqrt(ms + RMS_EPS) * g).astype(BF16)


def _attn_kernel(q_ref, k_ref, kb_ref, v_ref, gq_ref, gk_ref, o_ref, kp_ref, s_ref, m_ref, l_ref, acc_ref, *,
                 blk, heads):
    qi = pl.program_id(2)
    hcols = [slice(hh * HEAD_DIM, (hh + 1) * HEAD_DIM) for hh in range(heads)]
    seq = k_ref.shape[0]

    @pl.when(qi == 0)
    def _():
        def fill(c, carry):
            rows = pl.ds(pl.multiple_of(c * blk, blk), blk)
            for hh, cols in enumerate(hcols):
                kp_ref[hh, rows, :HEAD_DIM] = _head_rmsnorm(k_ref[rows, cols], gk_ref[...])
                kp_ref[hh, rows, HEAD_DIM:] = kb_ref[rows, cols]
            return carry
        lax.fori_loop(0, seq // blk, fill, 0)

    lane = lax.broadcasted_iota(jnp.int32, (blk, HEAD_DIM), 1)
    q_ones = jnp.where(lane < 3, 1.0, 0.0).astype(BF16)
    key_pos = lax.broadcasted_iota(jnp.int32, (blk, blk), 0)
    qry_pos = lax.broadcasted_iota(jnp.int32, (blk, blk), 1)

    for hh, cols in enumerate(hcols):
        q2 = jnp.concatenate([_head_rmsnorm(q_ref[:, cols], gq_ref[...]), q_ones], axis=1)

        def scores(j, slot):
            rows = pl.ds(pl.multiple_of(j * blk, blk), blk)
            s_ref[slot] = lax.dot_general(kp_ref[hh, rows, :], q2, NT_DIMS, preferred_element_type=F32)

        def update(j, slot, diagonal=False):
            rows = pl.ds(pl.multiple_of(j * blk, blk), blk)
            s = s_ref[slot]
            if diagonal:
                s = jnp.where(key_pos <= qry_pos, s, -jnp.inf)
                m_new = jnp.max(s, axis=0, keepdims=True)
            else:
                m = m_ref[...]
                m_new = jnp.maximum(m, jnp.max(s, axis=0, keepdims=True))
            p = jnp.exp2(s - m_new)
            pv = lax.dot_general(v_ref[rows, cols], p.astype(BF16), TN_DIMS, preferred_element_type=F32)
            m_ref[...] = m_new
            if diagonal:
                l_ref[...] = jnp.sum(p, axis=0, keepdims=True)
                acc_ref[...] = pv
            else:
                alpha = jnp.exp2(m - m_new)
                l_ref[...] = alpha * l_ref[...] + jnp.sum(p, axis=0, keepdims=True)
                acc_ref[...] = alpha * acc_ref[...] + pv

        def pair(t0):
            scores(t0 + 1, 1)
            update(t0, 0)
            scores(t0 + 2, 0)
            update(t0 + 1, 1)

        def quad(k, carry):
            pair(4 * k)
            pair(4 * k + 2)
            return carry

        scores(qi, 1)
        scores(0, 0)
        update(qi, 1, diagonal=True)
        lax.fori_loop(0, qi // 4, quad, 0)

        @pl.when((qi // 2) % 2 == 1)
        def _():
            pair(4 * (qi // 4))

        @pl.when(qi % 2 == 1)
        def _():
            update(qi - 1, 0)

        o_ref[:, cols] = (acc_ref[...] / l_ref[...]).T.astype(o_ref.dtype)


def _fox_attention(proj, kbias, gq, gk, batch, seq, n_heads, blk=512, heads=2):
    blk = _tile(seq, blk)
    heads = _tile(n_heads, heads)
    nq = seq // blk
    hblk = n_heads // heads
    w = heads * HEAD_DIM
    gain = pl.BlockSpec((1, HEAD_DIM), lambda b, h, i: (0, 0))
    return pl.pallas_call(
        functools.partial(_attn_kernel, blk=blk, heads=heads),
        grid=(batch, hblk, nq),
        in_specs=[pl.BlockSpec((blk, w), lambda b, h, i: (b * nq + i, h)),
                  pl.BlockSpec((seq, w), lambda b, h, i: (b, hblk + h)),
                  pl.BlockSpec((seq, w), lambda b, h, i: (b, h)),
                  pl.BlockSpec((seq, w), lambda b, h, i: (b, 2 * hblk + h)), gain, gain],
        out_specs=pl.BlockSpec((blk, w), lambda b, h, i: (b * nq + i, h)),
        out_shape=jax.ShapeDtypeStruct((batch * seq, n_heads * HEAD_DIM), BF16),
        scratch_shapes=[pltpu.VMEM((heads, seq, 2 * HEAD_DIM), BF16), pltpu.VMEM((2, blk, blk), F32),
                        pltpu.VMEM((1, blk), F32), pltpu.VMEM((1, blk), F32), pltpu.VMEM((HEAD_DIM, blk), F32)],
        compiler_params=_params("parallel", "parallel", "arbitrary"),
        name="fox_attention",
    )(proj, proj, kbias, proj, gq, gk)


def _hgrn_kernel(q_ref, z_ref, v_ref, gate_ref, lbl_ref, gn_ref, o_ref, st_ref, rb_ref, *, layer, heads, chunks):
    C, c = HGRN_CHUNK, HGRN_SUB
    t = pl.program_id(2)

    @pl.when(t == 0)
    def _():
        st_ref[...] = jnp.zeros_like(st_ref)

    logits = lbl_ref[...]
    e = jnp.exp(logits - jnp.max(logits, axis=0, keepdims=True))
    lb_all = jnp.sum(e[:layer + 1], axis=0, keepdims=True) / jnp.sum(e, axis=0, keepdims=True)

    row = lax.broadcasted_iota(jnp.int32, (C, C), 0)
    col = lax.broadcasted_iota(jnp.int32, (C, C), 1)
    ltri = jnp.where(col <= row, 1.0, 0.0).astype(BF16)
    sizes = [c << l for l in range(1, (C // c).bit_length())]
    xor = row ^ col
    level = jnp.zeros((C, C), jnp.int32)
    for l, m in enumerate(sizes):
        level = jnp.where(xor >= m // 2, l + 1, level)
    level = jnp.where(col > row, -1, level)
    lane = lax.broadcasted_iota(jnp.int32, (c, C), 1)

    def chunk_body(ci, carry):
        r0 = pl.multiple_of(ci * C, C)
        for hh in range(heads):
            cols = slice(hh * HEAD_DIM, (hh + 1) * HEAD_DIM)
            lb = lb_all[:, cols]
            q = q_ref[pl.ds(r0, C), cols].astype(F32)
            z = z_ref[pl.ds(r0, C), cols].astype(F32)
            v = v_ref[pl.ds(r0, C), cols]
            gate = gate_ref[pl.ds(r0, C), cols].astype(F32)

            ez = jnp.exp(-jnp.abs(z))
            inv = 1.0 / (1.0 + ez)
            sig_z = jnp.where(z >= 0, 1.0, ez) * inv
            sig_mz = jnp.where(z >= 0, ez, 1.0) * inv
            kk = (1.0 - lb) * sig_mz
            qs = q * _sigmoid(q)
            log2f = jnp.concatenate(_split3(jnp.log2(lb + (1.0 - lb) * sig_z)), axis=1).astype(BF16)
            cs = jnp.dot(ltri, log2f, preferred_element_type=F32)
            b = cs[:, :HEAD_DIM] + cs[:, HEAD_DIM:2 * HEAD_DIM] + cs[:, 2 * HEAD_DIM:]
            b_last = b[C - 1:C, :]

            st = st_ref[hh]
            o = lax.dot_general((qs * jnp.exp2(b)).astype(BF16), st.astype(BF16), NT_DIMS,
                                preferred_element_type=F32)

            rb_ref[2 * hh] = kk
            rb_ref[2 * hh + 1] = b
            blocks = []
            for i in range(C // c):
                rows = slice(i * c, (i + 1) * c)
                q_i, b_i = qs[rows], b[rows]
                blk = jnp.zeros((c, C), F32)
                for s in range(i * c, (i + 1) * c):
                    w = q_i * rb_ref[2 * hh, s:s + 1, :] * jnp.exp2(b_i - rb_ref[2 * hh + 1, s:s + 1, :])
                    blk = jnp.where(lane == s, jnp.sum(w, axis=-1, keepdims=True), blk)
                blocks.append(blk)
            a = jnp.where(level == 0, jnp.concatenate(blocks, axis=0), 0.0)
            for l, m in enumerate(sizes):
                edge = jnp.concatenate(
                    [jnp.broadcast_to(b[j * m + m // 2 - 1:j * m + m // 2, :], (m, HEAD_DIM)) for j in range(C // m)],
                    axis=0)
                x = jnp.exp2(-jnp.abs(b - edge))
                pair = lax.dot_general((qs * x).astype(BF16), (kk * x).astype(BF16), NT_DIMS,
                                       preferred_element_type=F32)
                a = jnp.where(level == l + 1, pair, a)
            o = o + jnp.dot(a.astype(BF16), v, preferred_element_type=F32)

            k_d = (kk * jnp.exp2(b_last - b)).astype(BF16)
            st_ref[hh] = st * jnp.exp2(b_last) + lax.dot_general(v, k_d, TN_DIMS, preferred_element_type=F32)

            ms = jnp.mean(o * o, axis=-1, keepdims=True)
            on = o * lax.rsqrt(ms + RMS_EPS) * gn_ref[...]
            o_ref[pl.ds(r0, C), cols] = (on * (gate * _sigmoid(gate))).astype(o_ref.dtype)
        return carry

    lax.fori_loop(0, chunks, chunk_body, 0)


def _hgrn2(proj, lb_logits, gn, layer, batch, seq, n_heads, col0, heads=4, rows=512):
    heads = _tile(n_heads, heads)
    rows = _tile(seq, rows)
    assert rows % HGRN_CHUNK == 0 and col0 % heads == 0
    nt = seq // rows
    w = heads * HEAD_DIM
    hblk = n_heads // heads

    def in_spec(group):
        return pl.BlockSpec((rows, w), lambda b, h, t: (b * nt + t, col0 // heads + group * hblk + h))

    return pl.pallas_call(
        functools.partial(_hgrn_kernel, layer=layer, heads=heads, chunks=rows // HGRN_CHUNK),
        grid=(batch, hblk, nt),
        in_specs=[in_spec(0), in_spec(1), in_spec(2), in_spec(3),
                  pl.BlockSpec((lb_logits.shape[0], w), lambda b, h, t: (0, h)),
                  pl.BlockSpec((1, HEAD_DIM), lambda b, h, t: (0, 0))],
        out_specs=pl.BlockSpec((rows, w), lambda b, h, t: (b * nt + t, h)),
        out_shape=jax.ShapeDtypeStruct((batch * seq, n_heads * HEAD_DIM), BF16),
        scratch_shapes=[pltpu.VMEM((heads, HEAD_DIM, HEAD_DIM), F32),
                        pltpu.VMEM((2 * heads, HGRN_CHUNK, HEAD_DIM), F32)],
        compiler_params=_params("parallel", "parallel", "arbitrary"),
        name="hgrn2",
    )(proj, proj, proj, proj, lb_logits, gn)


def _ple_embed_kernel(p_ref, w_ref, g_ref, o_ref):
    e = jnp.dot(p_ref[...].astype(BF16), w_ref[...], preferred_element_type=F32)
    ms = jnp.mean(e * e, axis=-1, keepdims=True)
    o_ref[...] = (e * lax.rsqrt(ms + RMS_EPS) * g_ref[...]).astype(o_ref.dtype)


def _ple_embed(p, w, g):
    m, kd = p.shape
    n = w.shape[1]
    tm = _tile(m, 256)
    return pl.pallas_call(
        _ple_embed_kernel,
        grid=(m // tm,),
        in_specs=[pl.BlockSpec((tm, kd), lambda i: (i, 0)), pl.BlockSpec((kd, n), lambda i: (0, 0)),
                  pl.BlockSpec((1, n), lambda i: (0, 0))],
        out_specs=pl.BlockSpec((tm, n), lambda i: (i, 0)),
        out_shape=jax.ShapeDtypeStruct((m, n), BF16),
        compiler_params=_params("parallel"),
        name="ple_embed",
    )(p, w, g.reshape(1, n).astype(F32))


def _ple_gate_kernel(a_ref, w_ref, e_ref, h_ref, o_ref):
    gate = _sigmoid(jnp.dot(a_ref[...], w_ref[...], preferred_element_type=F32))
    o_ref[...] = h_ref[...] + gate * e_ref[...]


def _ple_gate(a, w, e, h, tm=1024, tn=512):
    m, kd = a.shape
    n = w.shape[1]
    tm, tn = _tile(m, tm), _tile(n, tn)
    return pl.pallas_call(
        _ple_gate_kernel,
        grid=(m // tm, n // tn),
        in_specs=[pl.BlockSpec((tm, kd), lambda i, j: (i, 0)), pl.BlockSpec((kd, tn), lambda i, j: (0, j)),
                  pl.BlockSpec((tm, tn), lambda i, j: (i, j)), pl.BlockSpec((tm, tn), lambda i, j: (i, j))],
        out_specs=pl.BlockSpec((tm, tn), lambda i, j: (i, j)),
        out_shape=jax.ShapeDtypeStruct((m, n), F32),
        compiler_params=_params("parallel", "arbitrary"),
        name="ple_gate",
    )(a, w, e, h)


def _layer(h, p, layer, batch, seq, norm_mix_g, w_in, fox_f_bias, fox_q_norm_g, fox_k_norm_g, hgrn_lb_logits,
           hgrn_norm_g, w_out, norm_mlp_g, w_up, w_down, ple_norm_g, w_ple_gate, w_ple_proj, ple_post_g):
    d = h.shape[1]
    d_a = d_b = d // 2
    n_a, n_b = d_a // HEAD_DIM, d_b // HEAD_DIM
    assert n_a <= HEAD_DIM and w_in.shape[1] == 3 * d_a + n_a + 4 * d_b

    w_in_t = jnp.swapaxes(w_in, 0, 1)
    w_fox = _cast_rows(w_in_t, 0, 3 * d_a)
    w_fa = _cast_rows(w_in_t, 3 * d_a, HEAD_DIM)
    w_hgrn = _cast_rows(w_in_t, 3 * d_a + n_a, 4 * d_b)
    q_gain = (fox_q_norm_g.astype(F32) * (LOG2E / math.sqrt(HEAD_DIM))).reshape(1, HEAD_DIM)
    k_gain = fox_k_norm_g.astype(F32).reshape(1, HEAD_DIM)

    u = _rmsnorm(h, norm_mix_g)
    proj_a, f_logit = _inproj(u, w_fox, w_fa)
    proj_b, w_up_b, w_out_b = _matmul(u, w_hgrn, BF16, w_transposed=True, side=(w_up, w_out), name="inproj_hgrn")

    f_bias = jnp.pad(fox_f_bias.astype(F32), (0, HEAD_DIM - n_a)).reshape(1, HEAD_DIM)
    kbias = _forget_bias_tiles(f_logit, f_bias, batch, seq, n_a)
    out_a = _fox_attention(proj_a, kbias, q_gain, k_gain, batch, seq, n_a)

    out_b = _hgrn2(proj_b, hgrn_lb_logits.astype(F32), hgrn_norm_g.astype(F32).reshape(1, HEAD_DIM), layer,
                   batch, seq, n_b, col0=0)

    h, hg, part = _outproj(out_a, out_b, w_out_b, h, norm_mlp_g)
    hid, w_down_b, w_gate_b = _matmul(hg, w_up_b, BF16, act="relu2", part=part, side=(w_down, w_ple_gate),
                                      vmem_limit=V7X_VMEM_LIMIT_HIGH, name="mlp_up")
    h = _matmul_residual(hid, w_down_b, h)

    e = _ple_embed(p, w_ple_proj.astype(BF16), ple_post_g)
    return _ple_gate(_rmsnorm(h, ple_norm_g), w_gate_b, e, h)


def kernel(x, p, norm_mix_g, w_in, fox_f_bias, fox_q_norm_g, fox_k_norm_g, hgrn_lb_logits, hgrn_norm_g, w_out,
           norm_mlp_g, w_up, w_down, ple_norm_g, w_ple_gate, w_ple_proj, ple_post_g):
    batch, seq, d = x.shape
    h = x.reshape(batch * seq, d)
    for i in range(w_in.shape[0]):
        h = _layer(h, p[i].reshape(batch * seq, -1), i, batch, seq, norm_mix_g[i], w_in[i], fox_f_bias[i],
                   fox_q_norm_g[i], fox_k_norm_g[i], hgrn_lb_logits, hgrn_norm_g[i], w_out[i], norm_mlp_g[i],
                   w_up[i], w_down[i], ple_norm_g[i], w_ple_gate[i], w_ple_proj[i], ple_post_g[i])
    return h.reshape(batch, seq, d)
```

```python
import functools
import math

import jax
import jax.numpy as jnp
from jax import lax
from jax.experimental import pallas as pl
from jax.experimental.pallas import tpu as pltpu

F32, BF16 = jnp.float32, jnp.bfloat16
HEAD_DIM = 128
RMS_EPS = 1e-6
LOG2E = math.log2(math.e)
HGRN_CHUNK = 128
HGRN_SUB = 8
V7X_VMEM_LIMIT = 56 * 1024 * 1024
V7X_VMEM_LIMIT_HIGH = 60 * 1024 * 1024

NT_DIMS = (((1,), (1,)), ((), ()))
TN_DIMS = (((0,), (0,)), ((), ()))


def _params(*semantics, vmem_limit=V7X_VMEM_LIMIT):
    return pltpu.CompilerParams(dimension_semantics=semantics, vmem_limit_bytes=vmem_limit)


def _tile(dim, pref):
    t = min(dim, pref)
    assert dim % t == 0, (dim, pref)
    return t


def _sigmoid(x):
    return 1.0 / (1.0 + jnp.exp(-x))


def _rmsnorm_kernel(x_ref, g_ref, o_ref):
    x = x_ref[...]
    ms = jnp.mean(x * x, axis=-1, keepdims=True)
    o_ref[...] = (x * lax.rsqrt(ms + RMS_EPS) * g_ref[...]).astype(o_ref.dtype)


def _rmsnorm(x, g, out_dtype=BF16):
    m, d = x.shape
    tm = _tile(m, 256)
    return pl.pallas_call(
        _rmsnorm_kernel,
        grid=(m // tm,),
        in_specs=[pl.BlockSpec((tm, d), lambda i: (i, 0)), pl.BlockSpec((1, d), lambda i: (0, 0))],
        out_specs=pl.BlockSpec((tm, d), lambda i: (i, 0)),
        out_shape=jax.ShapeDtypeStruct((m, d), out_dtype),
        compiler_params=_params("parallel"),
        name="rmsnorm",
    )(x, g.reshape(1, d).astype(F32))


def _row_scale(part_ref, d):
    return lax.rsqrt(jnp.sum(part_ref[...], axis=-1, keepdims=True) * (1.0 / d) + RMS_EPS)


def _fold_sq(x):
    sq = x * x
    out = sq[:, :HEAD_DIM]
    for t in range(1, x.shape[1] // HEAD_DIM):
        out = out + sq[:, t * HEAD_DIM:(t + 1) * HEAD_DIM]
    return out


def _mm_kernel(*refs, act, w_dims, scaled, n_side):
    a_ref, w_ref = refs[:2]
    side_in = refs[2 + scaled:2 + scaled + n_side]
    o_ref = refs[2 + scaled + n_side]
    side_out = refs[3 + scaled + n_side:3 + scaled + 2 * n_side]
    if scaled:
        part_ref, rs_ref = refs[2], refs[-1]

        @pl.when(pl.program_id(1) == 0)
        def _():
            rs_ref[...] = _row_scale(part_ref, a_ref.shape[1])

    for src, dst in zip(side_in, side_out):
        dst[...] = src[...].astype(dst.dtype)
    acc = lax.dot_general(a_ref[...], w_ref[...], w_dims, preferred_element_type=F32)
    if scaled:
        acc = acc * rs_ref[...]
    if act == "relu2":
        acc = jnp.square(jnp.maximum(acc, 0.0))
    o_ref[...] = acc.astype(o_ref.dtype)


def _matmul(a, w, out_dtype, act=None, part=None, w_transposed=False, side=(), tm=1024, tn=1024,
            vmem_limit=V7X_VMEM_LIMIT, name="matmul"):
    m, k = a.shape
    n = w.shape[0] if w_transposed else w.shape[1]
    tm, tn = _tile(m, tm), _tile(n, tn)
    ni, nj = m // tm, n // tn
    w_spec = pl.BlockSpec((tn, k), lambda i, j: (j, 0)) if w_transposed else pl.BlockSpec((k, tn), lambda i, j: (0, j))
    in_specs, operands = [pl.BlockSpec((tm, k), lambda i, j: (i, 0)), w_spec], [a, w]
    scratch = []
    if part is not None:
        in_specs.append(pl.BlockSpec((tm, part.shape[1]), lambda i, j: (i, 0)))
        operands.append(part)
        scratch.append(pltpu.VMEM((tm, 1), F32))
    out_specs = [pl.BlockSpec((tm, tn), lambda i, j: (i, j))]
    out_shape = [jax.ShapeDtypeStruct((m, n), out_dtype)]
    for sw in side:
        slab = pl.BlockSpec((sw.shape[0] // (ni * nj), sw.shape[1]), lambda i, j: (i * nj + j, 0))
        assert sw.shape[0] % (ni * nj) == 0
        in_specs.append(slab)
        operands.append(sw)
        out_specs.append(slab)
        out_shape.append(jax.ShapeDtypeStruct(sw.shape, BF16))
    w_dims = NT_DIMS if w_transposed else (((1,), (0,)), ((), ()))
    outs = pl.pallas_call(
        functools.partial(_mm_kernel, act=act, w_dims=w_dims, scaled=part is not None, n_side=len(side)),
        grid=(ni, nj),
        in_specs=in_specs,
        out_specs=out_specs,
        out_shape=out_shape,
        scratch_shapes=scratch,
        compiler_params=_params("parallel", "arbitrary", vmem_limit=vmem_limit),
        name=name,
    )(*operands)
    return outs if side else outs[0]


def _cast_rows_kernel(w_ref, o_ref):
    o_ref[...] = w_ref[...].astype(o_ref.dtype)


def _cast_rows(w, row0, nrows, tr=512):
    k = w.shape[1]
    tr = math.gcd(nrows, tr)
    align = math.gcd(math.gcd(row0, tr), 64)
    return pl.pallas_call(
        _cast_rows_kernel,
        grid=(nrows // tr,),
        in_specs=[pl.BlockSpec((pl.Element(tr), pl.Element(k)),
                               lambda i: (pl.multiple_of(row0 + i * tr, align), 0))],
        out_specs=pl.BlockSpec((tr, k), lambda i: (i, 0)),
        out_shape=jax.ShapeDtypeStruct((nrows, k), BF16),
        compiler_params=_params("parallel"),
        name="cast_weight_rows",
    )(w)


def _inproj_kernel(u_ref, wt_ref, wft_ref, o_ref, f_ref):
    @pl.when(pl.program_id(1) == 0)
    def _():
        f_ref[...] = lax.dot_general(u_ref[...], wft_ref[...], NT_DIMS, preferred_element_type=F32)

    o_ref[...] = lax.dot_general(u_ref[...], wt_ref[...], NT_DIMS, preferred_element_type=F32).astype(o_ref.dtype)


def _inproj(u, wt, wft, tm=1024, tn=1024):
    m, k = u.shape
    n, nf = wt.shape[0], wft.shape[0]
    tm, tn = _tile(m, tm), _tile(n, tn)
    return pl.pallas_call(
        _inproj_kernel,
        grid=(m // tm, n // tn),
        in_specs=[pl.BlockSpec((tm, k), lambda i, j: (i, 0)), pl.BlockSpec((tn, k), lambda i, j: (j, 0)),
                  pl.BlockSpec((nf, k), lambda i, j: (0, 0))],
        out_specs=[pl.BlockSpec((tm, tn), lambda i, j: (i, j)), pl.BlockSpec((tm, nf), lambda i, j: (i, 0))],
        out_shape=[jax.ShapeDtypeStruct((m, n), BF16), jax.ShapeDtypeStruct((m, nf), F32)],
        compiler_params=_params("parallel", "arbitrary"),
        name="inproj_fox",
    )(u, wt, wft)


def _outproj_kernel(a_ref, b_ref, wa_ref, wb_ref, x_ref, g_ref, o_ref, ob_ref, part_ref):
    acc = jnp.dot(a_ref[...], wa_ref[...], preferred_element_type=F32)
    acc = acc + jnp.dot(b_ref[...], wb_ref[...], preferred_element_type=F32)
    h = x_ref[...] + acc
    o_ref[...] = h
    ob_ref[...] = (h * g_ref[...]).astype(ob_ref.dtype)
    part_ref[...] = _fold_sq(h)


def _outproj(a, b, w, x, g, tm=1024, tn=512):
    m, ka = a.shape
    kb = b.shape[1]
    n = w.shape[1]
    assert ka == kb and w.shape[0] == ka + kb
    tm, tn = _tile(m, tm), _tile(n, tn)
    tile = pl.BlockSpec((tm, tn), lambda i, j: (i, j))
    return pl.pallas_call(
        _outproj_kernel,
        grid=(m // tm, n // tn),
        in_specs=[pl.BlockSpec((tm, ka), lambda i, j: (i, 0)), pl.BlockSpec((tm, kb), lambda i, j: (i, 0)),
                  pl.BlockSpec((ka, tn), lambda i, j: (0, j)), pl.BlockSpec((kb, tn), lambda i, j: (1, j)), tile,
                  pl.BlockSpec((1, tn), lambda i, j: (0, j))],
        out_specs=[tile, tile, pl.BlockSpec((tm, HEAD_DIM), lambda i, j: (i, j))],
        out_shape=[jax.ShapeDtypeStruct((m, n), F32), jax.ShapeDtypeStruct((m, n), BF16),
                   jax.ShapeDtypeStruct((m, n // tn * HEAD_DIM), F32)],
        compiler_params=_params("parallel", "arbitrary"),
        name="outproj",
    )(a, b, w, w, x, g.reshape(1, n).astype(F32))


def _mm_res_kernel(a_ref, w_ref, r_ref, o_ref):
    @pl.when(pl.program_id(2) == 0)
    def _():
        o_ref[...] = r_ref[...]

    o_ref[...] += jnp.dot(a_ref[...], w_ref[...], preferred_element_type=F32)


def _matmul_residual(a, w, r, tm=1024, tn=1024, tk=4096):
    m, kd = a.shape
    n = w.shape[1]
    tm, tn, tk = _tile(m, tm), _tile(n, tn), _tile(kd, tk)
    return pl.pallas_call(
        _mm_res_kernel,
        grid=(m // tm, n // tn, kd // tk),
        in_specs=[pl.BlockSpec((tm, tk), lambda i, j, k: (i, k)), pl.BlockSpec((tk, tn), lambda i, j, k: (k, j)),
                  pl.BlockSpec((tm, tn), lambda i, j, k: (i, j))],
        out_specs=pl.BlockSpec((tm, tn), lambda i, j, k: (i, j)),
        out_shape=jax.ShapeDtypeStruct((m, n), F32),
        compiler_params=_params("parallel", "parallel", "arbitrary", vmem_limit=V7X_VMEM_LIMIT_HIGH),
        name="mlp_down",
    )(a, w, r)


def _split3(x):
    hi = x.astype(BF16).astype(F32)
    r = x - hi
    mid = r.astype(BF16).astype(F32)
    return hi, mid, r - mid


def _fcum_kernel(f_ref, bias_ref, o_ref, carry_ref, *, n_heads):
    t = pl.program_id(1)

    @pl.when(t == 0)
    def _():
        carry_ref[...] = jnp.zeros_like(carry_ref)

    x = f_ref[...] + bias_ref[...]
    ls = jnp.minimum(x, 0.0) - jnp.log1p(jnp.exp(-jnp.abs(x)))
    tc = x.shape[0]
    r = lax.broadcasted_iota(jnp.int32, (tc, tc), 0)
    c = lax.broadcasted_iota(jnp.int32, (tc, tc), 1)
    tri = jnp.where(c <= r, 1.0, 0.0).astype(BF16)
    c3 = jnp.dot(tri, jnp.concatenate(_split3(ls), axis=1).astype(BF16), preferred_element_type=F32)
    cs = c3[:, :HEAD_DIM] + c3[:, HEAD_DIM:2 * HEAD_DIM] + c3[:, 2 * HEAD_DIM:] + carry_ref[...]
    carry_ref[...] = cs[tc - 1:tc, :]
    lane = lax.broadcasted_iota(jnp.int32, (tc, HEAD_DIM), 1)
    for h in range(n_heads):
        hi, mid, lo = _split3(jnp.broadcast_to(cs[:, h:h + 1] * (-LOG2E), (tc, HEAD_DIM)))
        tile = jnp.where(lane == 0, hi, jnp.where(lane == 1, mid, jnp.where(lane == 2, lo, 0.0)))
        o_ref[:, h * HEAD_DIM:(h + 1) * HEAD_DIM] = tile.astype(o_ref.dtype)


def _forget_bias_tiles(f_logit, bias, batch, seq, n_heads):
    tc = _tile(seq, 256)
    nt = seq // tc
    return pl.pallas_call(
        functools.partial(_fcum_kernel, n_heads=n_heads),
        grid=(batch, nt),
        in_specs=[pl.BlockSpec((tc, HEAD_DIM), lambda b, t: (b * nt + t, 0)),
                  pl.BlockSpec((1, HEAD_DIM), lambda b, t: (0, 0))],
        out_specs=pl.BlockSpec((tc, n_heads * HEAD_DIM), lambda b, t: (b * nt + t, 0)),
        out_shape=jax.ShapeDtypeStruct((batch * seq, n_heads * HEAD_DIM), BF16),
        scratch_shapes=[pltpu.VMEM((1, HEAD_DIM), F32)],
        compiler_params=_params("parallel", "arbitrary"),
        name="fox_forget_cumsum",
    )(f_logit, bias)


def _head_rmsnorm(x, g):
    x = x.astype(F32)
    ms = jnp.mean(x * x, axis=-1, keepdims=True)
    return (x * lax.rsqrt(ms + RMS_EPS) * g).astype(BF16)


def _attn_kernel(q_ref, k_ref, kb_ref, v_ref, gq_ref, gk_ref, o_ref, kp_ref, s_ref, m_ref, l_ref, acc_ref, *,
                 blk, heads):
    qi = pl.program_id(2)
    hcols = [slice(hh * HEAD_DIM, (hh + 1) * HEAD_DIM) for hh in range(heads)]
    seq = k_ref.shape[0]

    @pl.when(qi == 0)
    def _():
        def fill(c, carry):
            rows = pl.ds(pl.multiple_of(c * blk, blk), blk)
            for hh, cols in enumerate(hcols):
                kp_ref[hh, rows, :HEAD_DIM] = _head_rmsnorm(k_ref[rows, cols], gk_ref[...])
                kp_ref[hh, rows, HEAD_DIM:] = kb_ref[rows, cols]
            return carry
        lax.fori_loop(0, seq // blk, fill, 0)

    lane = lax.broadcasted_iota(jnp.int32, (blk, HEAD_DIM), 1)
    q_ones = jnp.where(lane < 3, 1.0, 0.0).astype(BF16)
    key_pos = lax.broadcasted_iota(jnp.int32, (blk, blk), 0)
    qry_pos = lax.broadcasted_iota(jnp.int32, (blk, blk), 1)

    q2 = [jnp.concatenate([_head_rmsnorm(q_ref[:, cols], gq_ref[...]), q_ones], axis=1) for cols in hcols]

    def scores(hh, j, slot):
        rows = pl.ds(pl.multiple_of(j * blk, blk), blk)
        s_ref[hh, slot] = lax.dot_general(kp_ref[hh, rows, :], q2[hh], NT_DIMS, preferred_element_type=F32)

    def update(hh, j, slot, diagonal=False):
        rows = pl.ds(pl.multiple_of(j * blk, blk), blk)
        s = s_ref[hh, slot]
        if diagonal:
            s = jnp.where(key_pos <= qry_pos, s, -jnp.inf)
            m_new = jnp.max(s, axis=0, keepdims=True)
        else:
            m = m_ref[hh]
            m_new = jnp.maximum(m, jnp.max(s, axis=0, keepdims=True))
        p = jnp.exp2(s - m_new)
        pv = lax.dot_general(v_ref[rows, hcols[hh]], p.astype(BF16), TN_DIMS, preferred_element_type=F32)
        m_ref[hh] = m_new
        if diagonal:
            l_ref[hh] = jnp.sum(p, axis=0, keepdims=True)
            acc_ref[hh] = pv
        else:
            alpha = jnp.exp2(m - m_new)
            l_ref[hh] = alpha * l_ref[hh] + jnp.sum(p, axis=0, keepdims=True)
            acc_ref[hh] = alpha * acc_ref[hh] + pv

    def each_head(fn, *args, **kwargs):
        for hh in range(heads):
            fn(hh, *args, **kwargs)

    def pair(t0):
        each_head(scores, t0 + 1, 1)
        each_head(update, t0, 0)
        each_head(scores, t0 + 2, 0)
        each_head(update, t0 + 1, 1)

    def quad(k, carry):
        pair(4 * k)
        pair(4 * k + 2)
        return carry

    each_head(scores, qi, 1)
    each_head(scores, 0, 0)
    each_head(update, qi, 1, diagonal=True)
    lax.fori_loop(0, qi // 4, quad, 0)

    @pl.when((qi // 2) % 2 == 1)
    def _():
        pair(4 * (qi // 4))

    @pl.when(qi % 2 == 1)
    def _():
        each_head(update, qi - 1, 0)

    for hh, cols in enumerate(hcols):
        o_ref[:, cols] = (acc_ref[hh] / l_ref[hh]).T.astype(o_ref.dtype)


def _fox_attention(proj, kbias, gq, gk, batch, seq, n_heads, blk=512, heads=2):
    blk = _tile(seq, blk)
    heads = _tile(n_heads, heads)
    nq = seq // blk
    hblk = n_heads // heads
    w = heads * HEAD_DIM
    gain = pl.BlockSpec((1, HEAD_DIM), lambda b, h, i: (0, 0))
    return pl.pallas_call(
        functools.partial(_attn_kernel, blk=blk, heads=heads),
        grid=(batch, hblk, nq),
        in_specs=[pl.BlockSpec((blk, w), lambda b, h, i: (b * nq + i, h)),
                  pl.BlockSpec((seq, w), lambda b, h, i: (b, hblk + h)),
                  pl.BlockSpec((seq, w), lambda b, h, i: (b, h)),
                  pl.BlockSpec((seq, w), lambda b, h, i: (b, 2 * hblk + h)), gain, gain],
        out_specs=pl.BlockSpec((blk, w), lambda b, h, i: (b * nq + i, h)),
        out_shape=jax.ShapeDtypeStruct((batch * seq, n_heads * HEAD_DIM), BF16),
        scratch_shapes=[pltpu.VMEM((heads, seq, 2 * HEAD_DIM), BF16), pltpu.VMEM((heads, 2, blk, blk), F32),
                        pltpu.VMEM((heads, 1, blk), F32), pltpu.VMEM((heads, 1, blk), F32),
                        pltpu.VMEM((heads, HEAD_DIM, blk), F32)],
        compiler_params=_params("parallel", "parallel", "arbitrary"),
        name="fox_attention",
    )(proj, proj, kbias, proj, gq, gk)


def _hgrn_kernel(q_ref, z_ref, v_ref, gate_ref, lbl_ref, gn_ref, o_ref, st_ref, rb_ref, *, layer, heads, chunks):
    C, c = HGRN_CHUNK, HGRN_SUB
    t = pl.program_id(2)

    @pl.when(t == 0)
    def _():
        st_ref[...] = jnp.zeros_like(st_ref)

    logits = lbl_ref[...]
    e = jnp.exp(logits - jnp.max(logits, axis=0, keepdims=True))
    lb_all = jnp.sum(e[:layer + 1], axis=0, keepdims=True) / jnp.sum(e, axis=0, keepdims=True)

    row = lax.broadcasted_iota(jnp.int32, (C, C), 0)
    col = lax.broadcasted_iota(jnp.int32, (C, C), 1)
    ltri = jnp.where(col <= row, 1.0, 0.0).astype(BF16)
    sizes = [c << l for l in range(1, (C // c).bit_length())]
    xor = row ^ col
    level = jnp.zeros((C, C), jnp.int32)
    for l, m in enumerate(sizes):
        level = jnp.where(xor >= m // 2, l + 1, level)
    level = jnp.where(col > row, -1, level)
    lane = lax.broadcasted_iota(jnp.int32, (c, C), 1)

    def chunk_body(ci, carry):
        r0 = pl.multiple_of(ci * C, C)
        for hh in range(heads):
            cols = slice(hh * HEAD_DIM, (hh + 1) * HEAD_DIM)
            lb = lb_all[:, cols]
            q = q_ref[pl.ds(r0, C), cols].astype(F32)
            z = z_ref[pl.ds(r0, C), cols].astype(F32)
            v = v_ref[pl.ds(r0, C), cols]
            gate = gate_ref[pl.ds(r0, C), cols].astype(F32)

            ez = jnp.exp(-jnp.abs(z))
            inv = 1.0 / (1.0 + ez)
            sig_z = jnp.where(z >= 0, 1.0, ez) * inv
            sig_mz = jnp.where(z >= 0, ez, 1.0) * inv
            kk = (1.0 - lb) * sig_mz
            qs = q * _sigmoid(q)
            log2f = jnp.concatenate(_split3(jnp.log2(lb + (1.0 - lb) * sig_z)), axis=1).astype(BF16)
            cs = jnp.dot(ltri, log2f, preferred_element_type=F32)
            b = cs[:, :HEAD_DIM] + cs[:, HEAD_DIM:2 * HEAD_DIM] + cs[:, 2 * HEAD_DIM:]
            b_last = b[C - 1:C, :]

            st = st_ref[hh]
            o = lax.dot_general((qs * jnp.exp2(b)).astype(BF16), st.astype(BF16), NT_DIMS,
                                preferred_element_type=F32)

            rb_ref[2 * hh] = kk
            rb_ref[2 * hh + 1] = b
            blocks = []
            for i in range(C // c):
                rows = slice(i * c, (i + 1) * c)
                q_i, b_i = qs[rows], b[rows]
                blk = jnp.zeros((c, C), F32)
                for s in range(i * c, (i + 1) * c):
                    w = q_i * rb_ref[2 * hh, s:s + 1, :] * jnp.exp2(b_i - rb_ref[2 * hh + 1, s:s + 1, :])
                    blk = jnp.where(lane == s, jnp.sum(w, axis=-1, keepdims=True), blk)
                blocks.append(blk)
            a = jnp.where(level == 0, jnp.concatenate(blocks, axis=0), 0.0)
            for l, m in enumerate(sizes):
                edge = jnp.concatenate(
                    [jnp.broadcast_to(b[j * m + m // 2 - 1:j * m + m // 2, :], (m, HEAD_DIM)) for j in range(C // m)],
                    axis=0)
                x = jnp.exp2(-jnp.abs(b - edge))
                pair = lax.dot_general((qs * x).astype(BF16), (kk * x).astype(BF16), NT_DIMS,
                                       preferred_element_type=F32)
                a = jnp.where(level == l + 1, pair, a)
            o = o + jnp.dot(a.astype(BF16), v, preferred_element_type=F32)

            k_d = (kk * jnp.exp2(b_last - b)).astype(BF16)
            st_ref[hh] = st * jnp.exp2(b_last) + lax.dot_general(v, k_d, TN_DIMS, preferred_element_type=F32)

            ms = jnp.mean(o * o, axis=-1, keepdims=True)
            on = o * lax.rsqrt(ms + RMS_EPS) * gn_ref[...]
            o_ref[pl.ds(r0, C), cols] = (on * (gate * _sigmoid(gate))).astype(o_ref.dtype)
        return carry

    lax.fori_loop(0, chunks, chunk_body, 0)


def _hgrn2(proj, lb_logits, gn, layer, batch, seq, n_heads, col0, heads=4, rows=512):
    heads = _tile(n_heads, heads)
    rows = _tile(seq, rows)
    assert rows % HGRN_CHUNK == 0 and col0 % heads == 0
    nt = seq // rows
    w = heads * HEAD_DIM
    hblk = n_heads // heads

    def in_spec(group):
        return pl.BlockSpec((rows, w), lambda b, h, t: (b * nt + t, col0 // heads + group * hblk + h))

    return pl.pallas_call(
        functools.partial(_hgrn_kernel, layer=layer, heads=heads, chunks=rows // HGRN_CHUNK),
        grid=(batch, hblk, nt),
        in_specs=[in_spec(0), in_spec(1), in_spec(2), in_spec(3),
                  pl.BlockSpec((lb_logits.shape[0], w), lambda b, h, t: (0, h)),
                  pl.BlockSpec((1, HEAD_DIM), lambda b, h, t: (0, 0))],
        out_specs=pl.BlockSpec((rows, w), lambda b, h, t: (b * nt + t, h)),
        out_shape=jax.ShapeDtypeStruct((batch * seq, n_heads * HEAD_DIM), BF16),
        scratch_shapes=[pltpu.VMEM((heads, HEAD_DIM, HEAD_DIM), F32),
                        pltpu.VMEM((2 * heads, HGRN_CHUNK, HEAD_DIM), F32)],
        compiler_params=_params("parallel", "parallel", "arbitrary"),
        name="hgrn2",
    )(proj, proj, proj, proj, lb_logits, gn)


def _ple_embed_kernel(p_ref, w_ref, g_ref, o_ref):
    e = jnp.dot(p_ref[...].astype(BF16), w_ref[...], preferred_element_type=F32)
    ms = jnp.mean(e * e, axis=-1, keepdims=True)
    o_ref[...] = (e * lax.rsqrt(ms + RMS_EPS) * g_ref[...]).astype(o_ref.dtype)


def _ple_embed(p, w, g):
    m, kd = p.shape
    n = w.shape[1]
    tm = _tile(m, 256)
    return pl.pallas_call(
        _ple_embed_kernel,
        grid=(m // tm,),
        in_specs=[pl.BlockSpec((tm, kd), lambda i: (i, 0)), pl.BlockSpec((kd, n), lambda i: (0, 0)),
                  pl.BlockSpec((1, n), lambda i: (0, 0))],
        out_specs=pl.BlockSpec((tm, n), lambda i: (i, 0)),
        out_shape=jax.ShapeDtypeStruct((m, n), BF16),
        compiler_params=_params("parallel"),
        name="ple_embed",
    )(p, w, g.reshape(1, n).astype(F32))


def _ple_gate_kernel(a_ref, w_ref, e_ref, h_ref, o_ref):
    gate = _sigmoid(jnp.dot(a_ref[...], w_ref[...], preferred_element_type=F32))
    o_ref[...] = h_ref[...] + gate * e_ref[...]


def _ple_gate(a, w, e, h, tm=1024, tn=512):
    m, kd = a.shape
    n = w.shape[1]
    tm, tn = _tile(m, tm), _tile(n, tn)
    return pl.pallas_call(
        _ple_gate_kernel,
        grid=(m // tm, n // tn),
        in_specs=[pl.BlockSpec((tm, kd), lambda i, j: (i, 0)), pl.BlockSpec((kd, tn), lambda i, j: (0, j)),
                  pl.BlockSpec((tm, tn), lambda i, j: (i, j)), pl.BlockSpec((tm, tn), lambda i, j: (i, j))],
        out_specs=pl.BlockSpec((tm, tn), lambda i, j: (i, j)),
        out_shape=jax.ShapeDtypeStruct((m, n), F32),
        compiler_params=_params("parallel", "arbitrary"),
        name="ple_gate",
    )(a, w, e, h)


def _layer(h, p, layer, batch, seq, norm_mix_g, w_in, fox_f_bias, fox_q_norm_g, fox_k_norm_g, hgrn_lb_logits,
           hgrn_norm_g, w_out, norm_mlp_g, w_up, w_down, ple_norm_g, w_ple_gate, w_ple_proj, ple_post_g):
    d = h.shape[1]
    d_a = d_b = d // 2
    n_a, n_b = d_a // HEAD_DIM, d_b // HEAD_DIM
    assert n_a <= HEAD_DIM and w_in.shape[1] == 3 * d_a + n_a + 4 * d_b

    w_in_t = jnp.swapaxes(w_in, 0, 1)
    w_fox = _cast_rows(w_in_t, 0, 3 * d_a)
    w_fa = _cast_rows(w_in_t, 3 * d_a, HEAD_DIM)
    w_hgrn = _cast_rows(w_in_t, 3 * d_a + n_a, 4 * d_b)
    q_gain = (fox_q_norm_g.astype(F32) * (LOG2E / math.sqrt(HEAD_DIM))).reshape(1, HEAD_DIM)
    k_gain = fox_k_norm_g.astype(F32).reshape(1, HEAD_DIM)

    u = _rmsnorm(h, norm_mix_g)
    proj_a, f_logit = _inproj(u, w_fox, w_fa)
    proj_b, w_up_b, w_out_b = _matmul(u, w_hgrn, BF16, w_transposed=True, side=(w_up, w_out), name="inproj_hgrn")

    f_bias = jnp.pad(fox_f_bias.astype(F32), (0, HEAD_DIM - n_a)).reshape(1, HEAD_DIM)
    kbias = _forget_bias_tiles(f_logit, f_bias, batch, seq, n_a)
    out_a = _fox_attention(proj_a, kbias, q_gain, k_gain, batch, seq, n_a)

    out_b = _hgrn2(proj_b, hgrn_lb_logits.astype(F32), hgrn_norm_g.astype(F32).reshape(1, HEAD_DIM), layer,
                   batch, seq, n_b, col0=0)

    h, hg, part = _outproj(out_a, out_b, w_out_b, h, norm_mlp_g)
    hid, w_down_b, w_gate_b = _matmul(hg, w_up_b, BF16, act="relu2", part=part, side=(w_down, w_ple_gate),
                                      vmem_limit=V7X_VMEM_LIMIT_HIGH, name="mlp_up")
    h = _matmul_residual(hid, w_down_b, h)

    e = _ple_embed(p, w_ple_proj.astype(BF16), ple_post_g)
    return _ple_gate(_rmsnorm(h, ple_norm_g), w_gate_b, e, h)


def kernel(x, p, norm_mix_g, w_in, fox_f_bias, fox_q_norm_g, fox_k_norm_g, hgrn_lb_logits, hgrn_norm_g, w_out,
           norm_mlp_g, w_up, w_down, ple_norm_g, w_ple_gate, w_ple_proj, ple_post_g):
    batch, seq, d = x.shape
    h = x.reshape(batch * seq, d)
    for i in range(w_in.shape[0]):
        h = _layer(h, p[i].reshape(batch * seq, -1), i, batch, seq, norm_mix_g[i], w_in[i], fox_f_bias[i],
                   fox_q_norm_g[i], fox_k_norm_g[i], hgrn_lb_logits, hgrn_norm_g[i], w_out[i], norm_mlp_g[i],
                   w_up[i], w_down[i], ple_norm_g[i], w_ple_gate[i], w_ple_proj[i], ple_post_g[i])
    return h.reshape(batch, seq, d)
```

```python
import functools
import math

import jax
import jax.numpy as jnp
from jax import lax
from jax.experimental import pallas as pl
from jax.experimental.pallas import tpu as pltpu

F32, BF16 = jnp.float32, jnp.bfloat16
HEAD_DIM = 128
RMS_EPS = 1e-6
LOG2E = math.log2(math.e)
HGRN_CHUNK = 128
HGRN_SUB = 8
V7X_VMEM_LIMIT = 56 * 1024 * 1024
V7X_VMEM_LIMIT_HIGH = 60 * 1024 * 1024

NT_DIMS = (((1,), (1,)), ((), ()))
TN_DIMS = (((0,), (0,)), ((), ()))


def _params(*semantics, vmem_limit=V7X_VMEM_LIMIT):
    return pltpu.CompilerParams(dimension_semantics=semantics, vmem_limit_bytes=vmem_limit)


def _tile(dim, pref):
    t = min(dim, pref)
    assert dim % t == 0, (dim, pref)
    return t


def _sigmoid(x):
    return 1.0 / (1.0 + jnp.exp(-x))


def _rmsnorm_kernel(x_ref, g_ref, o_ref):
    x = x_ref[...]
    ms = jnp.mean(x * x, axis=-1, keepdims=True)
    o_ref[...] = (x * lax.rsqrt(ms + RMS_EPS) * g_ref[...]).astype(o_ref.dtype)


def _rmsnorm(x, g, out_dtype=BF16):
    m, d = x.shape
    tm = _tile(m, 256)
    return pl.pallas_call(
        _rmsnorm_kernel,
        grid=(m // tm,),
        in_specs=[pl.BlockSpec((tm, d), lambda i: (i, 0)), pl.BlockSpec((1, d), lambda i: (0, 0))],
        out_specs=pl.BlockSpec((tm, d), lambda i: (i, 0)),
        out_shape=jax.ShapeDtypeStruct((m, d), out_dtype),
        compiler_params=_params("parallel"),
        name="rmsnorm",
    )(x, g.reshape(1, d).astype(F32))


def _row_scale(part_ref, d):
    return lax.rsqrt(jnp.sum(part_ref[...], axis=-1, keepdims=True) * (1.0 / d) + RMS_EPS)


def _fold_sq(x):
    sq = x * x
    out = sq[:, :HEAD_DIM]
    for t in range(1, x.shape[1] // HEAD_DIM):
        out = out + sq[:, t * HEAD_DIM:(t + 1) * HEAD_DIM]
    return out


def _mm_kernel(*refs, act, w_dims, scaled, n_side):
    a_ref, w_ref = refs[:2]
    side_in = refs[2 + scaled:2 + scaled + n_side]
    o_ref = refs[2 + scaled + n_side]
    side_out = refs[3 + scaled + n_side:3 + scaled + 2 * n_side]
    if scaled:
        part_ref, rs_ref = refs[2], refs[-1]

        @pl.when(pl.program_id(1) == 0)
        def _():
            rs_ref[...] = _row_scale(part_ref, a_ref.shape[1])

    for src, dst in zip(side_in, side_out):
        dst[...] = src[...].astype(dst.dtype)
    acc = lax.dot_general(a_ref[...], w_ref[...], w_dims, preferred_element_type=F32)
    if scaled:
        acc = acc * rs_ref[...]
    if act == "relu2":
        acc = jnp.square(jnp.maximum(acc, 0.0))
    o_ref[...] = acc.astype(o_ref.dtype)


def _matmul(a, w, out_dtype, act=None, part=None, w_transposed=False, side=(), tm=1024, tn=1024,
            vmem_limit=V7X_VMEM_LIMIT, name="matmul"):
    m, k = a.shape
    n = w.shape[0] if w_transposed else w.shape[1]
    tm, tn = _tile(m, tm), _tile(n, tn)
    ni, nj = m // tm, n // tn
    w_spec = pl.BlockSpec((tn, k), lambda i, j: (j, 0)) if w_transposed else pl.BlockSpec((k, tn), lambda i, j: (0, j))
    in_specs, operands = [pl.BlockSpec((tm, k), lambda i, j: (i, 0)), w_spec], [a, w]
    scratch = []
    if part is not None:
        in_specs.append(pl.BlockSpec((tm, part.shape[1]), lambda i, j: (i, 0)))
        operands.append(part)
        scratch.append(pltpu.VMEM((tm, 1), F32))
    out_specs = [pl.BlockSpec((tm, tn), lambda i, j: (i, j))]
    out_shape = [jax.ShapeDtypeStruct((m, n), out_dtype)]
    for sw in side:
        slab = pl.BlockSpec((sw.shape[0] // (ni * nj), sw.shape[1]), lambda i, j: (i * nj + j, 0))
        assert sw.shape[0] % (ni * nj) == 0
        in_specs.append(slab)
        operands.append(sw)
        out_specs.append(slab)
        out_shape.append(jax.ShapeDtypeStruct(sw.shape, BF16))
    w_dims = NT_DIMS if w_transposed else (((1,), (0,)), ((), ()))
    outs = pl.pallas_call(
        functools.partial(_mm_kernel, act=act, w_dims=w_dims, scaled=part is not None, n_side=len(side)),
        grid=(ni, nj),
        in_specs=in_specs,
        out_specs=out_specs,
        out_shape=out_shape,
        scratch_shapes=scratch,
        compiler_params=_params("parallel", "arbitrary", vmem_limit=vmem_limit),
        name=name,
    )(*operands)
    return outs if side else outs[0]


def _cast_rows_kernel(w_ref, o_ref):
    o_ref[...] = w_ref[...].astype(o_ref.dtype)


def _cast_rows(w, row0, nrows, tr=512):
    k = w.shape[1]
    tr = math.gcd(nrows, tr)
    align = math.gcd(math.gcd(row0, tr), 64)
    return pl.pallas_call(
        _cast_rows_kernel,
        grid=(nrows // tr,),
        in_specs=[pl.BlockSpec((pl.Element(tr), pl.Element(k)),
                               lambda i: (pl.multiple_of(row0 + i * tr, align), 0))],
        out_specs=pl.BlockSpec((tr, k), lambda i: (i, 0)),
        out_shape=jax.ShapeDtypeStruct((nrows, k), BF16),
        compiler_params=_params("parallel"),
        name="cast_weight_rows",
    )(w)


def _inproj_kernel(u_ref, wt_ref, wft_ref, o_ref, f_ref):
    @pl.when(pl.program_id(1) == 0)
    def _():
        f_ref[...] = lax.dot_general(u_ref[...], wft_ref[...], NT_DIMS, preferred_element_type=F32)

    o_ref[...] = lax.dot_general(u_ref[...], wt_ref[...], NT_DIMS, preferred_element_type=F32).astype(o_ref.dtype)


def _inproj(u, wt, wft, tm=1024, tn=1024):
    m, k = u.shape
    n, nf = wt.shape[0], wft.shape[0]
    tm, tn = _tile(m, tm), _tile(n, tn)
    return pl.pallas_call(
        _inproj_kernel,
        grid=(m // tm, n // tn),
        in_specs=[pl.BlockSpec((tm, k), lambda i, j: (i, 0)), pl.BlockSpec((tn, k), lambda i, j: (j, 0)),
                  pl.BlockSpec((nf, k), lambda i, j: (0, 0))],
        out_specs=[pl.BlockSpec((tm, tn), lambda i, j: (i, j)), pl.BlockSpec((tm, nf), lambda i, j: (i, 0))],
        out_shape=[jax.ShapeDtypeStruct((m, n), BF16), jax.ShapeDtypeStruct((m, nf), F32)],
        compiler_params=_params("parallel", "arbitrary"),
        name="inproj_fox",
    )(u, wt, wft)


def _outproj_kernel(a_ref, b_ref, wa_ref, wb_ref, x_ref, g_ref, o_ref, ob_ref, part_ref):
    acc = jnp.dot(a_ref[...], wa_ref[...], preferred_element_type=F32)
    acc = acc + jnp.dot(b_ref[...], wb_ref[...], preferred_element_type=F32)
    h = x_ref[...] + acc
    o_ref[...] = h
    ob_ref[...] = (h * g_ref[...]).astype(ob_ref.dtype)
    part_ref[...] = _fold_sq(h)


def _outproj(a, b, w, x, g, tm=1024, tn=512):
    m, ka = a.shape
    kb = b.shape[1]
    n = w.shape[1]
    assert ka == kb and w.shape[0] == ka + kb
    tm, tn = _tile(m, tm), _tile(n, tn)
    tile = pl.BlockSpec((tm, tn), lambda i, j: (i, j))
    return pl.pallas_call(
        _outproj_kernel,
        grid=(m // tm, n // tn),
        in_specs=[pl.BlockSpec((tm, ka), lambda i, j: (i, 0)), pl.BlockSpec((tm, kb), lambda i, j: (i, 0)),
                  pl.BlockSpec((ka, tn), lambda i, j: (0, j)), pl.BlockSpec((kb, tn), lambda i, j: (1, j)), tile,
                  pl.BlockSpec((1, tn), lambda i, j: (0, j))],
        out_specs=[tile, tile, pl.BlockSpec((tm, HEAD_DIM), lambda i, j: (i, j))],
        out_shape=[jax.ShapeDtypeStruct((m, n), F32), jax.ShapeDtypeStruct((m, n), BF16),
                   jax.ShapeDtypeStruct((m, n // tn * HEAD_DIM), F32)],
        compiler_params=_params("parallel", "arbitrary"),
        name="outproj",
    )(a, b, w, w, x, g.reshape(1, n).astype(F32))


def _mm_res_kernel(a_ref, w_ref, r_ref, o_ref):
    @pl.when(pl.program_id(2) == 0)
    def _():
        o_ref[...] = r_ref[...]

    o_ref[...] += jnp.dot(a_ref[...], w_ref[...], preferred_element_type=F32)


def _matmul_residual(a, w, r, tm=1024, tn=1024, tk=4096):
    m, kd = a.shape
    n = w.shape[1]
    tm, tn, tk = _tile(m, tm), _tile(n, tn), _tile(kd, tk)
    return pl.pallas_call(
        _mm_res_kernel,
        grid=(m // tm, n // tn, kd // tk),
        in_specs=[pl.BlockSpec((tm, tk), lambda i, j, k: (i, k)), pl.BlockSpec((tk, tn), lambda i, j, k: (k, j)),
                  pl.BlockSpec((tm, tn), lambda i, j, k: (i, j))],
        out_specs=pl.BlockSpec((tm, tn), lambda i, j, k: (i, j)),
        out_shape=jax.ShapeDtypeStruct((m, n), F32),
        compiler_params=_params("parallel", "parallel", "arbitrary", vmem_limit=V7X_VMEM_LIMIT_HIGH),
        name="mlp_down",
    )(a, w, r)


def _split3(x):
    hi = x.astype(BF16).astype(F32)
    r = x - hi
    mid = r.astype(BF16).astype(F32)
    return hi, mid, r - mid


def _fcum_kernel(f_ref, bias_ref, o_ref, carry_ref, *, n_heads):
    t = pl.program_id(1)

    @pl.when(t == 0)
    def _():
        carry_ref[...] = jnp.zeros_like(carry_ref)

    x = f_ref[...] + bias_ref[...]
    ls = jnp.minimum(x, 0.0) - jnp.log1p(jnp.exp(-jnp.abs(x)))
    tc = x.shape[0]
    r = lax.broadcasted_iota(jnp.int32, (tc, tc), 0)
    c = lax.broadcasted_iota(jnp.int32, (tc, tc), 1)
    tri = jnp.where(c <= r, 1.0, 0.0).astype(BF16)
    c3 = jnp.dot(tri, jnp.concatenate(_split3(ls), axis=1).astype(BF16), preferred_element_type=F32)
    cs = c3[:, :HEAD_DIM] + c3[:, HEAD_DIM:2 * HEAD_DIM] + c3[:, 2 * HEAD_DIM:] + carry_ref[...]
    carry_ref[...] = cs[tc - 1:tc, :]
    lane = lax.broadcasted_iota(jnp.int32, (tc, HEAD_DIM), 1)
    for h in range(n_heads):
        hi, mid, lo = _split3(jnp.broadcast_to(cs[:, h:h + 1] * (-LOG2E), (tc, HEAD_DIM)))
        tile = jnp.where(lane == 0, hi, jnp.where(lane == 1, mid, jnp.where(lane == 2, lo, 0.0)))
        o_ref[:, h * HEAD_DIM:(h + 1) * HEAD_DIM] = tile.astype(o_ref.dtype)


def _forget_bias_tiles(f_logit, bias, batch, seq, n_heads):
    tc = _tile(seq, 256)
    nt = seq // tc
    return pl.pallas_call(
        functools.partial(_fcum_kernel, n_heads=n_heads),
        grid=(batch, nt),
        in_specs=[pl.BlockSpec((tc, HEAD_DIM), lambda b, t: (b * nt + t, 0)),
                  pl.BlockSpec((1, HEAD_DIM), lambda b, t: (0, 0))],
        out_specs=pl.BlockSpec((tc, n_heads * HEAD_DIM), lambda b, t: (b * nt + t, 0)),
        out_shape=jax.ShapeDtypeStruct((batch * seq, n_heads * HEAD_DIM), BF16),
        scratch_shapes=[pltpu.VMEM((1, HEAD_DIM), F32)],
        compiler_params=_params("parallel", "arbitrary"),
        name="fox_forget_cumsum",
    )(f_logit, bias)


def _head_rmsnorm(x, g):
    x = x.astype(F32)
    ms = jnp.mean(x * x, axis=-1, keepdims=True)
    return (x * lax.rsqrt(ms + RMS_EPS) * g).astype(BF16)


def _attn_kernel(q_ref, k_ref, kb_ref, v_ref, gq_ref, gk_ref, o_ref, kp_ref, s_ref, m_ref, l_ref, acc_ref, *,
                 blk, heads):
    qi = pl.program_id(2)
    hcols = [slice(hh * HEAD_DIM, (hh + 1) * HEAD_DIM) for hh in range(heads)]
    seq = k_ref.shape[0]

    @pl.when(qi == 0)
    def _():
        def fill(c, carry):
            rows = pl.ds(pl.multiple_of(c * blk, blk), blk)
            for hh, cols in enumerate(hcols):
                kp_ref[hh, rows, :HEAD_DIM] = _head_rmsnorm(k_ref[rows, cols], gk_ref[...])
                kp_ref[hh, rows, HEAD_DIM:] = kb_ref[rows, cols]
            return carry
        lax.fori_loop(0, seq // blk, fill, 0)

    lane = lax.broadcasted_iota(jnp.int32, (blk, HEAD_DIM), 1)
    q_ones = jnp.where(lane < 3, 1.0, 0.0).astype(BF16)
    key_pos = lax.broadcasted_iota(jnp.int32, (blk, blk), 0)
    qry_pos = lax.broadcasted_iota(jnp.int32, (blk, blk), 1)

    q2 = [jnp.concatenate([_head_rmsnorm(q_ref[:, cols], gq_ref[...]), q_ones], axis=1) for cols in hcols]

    def scores(hh, j, slot):
        rows = pl.ds(pl.multiple_of(j * blk, blk), blk)
        s_ref[hh, slot] = lax.dot_general(kp_ref[hh, rows, :], q2[hh], NT_DIMS, preferred_element_type=F32)

    def update(hh, j, slot, diagonal=False):
        rows = pl.ds(pl.multiple_of(j * blk, blk), blk)
        s = s_ref[hh, slot]
        if diagonal:
            s = jnp.where(key_pos <= qry_pos, s, -jnp.inf)
            m_new = jnp.max(s, axis=0, keepdims=True)
        else:
            m = m_ref[hh]
            m_new = jnp.maximum(m, jnp.max(s, axis=0, keepdims=True))
        p = jnp.exp2(s - m_new)
        pv = lax.dot_general(v_ref[rows, hcols[hh]], p.astype(BF16), TN_DIMS, preferred_element_type=F32)
        m_ref[hh] = m_new
        if diagonal:
            l_ref[hh] = jnp.sum(p, axis=0, keepdims=True)
            acc_ref[hh] = pv
        else:
            alpha = jnp.exp2(m - m_new)
            l_ref[hh] = alpha * l_ref[hh] + jnp.sum(p, axis=0, keepdims=True)
            acc_ref[hh] = alpha * acc_ref[hh] + pv

    def each_head(fn, *args, **kwargs):
        for hh in range(heads):
            fn(hh, *args, **kwargs)

    def pair(t0):
        each_head(scores, t0 + 1, 1)
        each_head(update, t0, 0)
        each_head(scores, t0 + 2, 0)
        each_head(update, t0 + 1, 1)

    def quad(k, carry):
        pair(4 * k)
        pair(4 * k + 2)
        return carry

    each_head(scores, qi, 1)
    each_head(scores, 0, 0)
    each_head(update, qi, 1, diagonal=True)
    lax.fori_loop(0, qi // 4, quad, 0)

    @pl.when((qi // 2) % 2 == 1)
    def _():
        pair(4 * (qi // 4))

    @pl.when(qi % 2 == 1)
    def _():
        each_head(update, qi - 1, 0)

    for hh, cols in enumerate(hcols):
        o_ref[:, cols] = (acc_ref[hh] / l_ref[hh]).T.astype(o_ref.dtype)


def _fox_attention(proj, kbias, gq, gk, batch, seq, n_heads, blk=512, heads=2):
    blk = _tile(seq, blk)
    heads = _tile(n_heads, heads)
    nq = seq // blk
    hblk = n_heads // heads
    w = heads * HEAD_DIM
    gain = pl.BlockSpec((1, HEAD_DIM), lambda b, h, i: (0, 0))
    return pl.pallas_call(
        functools.partial(_attn_kernel, blk=blk, heads=heads),
        grid=(batch, hblk, nq),
        in_specs=[pl.BlockSpec((blk, w), lambda b, h, i: (b * nq + i, h)),
                  pl.BlockSpec((seq, w), lambda b, h, i: (b, hblk + h)),
                  pl.BlockSpec((seq, w), lambda b, h, i: (b, h)),
                  pl.BlockSpec((seq, w), lambda b, h, i: (b, 2 * hblk + h)), gain, gain],
        out_specs=pl.BlockSpec((blk, w), lambda b, h, i: (b * nq + i, h)),
        out_shape=jax.ShapeDtypeStruct((batch * seq, n_heads * HEAD_DIM), BF16),
        scratch_shapes=[pltpu.VMEM((heads, seq, 2 * HEAD_DIM), BF16), pltpu.VMEM((heads, 2, blk, blk), F32),
                        pltpu.VMEM((heads, 1, blk), F32), pltpu.VMEM((heads, 1, blk), F32),
                        pltpu.VMEM((heads, HEAD_DIM, blk), F32)],
        compiler_params=_params("parallel", "parallel", "arbitrary"),
        name="fox_attention",
    )(proj, proj, kbias, proj, gq, gk)


def _hgrn_kernel(q_ref, z_ref, v_ref, gate_ref, lbl_ref, gn_ref, o_ref, st_ref, rb_ref, *, layer, heads, chunks):
    C, c = HGRN_CHUNK, HGRN_SUB
    t = pl.program_id(2)

    @pl.when(t == 0)
    def _():
        st_ref[...] = jnp.zeros_like(st_ref)

    logits = lbl_ref[...]
    e = jnp.exp(logits - jnp.max(logits, axis=0, keepdims=True))
    lb_all = jnp.sum(e[:layer + 1], axis=0, keepdims=True) / jnp.sum(e, axis=0, keepdims=True)

    row = lax.broadcasted_iota(jnp.int32, (C, C), 0)
    col = lax.broadcasted_iota(jnp.int32, (C, C), 1)
    ltri = jnp.where(col <= row, 1.0, 0.0).astype(BF16)
    sizes = [c << l for l in range(1, (C // c).bit_length())]
    xor = row ^ col
    level = jnp.zeros((C, C), jnp.int32)
    for l, m in enumerate(sizes):
        level = jnp.where(xor >= m // 2, l + 1, level)
    level = jnp.where(col > row, -1, level)
    lane = lax.broadcasted_iota(jnp.int32, (c, C), 1)

    def chunk_body(ci, carry):
        r0 = pl.multiple_of(ci * C, C)
        hs = range(heads)
        cols = [slice(hh * HEAD_DIM, (hh + 1) * HEAD_DIM) for hh in hs]
        qs, kk, b, v, o, a = {}, {}, {}, {}, {}, {}
        for hh in hs:
            lb = lb_all[:, cols[hh]]
            q = q_ref[pl.ds(r0, C), cols[hh]].astype(F32)
            z = z_ref[pl.ds(r0, C), cols[hh]].astype(F32)
            v[hh] = v_ref[pl.ds(r0, C), cols[hh]]
            ez = jnp.exp(-jnp.abs(z))
            inv = 1.0 / (1.0 + ez)
            sig_z = jnp.where(z >= 0, 1.0, ez) * inv
            sig_mz = jnp.where(z >= 0, ez, 1.0) * inv
            kk[hh] = (1.0 - lb) * sig_mz
            qs[hh] = q * _sigmoid(q)
            log2f = jnp.concatenate(_split3(jnp.log2(lb + (1.0 - lb) * sig_z)), axis=1).astype(BF16)
            cs = jnp.dot(ltri, log2f, preferred_element_type=F32)
            b[hh] = cs[:, :HEAD_DIM] + cs[:, HEAD_DIM:2 * HEAD_DIM] + cs[:, 2 * HEAD_DIM:]
            rb_ref[2 * hh] = kk[hh]
            rb_ref[2 * hh + 1] = b[hh]

        for hh in hs:
            o[hh] = lax.dot_general((qs[hh] * jnp.exp2(b[hh])).astype(BF16), st_ref[hh].astype(BF16), NT_DIMS,
                                    preferred_element_type=F32)

        blocks = {hh: [] for hh in hs}
        for i in range(C // c):
            rows = slice(i * c, (i + 1) * c)
            blk = {hh: jnp.zeros((c, C), F32) for hh in hs}
            for s in range(i * c, (i + 1) * c):
                at_s = lane == s
                for hh in hs:
                    w = qs[hh][rows] * rb_ref[2 * hh, s:s + 1, :] * jnp.exp2(b[hh][rows] - rb_ref[2 * hh + 1, s:s + 1, :])
                    blk[hh] = jnp.where(at_s, jnp.sum(w, axis=-1, keepdims=True), blk[hh])
            for hh in hs:
                blocks[hh].append(blk[hh])
        in_sub = level == 0
        for hh in hs:
            a[hh] = jnp.where(in_sub, jnp.concatenate(blocks[hh], axis=0), 0.0)

        for l, m in enumerate(sizes):
            at_level = level == l + 1
            for hh in hs:
                bh = b[hh]
                edge = jnp.concatenate(
                    [jnp.broadcast_to(bh[j * m + m // 2 - 1:j * m + m // 2, :], (m, HEAD_DIM)) for j in range(C // m)],
                    axis=0)
                x = jnp.exp2(-jnp.abs(bh - edge))
                pair = lax.dot_general((qs[hh] * x).astype(BF16), (kk[hh] * x).astype(BF16), NT_DIMS,
                                       preferred_element_type=F32)
                a[hh] = jnp.where(at_level, pair, a[hh])

        for hh in hs:
            o[hh] = o[hh] + jnp.dot(a[hh].astype(BF16), v[hh], preferred_element_type=F32)
            b_last = b[hh][C - 1:C, :]
            k_d = (kk[hh] * jnp.exp2(b_last - b[hh])).astype(BF16)
            st_ref[hh] = st_ref[hh] * jnp.exp2(b_last) + lax.dot_general(v[hh], k_d, TN_DIMS,
                                                                         preferred_element_type=F32)

        for hh in hs:
            gate = gate_ref[pl.ds(r0, C), cols[hh]].astype(F32)
            ms = jnp.mean(o[hh] * o[hh], axis=-1, keepdims=True)
            on = o[hh] * lax.rsqrt(ms + RMS_EPS) * gn_ref[...]
            o_ref[pl.ds(r0, C), cols[hh]] = (on * (gate * _sigmoid(gate))).astype(o_ref.dtype)
        return carry

    lax.fori_loop(0, chunks, chunk_body, 0)


def _hgrn2(proj, lb_logits, gn, layer, batch, seq, n_heads, col0, heads=4, rows=512):
    heads = _tile(n_heads, heads)
    rows = _tile(seq, rows)
    assert rows % HGRN_CHUNK == 0 and col0 % heads == 0
    nt = seq // rows
    w = heads * HEAD_DIM
    hblk = n_heads // heads

    def in_spec(group):
        return pl.BlockSpec((rows, w), lambda b, h, t: (b * nt + t, col0 // heads + group * hblk + h))

    return pl.pallas_call(
        functools.partial(_hgrn_kernel, layer=layer, heads=heads, chunks=rows // HGRN_CHUNK),
        grid=(batch, hblk, nt),
        in_specs=[in_spec(0), in_spec(1), in_spec(2), in_spec(3),
                  pl.BlockSpec((lb_logits.shape[0], w), lambda b, h, t: (0, h)),
                  pl.BlockSpec((1, HEAD_DIM), lambda b, h, t: (0, 0))],
        out_specs=pl.BlockSpec((rows, w), lambda b, h, t: (b * nt + t, h)),
        out_shape=jax.ShapeDtypeStruct((batch * seq, n_heads * HEAD_DIM), BF16),
        scratch_shapes=[pltpu.VMEM((heads, HEAD_DIM, HEAD_DIM), F32),
                        pltpu.VMEM((2 * heads, HGRN_CHUNK, HEAD_DIM), F32)],
        compiler_params=_params("parallel", "parallel", "arbitrary"),
        name="hgrn2",
    )(proj, proj, proj, proj, lb_logits, gn)


def _ple_embed_kernel(p_ref, w_ref, g_ref, o_ref):
    e = jnp.dot(p_ref[...].astype(BF16), w_ref[...], preferred_element_type=F32)
    ms = jnp.mean(e * e, axis=-1, keepdims=True)
    o_ref[...] = (e * lax.rsqrt(ms + RMS_EPS) * g_ref[...]).astype(o_ref.dtype)


def _ple_embed(p, w, g):
    m, kd = p.shape
    n = w.shape[1]
    tm = _tile(m, 256)
    return pl.pallas_call(
        _ple_embed_kernel,
        grid=(m // tm,),
        in_specs=[pl.BlockSpec((tm, kd), lambda i: (i, 0)), pl.BlockSpec((kd, n), lambda i: (0, 0)),
                  pl.BlockSpec((1, n), lambda i: (0, 0))],
        out_specs=pl.BlockSpec((tm, n), lambda i: (i, 0)),
        out_shape=jax.ShapeDtypeStruct((m, n), BF16),
        compiler_params=_params("parallel"),
        name="ple_embed",
    )(p, w, g.reshape(1, n).astype(F32))


def _ple_gate_kernel(a_ref, w_ref, e_ref, h_ref, o_ref):
    gate = _sigmoid(jnp.dot(a_ref[...], w_ref[...], preferred_element_type=F32))
    o_ref[...] = h_ref[...] + gate * e_ref[...]


def _ple_gate(a, w, e, h, tm=1024, tn=512):
    m, kd = a.shape
    n = w.shape[1]
    tm, tn = _tile(m, tm), _tile(n, tn)
    return pl.pallas_call(
        _ple_gate_kernel,
        grid=(m // tm, n // tn),
        in_specs=[pl.BlockSpec((tm, kd), lambda i, j: (i, 0)), pl.BlockSpec((kd, tn), lambda i, j: (0, j)),
                  pl.BlockSpec((tm, tn), lambda i, j: (i, j)), pl.BlockSpec((tm, tn), lambda i, j: (i, j))],
        out_specs=pl.BlockSpec((tm, tn), lambda i, j: (i, j)),
        out_shape=jax.ShapeDtypeStruct((m, n), F32),
        compiler_params=_params("parallel", "arbitrary"),
        name="ple_gate",
    )(a, w, e, h)


def _layer(h, p, layer, batch, seq, norm_mix_g, w_in, fox_f_bias, fox_q_norm_g, fox_k_norm_g, hgrn_lb_logits,
           hgrn_norm_g, w_out, norm_mlp_g, w_up, w_down, ple_norm_g, w_ple_gate, w_ple_proj, ple_post_g):
    d = h.shape[1]
    d_a = d_b = d // 2
    n_a, n_b = d_a // HEAD_DIM, d_b // HEAD_DIM
    assert n_a <= HEAD_DIM and w_in.shape[1] == 3 * d_a + n_a + 4 * d_b

    w_in_t = jnp.swapaxes(w_in, 0, 1)
    w_fox = _cast_rows(w_in_t, 0, 3 * d_a)
    w_fa = _cast_rows(w_in_t, 3 * d_a, HEAD_DIM)
    w_hgrn = _cast_rows(w_in_t, 3 * d_a + n_a, 4 * d_b)
    q_gain = (fox_q_norm_g.astype(F32) * (LOG2E / math.sqrt(HEAD_DIM))).reshape(1, HEAD_DIM)
    k_gain = fox_k_norm_g.astype(F32).reshape(1, HEAD_DIM)

    u = _rmsnorm(h, norm_mix_g)
    proj_a, f_logit = _inproj(u, w_fox, w_fa)
    proj_b, w_up_b, w_out_b = _matmul(u, w_hgrn, BF16, w_transposed=True, side=(w_up, w_out), name="inproj_hgrn")

    f_bias = jnp.pad(fox_f_bias.astype(F32), (0, HEAD_DIM - n_a)).reshape(1, HEAD_DIM)
    kbias = _forget_bias_tiles(f_logit, f_bias, batch, seq, n_a)
    out_a = _fox_attention(proj_a, kbias, q_gain, k_gain, batch, seq, n_a)

    out_b = _hgrn2(proj_b, hgrn_lb_logits.astype(F32), hgrn_norm_g.astype(F32).reshape(1, HEAD_DIM), layer,
                   batch, seq, n_b, col0=0)

    h, hg, part = _outproj(out_a, out_b, w_out_b, h, norm_mlp_g)
    hid, w_down_b, w_gate_b = _matmul(hg, w_up_b, BF16, act="relu2", part=part, side=(w_down, w_ple_gate),
                                      vmem_limit=V7X_VMEM_LIMIT_HIGH, name="mlp_up")
    h = _matmul_residual(hid, w_down_b, h)

    e = _ple_embed(p, w_ple_proj.astype(BF16), ple_post_g)
    return _ple_gate(_rmsnorm(h, ple_norm_g), w_gate_b, e, h)


def kernel(x, p, norm_mix_g, w_in, fox_f_bias, fox_q_norm_g, fox_k_norm_g, hgrn_lb_logits, hgrn_norm_g, w_out,
           norm_mlp_g, w_up, w_down, ple_norm_g, w_ple_gate, w_ple_proj, ple_post_g):
    batch, seq, d = x.shape
    h = x.reshape(batch * seq, d)
    for i in range(w_in.shape[0]):
        h = _layer(h, p[i].reshape(batch * seq, -1), i, batch, seq, norm_mix_g[i], w_in[i], fox_f_bias[i],
                   fox_q_norm_g[i], fox_k_norm_g[i], hgrn_lb_logits, hgrn_norm_g[i], w_out[i], norm_mlp_g[i],
                   w_up[i], w_down[i], ple_norm_g[i], w_ple_gate[i], w_ple_proj[i], ple_post_g[i])
    return h.reshape(batch, seq, d)
```

```python
import functools
import math

import jax
import jax.numpy as jnp
from jax import lax
from jax.experimental import pallas as pl
from jax.experimental.pallas import tpu as pltpu

F32, BF16 = jnp.float32, jnp.bfloat16
HEAD_DIM = 128
RMS_EPS = 1e-6
LOG2E = math.log2(math.e)
HGRN_CHUNK = 128
HGRN_SUB = 8
V7X_VMEM_LIMIT = 56 * 1024 * 1024
V7X_VMEM_LIMIT_HIGH = 60 * 1024 * 1024

NT_DIMS = (((1,), (1,)), ((), ()))
TN_DIMS = (((0,), (0,)), ((), ()))


def _params(*semantics, vmem_limit=V7X_VMEM_LIMIT):
    return pltpu.CompilerParams(dimension_semantics=semantics, vmem_limit_bytes=vmem_limit)


def _tile(dim, pref):
    t = min(dim, pref)
    assert dim % t == 0, (dim, pref)
    return t


def _sigmoid(x):
    return 1.0 / (1.0 + jnp.exp(-x))


def _rmsnorm_kernel(x_ref, g_ref, o_ref):
    x = x_ref[...]
    ms = jnp.mean(x * x, axis=-1, keepdims=True)
    o_ref[...] = (x * lax.rsqrt(ms + RMS_EPS) * g_ref[...]).astype(o_ref.dtype)


def _rmsnorm(x, g, out_dtype=BF16):
    m, d = x.shape
    tm = _tile(m, 256)
    return pl.pallas_call(
        _rmsnorm_kernel,
        grid=(m // tm,),
        in_specs=[pl.BlockSpec((tm, d), lambda i: (i, 0)), pl.BlockSpec((1, d), lambda i: (0, 0))],
        out_specs=pl.BlockSpec((tm, d), lambda i: (i, 0)),
        out_shape=jax.ShapeDtypeStruct((m, d), out_dtype),
        compiler_params=_params("parallel"),
        name="rmsnorm",
    )(x, g.reshape(1, d).astype(F32))


def _row_scale(part_ref, d):
    return lax.rsqrt(jnp.sum(part_ref[...], axis=-1, keepdims=True) * (1.0 / d) + RMS_EPS)


def _fold_sq(x):
    sq = x * x
    out = sq[:, :HEAD_DIM]
    for t in range(1, x.shape[1] // HEAD_DIM):
        out = out + sq[:, t * HEAD_DIM:(t + 1) * HEAD_DIM]
    return out


def _mm_kernel(*refs, act, w_dims, scaled, n_side):
    a_ref, w_ref = refs[:2]
    side_in = refs[2 + scaled:2 + scaled + n_side]
    o_ref = refs[2 + scaled + n_side]
    side_out = refs[3 + scaled + n_side:3 + scaled + 2 * n_side]
    if scaled:
        part_ref, rs_ref = refs[2], refs[-1]

        @pl.when(pl.program_id(1) == 0)
        def _():
            rs_ref[...] = _row_scale(part_ref, a_ref.shape[1])

    for src, dst in zip(side_in, side_out):
        dst[...] = src[...].astype(dst.dtype)
    acc = lax.dot_general(a_ref[...], w_ref[...], w_dims, preferred_element_type=F32)
    if scaled:
        acc = acc * rs_ref[...]
    if act == "relu2":
        acc = jnp.square(jnp.maximum(acc, 0.0))
    o_ref[...] = acc.astype(o_ref.dtype)


def _matmul(a, w, out_dtype, act=None, part=None, w_transposed=False, side=(), tm=1024, tn=1024,
            vmem_limit=V7X_VMEM_LIMIT, name="matmul"):
    m, k = a.shape
    n = w.shape[0] if w_transposed else w.shape[1]
    tm, tn = _tile(m, tm), _tile(n, tn)
    ni, nj = m // tm, n // tn
    w_spec = pl.BlockSpec((tn, k), lambda i, j: (j, 0)) if w_transposed else pl.BlockSpec((k, tn), lambda i, j: (0, j))
    in_specs, operands = [pl.BlockSpec((tm, k), lambda i, j: (i, 0)), w_spec], [a, w]
    scratch = []
    if part is not None:
        in_specs.append(pl.BlockSpec((tm, part.shape[1]), lambda i, j: (i, 0)))
        operands.append(part)
        scratch.append(pltpu.VMEM((tm, 1), F32))
    out_specs = [pl.BlockSpec((tm, tn), lambda i, j: (i, j))]
    out_shape = [jax.ShapeDtypeStruct((m, n), out_dtype)]
    for sw in side:
        slab = pl.BlockSpec((sw.shape[0] // (ni * nj), sw.shape[1]), lambda i, j: (i * nj + j, 0))
        assert sw.shape[0] % (ni * nj) == 0
        in_specs.append(slab)
        operands.append(sw)
        out_specs.append(slab)
        out_shape.append(jax.ShapeDtypeStruct(sw.shape, BF16))
    w_dims = NT_DIMS if w_transposed else (((1,), (0,)), ((), ()))
    outs = pl.pallas_call(
        functools.partial(_mm_kernel, act=act, w_dims=w_dims, scaled=part is not None, n_side=len(side)),
        grid=(ni, nj),
        in_specs=in_specs,
        out_specs=out_specs,
        out_shape=out_shape,
        scratch_shapes=scratch,
        compiler_params=_params("parallel", "arbitrary", vmem_limit=vmem_limit),
        name=name,
    )(*operands)
    return outs if side else outs[0]


def _cast_rows_kernel(w_ref, o_ref):
    o_ref[...] = w_ref[...].astype(o_ref.dtype)


def _cast_rows(w, row0, nrows, tr=512):
    k = w.shape[1]
    tr = math.gcd(nrows, tr)
    align = math.gcd(math.gcd(row0, tr), 64)
    return pl.pallas_call(
        _cast_rows_kernel,
        grid=(nrows // tr,),
        in_specs=[pl.BlockSpec((pl.Element(tr), pl.Element(k)),
                               lambda i: (pl.multiple_of(row0 + i * tr, align), 0))],
        out_specs=pl.BlockSpec((tr, k), lambda i: (i, 0)),
        out_shape=jax.ShapeDtypeStruct((nrows, k), BF16),
        compiler_params=_params("parallel"),
        name="cast_weight_rows",
    )(w)


def _inproj_kernel(u_ref, wt_ref, wft_ref, side_ref, o_ref, f_ref, side_out_ref):
    @pl.when(pl.program_id(1) == 0)
    def _():
        f_ref[...] = lax.dot_general(u_ref[...], wft_ref[...], NT_DIMS, preferred_element_type=F32)

    side_out_ref[...] = side_ref[...].astype(side_out_ref.dtype)
    o_ref[...] = lax.dot_general(u_ref[...], wt_ref[...], NT_DIMS, preferred_element_type=F32).astype(o_ref.dtype)


def _inproj(u, wt, wft, side_src, side_row0, side_rows, tm=1024, tn=768):
    m, k = u.shape
    n, nf = wt.shape[0], wft.shape[0]
    tm, tn = _tile(m, tm), math.gcd(n, tn)
    ni, nj = m // tm, n // tn
    slab = side_rows // (ni * nj)
    assert slab * ni * nj == side_rows
    align = math.gcd(math.gcd(side_row0, slab), 64)
    return pl.pallas_call(
        _inproj_kernel,
        grid=(ni, nj),
        in_specs=[pl.BlockSpec((tm, k), lambda i, j: (i, 0)), pl.BlockSpec((tn, k), lambda i, j: (j, 0)),
                  pl.BlockSpec((nf, k), lambda i, j: (0, 0)),
                  pl.BlockSpec((pl.Element(slab), pl.Element(side_src.shape[1])),
                               lambda i, j: (pl.multiple_of(side_row0 + (i * nj + j) * slab, align), 0))],
        out_specs=[pl.BlockSpec((tm, tn), lambda i, j: (i, j)), pl.BlockSpec((tm, nf), lambda i, j: (i, 0)),
                   pl.BlockSpec((slab, side_src.shape[1]), lambda i, j: (i * nj + j, 0))],
        out_shape=[jax.ShapeDtypeStruct((m, n), BF16), jax.ShapeDtypeStruct((m, nf), F32),
                   jax.ShapeDtypeStruct((side_rows, side_src.shape[1]), BF16)],
        compiler_params=_params("parallel", "arbitrary"),
        name="inproj_fox",
    )(u, wt, wft, side_src)


def _outproj_kernel(a_ref, b_ref, wa_ref, wb_ref, x_ref, g_ref, o_ref, ob_ref, part_ref):
    acc = jnp.dot(a_ref[...], wa_ref[...], preferred_element_type=F32)
    acc = acc + jnp.dot(b_ref[...], wb_ref[...], preferred_element_type=F32)
    h = x_ref[...] + acc
    o_ref[...] = h
    ob_ref[...] = (h * g_ref[...]).astype(ob_ref.dtype)
    part_ref[...] = _fold_sq(h)


def _outproj(a, b, w, x, g, tm=1024, tn=512):
    m, ka = a.shape
    kb = b.shape[1]
    n = w.shape[1]
    assert ka == kb and w.shape[0] == ka + kb
    tm, tn = _tile(m, tm), _tile(n, tn)
    tile = pl.BlockSpec((tm, tn), lambda i, j: (i, j))
    return pl.pallas_call(
        _outproj_kernel,
        grid=(m // tm, n // tn),
        in_specs=[pl.BlockSpec((tm, ka), lambda i, j: (i, 0)), pl.BlockSpec((tm, kb), lambda i, j: (i, 0)),
                  pl.BlockSpec((ka, tn), lambda i, j: (0, j)), pl.BlockSpec((kb, tn), lambda i, j: (1, j)), tile,
                  pl.BlockSpec((1, tn), lambda i, j: (0, j))],
        out_specs=[tile, tile, pl.BlockSpec((tm, HEAD_DIM), lambda i, j: (i, j))],
        out_shape=[jax.ShapeDtypeStruct((m, n), F32), jax.ShapeDtypeStruct((m, n), BF16),
                   jax.ShapeDtypeStruct((m, n // tn * HEAD_DIM), F32)],
        compiler_params=_params("parallel", "arbitrary"),
        name="outproj",
    )(a, b, w, w, x, g.reshape(1, n).astype(F32))


def _mm_res_kernel(a_ref, w_ref, r_ref, o_ref):
    @pl.when(pl.program_id(2) == 0)
    def _():
        o_ref[...] = r_ref[...]

    o_ref[...] += jnp.dot(a_ref[...], w_ref[...], preferred_element_type=F32)


def _matmul_residual(a, w, r, tm=1024, tn=1024, tk=4096):
    m, kd = a.shape
    n = w.shape[1]
    tm, tn, tk = _tile(m, tm), _tile(n, tn), _tile(kd, tk)
    return pl.pallas_call(
        _mm_res_kernel,
        grid=(m // tm, n // tn, kd // tk),
        in_specs=[pl.BlockSpec((tm, tk), lambda i, j, k: (i, k)), pl.BlockSpec((tk, tn), lambda i, j, k: (k, j)),
                  pl.BlockSpec((tm, tn), lambda i, j, k: (i, j))],
        out_specs=pl.BlockSpec((tm, tn), lambda i, j, k: (i, j)),
        out_shape=jax.ShapeDtypeStruct((m, n), F32),
        compiler_params=_params("parallel", "parallel", "arbitrary", vmem_limit=V7X_VMEM_LIMIT_HIGH),
        name="mlp_down",
    )(a, w, r)


def _split3(x):
    hi = x.astype(BF16).astype(F32)
    r = x - hi
    mid = r.astype(BF16).astype(F32)
    return hi, mid, r - mid


def _fcum_kernel(f_ref, bias_ref, o_ref, carry_ref, *, n_heads):
    t = pl.program_id(1)

    @pl.when(t == 0)
    def _():
        carry_ref[...] = jnp.zeros_like(carry_ref)

    x = f_ref[...] + bias_ref[...]
    ls = jnp.minimum(x, 0.0) - jnp.log1p(jnp.exp(-jnp.abs(x)))
    tc = x.shape[0]
    r = lax.broadcasted_iota(jnp.int32, (tc, tc), 0)
    c = lax.broadcasted_iota(jnp.int32, (tc, tc), 1)
    tri = jnp.where(c <= r, 1.0, 0.0).astype(BF16)
    c3 = jnp.dot(tri, jnp.concatenate(_split3(ls), axis=1).astype(BF16), preferred_element_type=F32)
    cs = c3[:, :HEAD_DIM] + c3[:, HEAD_DIM:2 * HEAD_DIM] + c3[:, 2 * HEAD_DIM:] + carry_ref[...]
    carry_ref[...] = cs[tc - 1:tc, :]
    lane = lax.broadcasted_iota(jnp.int32, (tc, HEAD_DIM), 1)
    for h in range(n_heads):
        hi, mid, lo = _split3(jnp.broadcast_to(cs[:, h:h + 1] * (-LOG2E), (tc, HEAD_DIM)))
        tile = jnp.where(lane == 0, hi, jnp.where(lane == 1, mid, jnp.where(lane == 2, lo, 0.0)))
        o_ref[:, h * HEAD_DIM:(h + 1) * HEAD_DIM] = tile.astype(o_ref.dtype)


def _forget_bias_tiles(f_logit, bias, batch, seq, n_heads):
    tc = _tile(seq, 256)
    nt = seq // tc
    return pl.pallas_call(
        functools.partial(_fcum_kernel, n_heads=n_heads),
        grid=(batch, nt),
        in_specs=[pl.BlockSpec((tc, HEAD_DIM), lambda b, t: (b * nt + t, 0)),
                  pl.BlockSpec((1, HEAD_DIM), lambda b, t: (0, 0))],
        out_specs=pl.BlockSpec((tc, n_heads * HEAD_DIM), lambda b, t: (b * nt + t, 0)),
        out_shape=jax.ShapeDtypeStruct((batch * seq, n_heads * HEAD_DIM), BF16),
        scratch_shapes=[pltpu.VMEM((1, HEAD_DIM), F32)],
        compiler_params=_params("parallel", "arbitrary"),
        name="fox_forget_cumsum",
    )(f_logit, bias)


def _head_rmsnorm(x, g):
    x = x.astype(F32)
    ms = jnp.mean(x * x, axis=-1, keepdims=True)
    return (x * lax.rsqrt(ms + RMS_EPS) * g).astype(BF16)


def _attn_kernel(q_ref, k_ref, kb_ref, v_ref, gq_ref, gk_ref, o_ref, kp_ref, s_ref, m_ref, l_ref, acc_ref, *,
                 blk, heads):
    qi = pl.program_id(2)
    hcols = [slice(hh * HEAD_DIM, (hh + 1) * HEAD_DIM) for hh in range(heads)]
    seq = k_ref.shape[0]

    @pl.when(qi == 0)
    def _():
        def fill(c, carry):
            rows = pl.ds(pl.multiple_of(c * blk, blk), blk)
            for hh, cols in enumerate(hcols):
                kp_ref[hh, rows, :HEAD_DIM] = _head_rmsnorm(k_ref[rows, cols], gk_ref[...])
                kp_ref[hh, rows, HEAD_DIM:] = kb_ref[rows, cols]
            return carry
        lax.fori_loop(0, seq // blk, fill, 0)

    lane = lax.broadcasted_iota(jnp.int32, (blk, HEAD_DIM), 1)
    q_ones = jnp.where(lane < 3, 1.0, 0.0).astype(BF16)
    key_pos = lax.broadcasted_iota(jnp.int32, (blk, blk), 0)
    qry_pos = lax.broadcasted_iota(jnp.int32, (blk, blk), 1)

    q2 = [jnp.concatenate([_head_rmsnorm(q_ref[:, cols], gq_ref[...]), q_ones], axis=1) for cols in hcols]

    def scores(hh, j, slot):
        rows = pl.ds(pl.multiple_of(j * blk, blk), blk)
        s_ref[hh, slot] = lax.dot_general(kp_ref[hh, rows, :], q2[hh], NT_DIMS, preferred_element_type=F32)

    def update(hh, j, slot, diagonal=False):
        rows = pl.ds(pl.multiple_of(j * blk, blk), blk)
        s = s_ref[hh, slot]
        if diagonal:
            s = jnp.where(key_pos <= qry_pos, s, -jnp.inf)
            m_new = jnp.max(s, axis=0, keepdims=True)
        else:
            m = m_ref[hh]
            m_new = jnp.maximum(m, jnp.max(s, axis=0, keepdims=True))
        p = jnp.exp2(s - m_new)
        pv = lax.dot_general(v_ref[rows, hcols[hh]], p.astype(BF16), TN_DIMS, preferred_element_type=F32)
        m_ref[hh] = m_new
        if diagonal:
            l_ref[hh] = jnp.sum(p, axis=0, keepdims=True)
            acc_ref[hh] = pv
        else:
            alpha = jnp.exp2(m - m_new)
            l_ref[hh] = alpha * l_ref[hh] + jnp.sum(p, axis=0, keepdims=True)
            acc_ref[hh] = alpha * acc_ref[hh] + pv

    def each_head(fn, *args, **kwargs):
        for hh in range(heads):
            fn(hh, *args, **kwargs)

    def pair(t0):
        each_head(scores, t0 + 1, 1)
        for hh in range(heads):
            update(hh, t0, 0)
            scores(hh, t0 + 2, 0)
        each_head(update, t0 + 1, 1)

    def quad(k, carry):
        pair(4 * k)
        pair(4 * k + 2)
        return carry

    each_head(scores, qi, 1)
    each_head(scores, 0, 0)
    each_head(update, qi, 1, diagonal=True)
    lax.fori_loop(0, qi // 4, quad, 0)

    @pl.when((qi // 2) % 2 == 1)
    def _():
        pair(4 * (qi // 4))

    @pl.when(qi % 2 == 1)
    def _():
        each_head(update, qi - 1, 0)

    for hh, cols in enumerate(hcols):
        o_ref[:, cols] = (acc_ref[hh] / l_ref[hh]).T.astype(o_ref.dtype)


def _fox_attention(proj, kbias, gq, gk, batch, seq, n_heads, blk=512, heads=2):
    blk = _tile(seq, blk)
    heads = _tile(n_heads, heads)
    nq = seq // blk
    hblk = n_heads // heads
    w = heads * HEAD_DIM
    gain = pl.BlockSpec((1, HEAD_DIM), lambda b, h, i: (0, 0))
    return pl.pallas_call(
        functools.partial(_attn_kernel, blk=blk, heads=heads),
        grid=(batch, hblk, nq),
        in_specs=[pl.BlockSpec((blk, w), lambda b, h, i: (b * nq + i, h)),
                  pl.BlockSpec((seq, w), lambda b, h, i: (b, hblk + h)),
                  pl.BlockSpec((seq, w), lambda b, h, i: (b, h)),
                  pl.BlockSpec((seq, w), lambda b, h, i: (b, 2 * hblk + h)), gain, gain],
        out_specs=pl.BlockSpec((blk, w), lambda b, h, i: (b * nq + i, h)),
        out_shape=jax.ShapeDtypeStruct((batch * seq, n_heads * HEAD_DIM), BF16),
        scratch_shapes=[pltpu.VMEM((heads, seq, 2 * HEAD_DIM), BF16), pltpu.VMEM((heads, 2, blk, blk), F32),
                        pltpu.VMEM((heads, 1, blk), F32), pltpu.VMEM((heads, 1, blk), F32),
                        pltpu.VMEM((heads, HEAD_DIM, blk), F32)],
        compiler_params=_params("parallel", "parallel", "arbitrary"),
        name="fox_attention",
    )(proj, proj, kbias, proj, gq, gk)


def _hgrn_kernel(q_ref, z_ref, v_ref, gate_ref, lbl_ref, gn_ref, o_ref, st_ref, rb_ref, *, layer, heads, chunks):
    C, c = HGRN_CHUNK, HGRN_SUB
    t = pl.program_id(2)

    @pl.when(t == 0)
    def _():
        st_ref[...] = jnp.zeros_like(st_ref)

    logits = lbl_ref[...]
    e = jnp.exp(logits - jnp.max(logits, axis=0, keepdims=True))
    lb_all = jnp.sum(e[:layer + 1], axis=0, keepdims=True) / jnp.sum(e, axis=0, keepdims=True)

    row = lax.broadcasted_iota(jnp.int32, (C, C), 0)
    col = lax.broadcasted_iota(jnp.int32, (C, C), 1)
    ltri = jnp.where(col <= row, 1.0, 0.0).astype(BF16)
    sizes = [c << l for l in range(1, (C // c).bit_length())]
    xor = row ^ col
    level = jnp.zeros((C, C), jnp.int32)
    for l, m in enumerate(sizes):
        level = jnp.where(xor >= m // 2, l + 1, level)
    level = jnp.where(col > row, -1, level)
    lane = lax.broadcasted_iota(jnp.int32, (c, C), 1)

    def chunk_body(ci, carry):
        r0 = pl.multiple_of(ci * C, C)
        hs = range(heads)
        cols = [slice(hh * HEAD_DIM, (hh + 1) * HEAD_DIM) for hh in hs]
        qs, kk, b, v, o, a = {}, {}, {}, {}, {}, {}
        for hh in hs:
            lb = lb_all[:, cols[hh]]
            q = q_ref[pl.ds(r0, C), cols[hh]].astype(F32)
            z = z_ref[pl.ds(r0, C), cols[hh]].astype(F32)
            v[hh] = v_ref[pl.ds(r0, C), cols[hh]]
            ez = jnp.exp(-jnp.abs(z))
            inv = 1.0 / (1.0 + ez)
            sig_z = jnp.where(z >= 0, 1.0, ez) * inv
            sig_mz = jnp.where(z >= 0, ez, 1.0) * inv
            kk[hh] = (1.0 - lb) * sig_mz
            qs[hh] = q * _sigmoid(q)
            log2f = jnp.concatenate(_split3(jnp.log2(lb + (1.0 - lb) * sig_z)), axis=1).astype(BF16)
            cs = jnp.dot(ltri, log2f, preferred_element_type=F32)
            b[hh] = cs[:, :HEAD_DIM] + cs[:, HEAD_DIM:2 * HEAD_DIM] + cs[:, 2 * HEAD_DIM:]
            rb_ref[2 * hh] = kk[hh]
            rb_ref[2 * hh + 1] = b[hh]

        for hh in hs:
            o[hh] = lax.dot_general((qs[hh] * jnp.exp2(b[hh])).astype(BF16), st_ref[hh].astype(BF16), NT_DIMS,
                                    preferred_element_type=F32)

        blocks = {hh: [] for hh in hs}
        for i in range(C // c):
            rows = slice(i * c, (i + 1) * c)
            blk = {hh: jnp.zeros((c, C), F32) for hh in hs}
            for s in range(i * c, (i + 1) * c):
                at_s = lane == s
                for hh in hs:
                    w = qs[hh][rows] * rb_ref[2 * hh, s:s + 1, :] * jnp.exp2(b[hh][rows] - rb_ref[2 * hh + 1, s:s + 1, :])
                    blk[hh] = jnp.where(at_s, jnp.sum(w, axis=-1, keepdims=True), blk[hh])
            for hh in hs:
                blocks[hh].append(blk[hh])
        in_sub = level == 0
        for hh in hs:
            a[hh] = jnp.where(in_sub, jnp.concatenate(blocks[hh], axis=0), 0.0)

        for l, m in enumerate(sizes):
            at_level = level == l + 1
            for hh in hs:
                bh = b[hh]
                edge = jnp.concatenate(
                    [jnp.broadcast_to(bh[j * m + m // 2 - 1:j * m + m // 2, :], (m, HEAD_DIM)) for j in range(C // m)],
                    axis=0)
                x = jnp.exp2(-jnp.abs(bh - edge))
                pair = lax.dot_general((qs[hh] * x).astype(BF16), (kk[hh] * x).astype(BF16), NT_DIMS,
                                       preferred_element_type=F32)
                a[hh] = jnp.where(at_level, pair, a[hh])

        for hh in hs:
            o[hh] = o[hh] + jnp.dot(a[hh].astype(BF16), v[hh], preferred_element_type=F32)
            b_last = b[hh][C - 1:C, :]
            k_d = (kk[hh] * jnp.exp2(b_last - b[hh])).astype(BF16)
            st_ref[hh] = st_ref[hh] * jnp.exp2(b_last) + lax.dot_general(v[hh], k_d, TN_DIMS,
                                                                         preferred_element_type=F32)

        for hh in hs:
            gate = gate_ref[pl.ds(r0, C), cols[hh]].astype(F32)
            ms = jnp.mean(o[hh] * o[hh], axis=-1, keepdims=True)
            on = o[hh] * lax.rsqrt(ms + RMS_EPS) * gn_ref[...]
            o_ref[pl.ds(r0, C), cols[hh]] = (on * (gate * _sigmoid(gate))).astype(o_ref.dtype)
        return carry

    lax.fori_loop(0, chunks, chunk_body, 0)


def _hgrn2(proj, lb_logits, gn, layer, batch, seq, n_heads, col0, heads=4, rows=512):
    heads = _tile(n_heads, heads)
    rows = _tile(seq, rows)
    assert rows % HGRN_CHUNK == 0 and col0 % heads == 0
    nt = seq // rows
    w = heads * HEAD_DIM
    hblk = n_heads // heads

    def in_spec(group):
        return pl.BlockSpec((rows, w), lambda b, h, t: (b * nt + t, col0 // heads + group * hblk + h))

    return pl.pallas_call(
        functools.partial(_hgrn_kernel, layer=layer, heads=heads, chunks=rows // HGRN_CHUNK),
        grid=(batch, hblk, nt),
        in_specs=[in_spec(0), in_spec(1), in_spec(2), in_spec(3),
                  pl.BlockSpec((lb_logits.shape[0], w), lambda b, h, t: (0, h)),
                  pl.BlockSpec((1, HEAD_DIM), lambda b, h, t: (0, 0))],
        out_specs=pl.BlockSpec((rows, w), lambda b, h, t: (b * nt + t, h)),
        out_shape=jax.ShapeDtypeStruct((batch * seq, n_heads * HEAD_DIM), BF16),
        scratch_shapes=[pltpu.VMEM((heads, HEAD_DIM, HEAD_DIM), F32),
                        pltpu.VMEM((2 * heads, HGRN_CHUNK, HEAD_DIM), F32)],
        compiler_params=_params("parallel", "parallel", "arbitrary"),
        name="hgrn2",
    )(proj, proj, proj, proj, lb_logits, gn)


def _ple_embed_kernel(p_ref, w_ref, g_ref, o_ref):
    e = jnp.dot(p_ref[...].astype(BF16), w_ref[...], preferred_element_type=F32)
    ms = jnp.mean(e * e, axis=-1, keepdims=True)
    o_ref[...] = (e * lax.rsqrt(ms + RMS_EPS) * g_ref[...]).astype(o_ref.dtype)


def _ple_embed(p, w, g):
    m, kd = p.shape
    n = w.shape[1]
    tm = _tile(m, 256)
    return pl.pallas_call(
        _ple_embed_kernel,
        grid=(m // tm,),
        in_specs=[pl.BlockSpec((tm, kd), lambda i: (i, 0)), pl.BlockSpec((kd, n), lambda i: (0, 0)),
                  pl.BlockSpec((1, n), lambda i: (0, 0))],
        out_specs=pl.BlockSpec((tm, n), lambda i: (i, 0)),
        out_shape=jax.ShapeDtypeStruct((m, n), BF16),
        compiler_params=_params("parallel"),
        name="ple_embed",
    )(p, w, g.reshape(1, n).astype(F32))


def _ple_gate_kernel(a_ref, w_ref, e_ref, h_ref, o_ref):
    gate = _sigmoid(jnp.dot(a_ref[...], w_ref[...], preferred_element_type=F32))
    o_ref[...] = h_ref[...] + gate * e_ref[...]


def _ple_gate(a, w, e, h, tm=1024, tn=512):
    m, kd = a.shape
    n = w.shape[1]
    tm, tn = _tile(m, tm), _tile(n, tn)
    return pl.pallas_call(
        _ple_gate_kernel,
        grid=(m // tm, n // tn),
        in_specs=[pl.BlockSpec((tm, kd), lambda i, j: (i, 0)), pl.BlockSpec((kd, tn), lambda i, j: (0, j)),
                  pl.BlockSpec((tm, tn), lambda i, j: (i, j)), pl.BlockSpec((tm, tn), lambda i, j: (i, j))],
        out_specs=pl.BlockSpec((tm, tn), lambda i, j: (i, j)),
        out_shape=jax.ShapeDtypeStruct((m, n), F32),
        compiler_params=_params("parallel", "arbitrary"),
        name="ple_gate",
    )(a, w, e, h)


def _layer(h, p, layer, batch, seq, norm_mix_g, w_in, fox_f_bias, fox_q_norm_g, fox_k_norm_g, hgrn_lb_logits,
           hgrn_norm_g, w_out, norm_mlp_g, w_up, w_down, ple_norm_g, w_ple_gate, w_ple_proj, ple_post_g):
    d = h.shape[1]
    d_a = d_b = d // 2
    n_a, n_b = d_a // HEAD_DIM, d_b // HEAD_DIM
    assert n_a <= HEAD_DIM and w_in.shape[1] == 3 * d_a + n_a + 4 * d_b

    w_in_t = jnp.swapaxes(w_in, 0, 1)
    w_fox = _cast_rows(w_in_t, 0, 3 * d_a)
    w_fa = _cast_rows(w_in_t, 3 * d_a, HEAD_DIM)
    q_gain = (fox_q_norm_g.astype(F32) * (LOG2E / math.sqrt(HEAD_DIM))).reshape(1, HEAD_DIM)
    k_gain = fox_k_norm_g.astype(F32).reshape(1, HEAD_DIM)

    u = _rmsnorm(h, norm_mix_g)
    proj_a, f_logit, w_hgrn = _inproj(u, w_fox, w_fa, w_in_t, 3 * d_a + n_a, 4 * d_b)
    proj_b, w_up_b, w_out_b = _matmul(u, w_hgrn, BF16, w_transposed=True, side=(w_up, w_out), name="inproj_hgrn")

    f_bias = jnp.pad(fox_f_bias.astype(F32), (0, HEAD_DIM - n_a)).reshape(1, HEAD_DIM)
    kbias = _forget_bias_tiles(f_logit, f_bias, batch, seq, n_a)
    out_a = _fox_attention(proj_a, kbias, q_gain, k_gain, batch, seq, n_a)

    out_b = _hgrn2(proj_b, hgrn_lb_logits.astype(F32), hgrn_norm_g.astype(F32).reshape(1, HEAD_DIM), layer,
                   batch, seq, n_b, col0=0)

    h, hg, part = _outproj(out_a, out_b, w_out_b, h, norm_mlp_g)
    hid, w_down_b, w_gate_b = _matmul(hg, w_up_b, BF16, act="relu2", part=part, side=(w_down, w_ple_gate),
                                      vmem_limit=V7X_VMEM_LIMIT_HIGH, name="mlp_up")
    h = _matmul_residual(hid, w_down_b, h)

    e = _ple_embed(p, w_ple_proj.astype(BF16), ple_post_g)
    return _ple_gate(_rmsnorm(h, ple_norm_g), w_gate_b, e, h)


def kernel(x, p, norm_mix_g, w_in, fox_f_bias, fox_q_norm_g, fox_k_norm_g, hgrn_lb_logits, hgrn_norm_g, w_out,
           norm_mlp_g, w_up, w_down, ple_norm_g, w_ple_gate, w_ple_proj, ple_post_g):
    batch, seq, d = x.shape
    h = x.reshape(batch * seq, d)
    for i in range(w_in.shape[0]):
        h = _layer(h, p[i].reshape(batch * seq, -1), i, batch, seq, norm_mix_g[i], w_in[i], fox_f_bias[i],
                   fox_q_norm_g[i], fox_k_norm_g[i], hgrn_lb_logits, hgrn_norm_g[i], w_out[i], norm_mlp_g[i],
                   w_up[i], w_down[i], ple_norm_g[i], w_ple_gate[i], w_ple_proj[i], ple_post_g[i])
    return h.reshape(batch, seq, d)
```

```python
import functools
import math

import jax
import jax.numpy as jnp
from jax import lax
from jax.experimental import pallas as pl
from jax.experimental.pallas import tpu as pltpu

F32, BF16 = jnp.float32, jnp.bfloat16
HEAD_DIM = 128
RMS_EPS = 1e-6
LOG2E = math.log2(math.e)
HGRN_CHUNK = 128
HGRN_SUB = 8
V7X_VMEM_LIMIT = 56 * 1024 * 1024
V7X_VMEM_LIMIT_HIGH = 60 * 1024 * 1024
V7X_VMEM_LIMIT_MAX = 63 * 1024 * 1024

NT_DIMS = (((1,), (1,)), ((), ()))
TN_DIMS = (((0,), (0,)), ((), ()))


def _params(*semantics, vmem_limit=V7X_VMEM_LIMIT):
    return pltpu.CompilerParams(dimension_semantics=semantics, vmem_limit_bytes=vmem_limit)


def _tile(dim, pref):
    t = min(dim, pref)
    assert dim % t == 0, (dim, pref)
    return t


def _sigmoid(x):
    return 1.0 / (1.0 + jnp.exp(-x))


def _rmsnorm_kernel(x_ref, g_ref, o_ref):
    x = x_ref[...]
    ms = jnp.mean(x * x, axis=-1, keepdims=True)
    o_ref[...] = (x * lax.rsqrt(ms + RMS_EPS) * g_ref[...]).astype(o_ref.dtype)


def _rmsnorm(x, g, out_dtype=BF16):
    m, d = x.shape
    tm = _tile(m, 256)
    return pl.pallas_call(
        _rmsnorm_kernel,
        grid=(m // tm,),
        in_specs=[pl.BlockSpec((tm, d), lambda i: (i, 0)), pl.BlockSpec((1, d), lambda i: (0, 0))],
        out_specs=pl.BlockSpec((tm, d), lambda i: (i, 0)),
        out_shape=jax.ShapeDtypeStruct((m, d), out_dtype),
        compiler_params=_params("parallel"),
        name="rmsnorm",
    )(x, g.reshape(1, d).astype(F32))


def _row_scale(part_ref, d):
    return lax.rsqrt(jnp.sum(part_ref[...], axis=-1, keepdims=True) * (1.0 / d) + RMS_EPS)


def _fold_sq(x):
    sq = x * x
    out = sq[:, :HEAD_DIM]
    for t in range(1, x.shape[1] // HEAD_DIM):
        out = out + sq[:, t * HEAD_DIM:(t + 1) * HEAD_DIM]
    return out


def _mm_kernel(*refs, act, w_dims, scaled, side_scaled):
    n_side, n_scale = len(side_scaled), sum(side_scaled)
    a_ref, w_ref = refs[:2]
    side_in = refs[2 + scaled:2 + scaled + n_side]
    scales = iter(refs[2 + scaled + n_side:2 + scaled + n_side + n_scale])
    o_ref = refs[2 + scaled + n_side + n_scale]
    side_out = refs[3 + scaled + n_side + n_scale:3 + scaled + 2 * n_side + n_scale]
    if scaled:
        part_ref, rs_ref = refs[2], refs[-1]

        @pl.when(pl.program_id(1) == 0)
        def _():
            rs_ref[...] = _row_scale(part_ref, a_ref.shape[1])

    for src, dst, has_scale in zip(side_in, side_out, side_scaled):
        slab = src[...] * next(scales)[...] if has_scale else src[...]
        dst[...] = slab.astype(dst.dtype)
    acc = lax.dot_general(a_ref[...], w_ref[...], w_dims, preferred_element_type=F32)
    if scaled:
        acc = acc * rs_ref[...]
    if act == "relu2":
        acc = jnp.square(jnp.maximum(acc, 0.0))
    o_ref[...] = acc.astype(o_ref.dtype)


def _matmul(a, w, out_dtype, act=None, part=None, w_transposed=False, side=(), tm=1024, tn=1024,
            vmem_limit=V7X_VMEM_LIMIT, name="matmul"):
    m, k = a.shape
    n = w.shape[0] if w_transposed else w.shape[1]
    tm, tn = _tile(m, tm), _tile(n, tn)
    ni, nj = m // tm, n // tn
    w_spec = pl.BlockSpec((tn, k), lambda i, j: (j, 0)) if w_transposed else pl.BlockSpec((k, tn), lambda i, j: (0, j))
    in_specs, operands = [pl.BlockSpec((tm, k), lambda i, j: (i, 0)), w_spec], [a, w]
    scratch = []
    if part is not None:
        in_specs.append(pl.BlockSpec((tm, part.shape[1]), lambda i, j: (i, 0)))
        operands.append(part)
        scratch.append(pltpu.VMEM((tm, 1), F32))
    out_specs = [pl.BlockSpec((tm, tn), lambda i, j: (i, j))]
    out_shape = [jax.ShapeDtypeStruct((m, n), out_dtype)]
    side = [sw if isinstance(sw, tuple) else (sw, None) for sw in side]
    scale_specs, scale_operands = [], []
    for sw, row_scale in side:
        rows = sw.shape[0] // (ni * nj)
        assert rows * ni * nj == sw.shape[0]
        slab = pl.BlockSpec((rows, sw.shape[1]), lambda i, j: (i * nj + j, 0))
        in_specs.append(slab)
        operands.append(sw)
        out_specs.append(slab)
        out_shape.append(jax.ShapeDtypeStruct(sw.shape, BF16))
        if row_scale is not None:
            scale_specs.append(pl.BlockSpec((rows, 1), lambda i, j: (i * nj + j, 0)))
            scale_operands.append(row_scale.astype(F32).reshape(-1, 1))
    in_specs += scale_specs
    operands += scale_operands
    w_dims = NT_DIMS if w_transposed else (((1,), (0,)), ((), ()))
    outs = pl.pallas_call(
        functools.partial(_mm_kernel, act=act, w_dims=w_dims, scaled=part is not None,
                          side_scaled=tuple(rs is not None for _, rs in side)),
        grid=(ni, nj),
        in_specs=in_specs,
        out_specs=out_specs,
        out_shape=out_shape,
        scratch_shapes=scratch,
        compiler_params=_params("parallel", "arbitrary", vmem_limit=vmem_limit),
        name=name,
    )(*operands)
    return outs if side else outs[0]


def _cast_rows_kernel(w_ref, o_ref):
    o_ref[...] = w_ref[...].astype(o_ref.dtype)


def _cast_rows(w, row0, nrows, tr=512):
    k = w.shape[1]
    tr = math.gcd(nrows, tr)
    align = math.gcd(math.gcd(row0, tr), 64)
    return pl.pallas_call(
        _cast_rows_kernel,
        grid=(nrows // tr,),
        in_specs=[pl.BlockSpec((pl.Element(tr), pl.Element(k)),
                               lambda i: (pl.multiple_of(row0 + i * tr, align), 0))],
        out_specs=pl.BlockSpec((tr, k), lambda i: (i, 0)),
        out_shape=jax.ShapeDtypeStruct((nrows, k), BF16),
        compiler_params=_params("parallel"),
        name="cast_weight_rows",
    )(w)


def _inproj_kernel(u_ref, wt_ref, wft_ref, side_ref, o_ref, f_ref, side_out_ref):
    @pl.when(pl.program_id(1) == 0)
    def _():
        f_ref[...] = lax.dot_general(u_ref[...], wft_ref[...], NT_DIMS, preferred_element_type=F32)

    side_out_ref[...] = side_ref[...].astype(side_out_ref.dtype)
    o_ref[...] = lax.dot_general(u_ref[...], wt_ref[...], NT_DIMS, preferred_element_type=F32).astype(o_ref.dtype)


def _inproj(u, wt, wft, side_src, side_row0, side_rows, tm=1024, tn=768):
    m, k = u.shape
    n, nf = wt.shape[0], wft.shape[0]
    tm, tn = _tile(m, tm), math.gcd(n, tn)
    ni, nj = m // tm, n // tn
    slab = side_rows // (ni * nj)
    assert slab * ni * nj == side_rows
    align = math.gcd(math.gcd(side_row0, slab), 64)
    return pl.pallas_call(
        _inproj_kernel,
        grid=(ni, nj),
        in_specs=[pl.BlockSpec((tm, k), lambda i, j: (i, 0)), pl.BlockSpec((tn, k), lambda i, j: (j, 0)),
                  pl.BlockSpec((nf, k), lambda i, j: (0, 0)),
                  pl.BlockSpec((pl.Element(slab), pl.Element(side_src.shape[1])),
                               lambda i, j: (pl.multiple_of(side_row0 + (i * nj + j) * slab, align), 0))],
        out_specs=[pl.BlockSpec((tm, tn), lambda i, j: (i, j)), pl.BlockSpec((tm, nf), lambda i, j: (i, 0)),
                   pl.BlockSpec((slab, side_src.shape[1]), lambda i, j: (i * nj + j, 0))],
        out_shape=[jax.ShapeDtypeStruct((m, n), BF16), jax.ShapeDtypeStruct((m, nf), F32),
                   jax.ShapeDtypeStruct((side_rows, side_src.shape[1]), BF16)],
        compiler_params=_params("parallel", "arbitrary"),
        name="inproj_fox",
    )(u, wt, wft, side_src)


def _outproj_kernel(a_ref, b_ref, wa_ref, wb_ref, x_ref, g_ref, o_ref, ob_ref, part_ref):
    acc = jnp.dot(a_ref[...], wa_ref[...], preferred_element_type=F32)
    acc = acc + jnp.dot(b_ref[...], wb_ref[...], preferred_element_type=F32)
    h = x_ref[...] + acc
    o_ref[...] = h
    ob_ref[...] = (h * g_ref[...]).astype(ob_ref.dtype)
    part_ref[...] = _fold_sq(h)


def _outproj(a, b, w, x, g, tm=1024, tn=512):
    m, ka = a.shape
    kb = b.shape[1]
    n = w.shape[1]
    assert ka == kb and w.shape[0] == ka + kb
    tm, tn = _tile(m, tm), _tile(n, tn)
    tile = pl.BlockSpec((tm, tn), lambda i, j: (i, j))
    return pl.pallas_call(
        _outproj_kernel,
        grid=(m // tm, n // tn),
        in_specs=[pl.BlockSpec((tm, ka), lambda i, j: (i, 0)), pl.BlockSpec((tm, kb), lambda i, j: (i, 0)),
                  pl.BlockSpec((ka, tn), lambda i, j: (0, j)), pl.BlockSpec((kb, tn), lambda i, j: (1, j)), tile,
                  pl.BlockSpec((1, tn), lambda i, j: (0, j))],
        out_specs=[tile, tile, pl.BlockSpec((tm, HEAD_DIM), lambda i, j: (i, j))],
        out_shape=[jax.ShapeDtypeStruct((m, n), F32), jax.ShapeDtypeStruct((m, n), BF16),
                   jax.ShapeDtypeStruct((m, n // tn * HEAD_DIM), F32)],
        compiler_params=_params("parallel", "arbitrary"),
        name="outproj",
    )(a, b, w, w, x, g.reshape(1, n).astype(F32))


def _mm_res_kernel(a_ref, w_ref, r_ref, o_ref, ob_ref):
    k = pl.program_id(2)

    @pl.when(k == 0)
    def _():
        o_ref[...] = r_ref[...]

    o_ref[...] += jnp.dot(a_ref[...], w_ref[...], preferred_element_type=F32)

    @pl.when(k == pl.num_programs(2) - 1)
    def _():
        ob_ref[...] = o_ref[...].astype(ob_ref.dtype)


def _matmul_residual(a, w, r, tm=1024, tn=1024, tk=4096):
    m, kd = a.shape
    n = w.shape[1]
    tm, tn, tk = _tile(m, tm), _tile(n, tn), _tile(kd, tk)
    return pl.pallas_call(
        _mm_res_kernel,
        grid=(m // tm, n // tn, kd // tk),
        in_specs=[pl.BlockSpec((tm, tk), lambda i, j, k: (i, k)), pl.BlockSpec((tk, tn), lambda i, j, k: (k, j)),
                  pl.BlockSpec((tm, tn), lambda i, j, k: (i, j))],
        out_specs=[pl.BlockSpec((tm, tn), lambda i, j, k: (i, j)), pl.BlockSpec((tm, tn), lambda i, j, k: (i, j))],
        out_shape=[jax.ShapeDtypeStruct((m, n), F32), jax.ShapeDtypeStruct((m, n), BF16)],
        compiler_params=_params("parallel", "parallel", "arbitrary", vmem_limit=V7X_VMEM_LIMIT_MAX),
        name="mlp_down",
    )(a, w, r)


def _split3(x):
    hi = x.astype(BF16).astype(F32)
    r = x - hi
    mid = r.astype(BF16).astype(F32)
    return hi, mid, r - mid


def _fcum_kernel(f_ref, bias_ref, o_ref, carry_ref, *, n_heads):
    t = pl.program_id(1)

    @pl.when(t == 0)
    def _():
        carry_ref[...] = jnp.zeros_like(carry_ref)

    x = f_ref[...] + bias_ref[...]
    ls = jnp.minimum(x, 0.0) - jnp.log1p(jnp.exp(-jnp.abs(x)))
    tc = x.shape[0]
    r = lax.broadcasted_iota(jnp.int32, (tc, tc), 0)
    c = lax.broadcasted_iota(jnp.int32, (tc, tc), 1)
    tri = jnp.where(c <= r, 1.0, 0.0).astype(BF16)
    c3 = jnp.dot(tri, jnp.concatenate(_split3(ls), axis=1).astype(BF16), preferred_element_type=F32)
    cs = c3[:, :HEAD_DIM] + c3[:, HEAD_DIM:2 * HEAD_DIM] + c3[:, 2 * HEAD_DIM:] + carry_ref[...]
    carry_ref[...] = cs[tc - 1:tc, :]
    lane = lax.broadcasted_iota(jnp.int32, (tc, HEAD_DIM), 1)
    for h in range(n_heads):
        hi, mid, lo = _split3(jnp.broadcast_to(cs[:, h:h + 1] * (-LOG2E), (tc, HEAD_DIM)))
        tile = jnp.where(lane == 0, hi, jnp.where(lane == 1, mid, jnp.where(lane == 2, lo, 0.0)))
        o_ref[:, h * HEAD_DIM:(h + 1) * HEAD_DIM] = tile.astype(o_ref.dtype)


def _forget_bias_tiles(f_logit, bias, batch, seq, n_heads):
    tc = _tile(seq, 256)
    nt = seq // tc
    return pl.pallas_call(
        functools.partial(_fcum_kernel, n_heads=n_heads),
        grid=(batch, nt),
        in_specs=[pl.BlockSpec((tc, HEAD_DIM), lambda b, t: (b * nt + t, 0)),
                  pl.BlockSpec((1, HEAD_DIM), lambda b, t: (0, 0))],
        out_specs=pl.BlockSpec((tc, n_heads * HEAD_DIM), lambda b, t: (b * nt + t, 0)),
        out_shape=jax.ShapeDtypeStruct((batch * seq, n_heads * HEAD_DIM), BF16),
        scratch_shapes=[pltpu.VMEM((1, HEAD_DIM), F32)],
        compiler_params=_params("parallel", "arbitrary"),
        name="fox_forget_cumsum",
    )(f_logit, bias)


def _head_rmsnorm(x, g):
    x = x.astype(F32)
    ms = jnp.mean(x * x, axis=-1, keepdims=True)
    return (x * lax.rsqrt(ms + RMS_EPS) * g).astype(BF16)


def _attn_kernel(q_ref, k_ref, kb_ref, v_ref, gq_ref, gk_ref, o_ref, kp_ref, s_ref, m_ref, l_ref, acc_ref, *,
                 blk, heads):
    qi = pl.program_id(2)
    hcols = [slice(hh * HEAD_DIM, (hh + 1) * HEAD_DIM) for hh in range(heads)]
    seq = k_ref.shape[0]

    @pl.when(qi == 0)
    def _():
        def fill(c, carry):
            rows = pl.ds(pl.multiple_of(c * blk, blk), blk)
            for hh, cols in enumerate(hcols):
                kp_ref[hh, rows, :HEAD_DIM] = _head_rmsnorm(k_ref[rows, cols], gk_ref[...])
                kp_ref[hh, rows, HEAD_DIM:] = kb_ref[rows, cols]
            return carry
        lax.fori_loop(0, seq // blk, fill, 0)

    lane = lax.broadcasted_iota(jnp.int32, (blk, HEAD_DIM), 1)
    q_ones = jnp.where(lane < 3, 1.0, 0.0).astype(BF16)
    key_pos = lax.broadcasted_iota(jnp.int32, (blk, blk), 0)
    qry_pos = lax.broadcasted_iota(jnp.int32, (blk, blk), 1)

    q2 = [jnp.concatenate([_head_rmsnorm(q_ref[:, cols], gq_ref[...]), q_ones], axis=1) for cols in hcols]

    def scores(hh, j, slot):
        rows = pl.ds(pl.multiple_of(j * blk, blk), blk)
        s_ref[hh, slot] = lax.dot_general(kp_ref[hh, rows, :], q2[hh], NT_DIMS, preferred_element_type=F32)

    def update(hh, j, slot, diagonal=False):
        rows = pl.ds(pl.multiple_of(j * blk, blk), blk)
        s = s_ref[hh, slot]
        if diagonal:
            s = jnp.where(key_pos <= qry_pos, s, -jnp.inf)
            m_new = jnp.max(s, axis=0, keepdims=True)
        else:
            m = m_ref[hh]
            m_new = jnp.maximum(m, jnp.max(s, axis=0, keepdims=True))
        p = jnp.exp2(s - m_new)
        pv = lax.dot_general(v_ref[rows, hcols[hh]], p.astype(BF16), TN_DIMS, preferred_element_type=F32)
        m_ref[hh] = m_new
        if diagonal:
            l_ref[hh] = jnp.sum(p, axis=0, keepdims=True)
            acc_ref[hh] = pv
        else:
            alpha = jnp.exp2(m - m_new)
            l_ref[hh] = alpha * l_ref[hh] + jnp.sum(p, axis=0, keepdims=True)
            acc_ref[hh] = alpha * acc_ref[hh] + pv

    def each_head(fn, *args, **kwargs):
        for hh in range(heads):
            fn(hh, *args, **kwargs)

    def pair(t0):
        each_head(scores, t0 + 1, 1)
        for hh in range(heads):
            update(hh, t0, 0)
            scores(hh, t0 + 2, 0)
        each_head(update, t0 + 1, 1)

    def quad(k, carry):
        pair(4 * k)
        pair(4 * k + 2)
        return carry

    each_head(scores, qi, 1)
    each_head(scores, 0, 0)
    each_head(update, qi, 1, diagonal=True)
    lax.fori_loop(0, qi // 4, quad, 0)

    @pl.when((qi // 2) % 2 == 1)
    def _():
        pair(4 * (qi // 4))

    @pl.when(qi % 2 == 1)
    def _():
        each_head(update, qi - 1, 0)

    for hh, cols in enumerate(hcols):
        o_ref[:, cols] = (acc_ref[hh] / l_ref[hh]).T.astype(o_ref.dtype)


def _fox_attention(proj, kbias, gq, gk, batch, seq, n_heads, blk=512, heads=2):
    blk = _tile(seq, blk)
    heads = _tile(n_heads, heads)
    nq = seq // blk
    hblk = n_heads // heads
    w = heads * HEAD_DIM
    gain = pl.BlockSpec((1, HEAD_DIM), lambda b, h, i: (0, 0))
    return pl.pallas_call(
        functools.partial(_attn_kernel, blk=blk, heads=heads),
        grid=(batch, hblk, nq),
        in_specs=[pl.BlockSpec((blk, w), lambda b, h, i: (b * nq + i, h)),
                  pl.BlockSpec((seq, w), lambda b, h, i: (b, hblk + h)),
                  pl.BlockSpec((seq, w), lambda b, h, i: (b, h)),
                  pl.BlockSpec((seq, w), lambda b, h, i: (b, 2 * hblk + h)), gain, gain],
        out_specs=pl.BlockSpec((blk, w), lambda b, h, i: (b * nq + i, h)),
        out_shape=jax.ShapeDtypeStruct((batch * seq, n_heads * HEAD_DIM), BF16),
        scratch_shapes=[pltpu.VMEM((heads, seq, 2 * HEAD_DIM), BF16), pltpu.VMEM((heads, 2, blk, blk), F32),
                        pltpu.VMEM((heads, 1, blk), F32), pltpu.VMEM((heads, 1, blk), F32),
                        pltpu.VMEM((heads, HEAD_DIM, blk), F32)],
        compiler_params=_params("parallel", "parallel", "arbitrary"),
        name="fox_attention",
    )(proj, proj, kbias, proj, gq, gk)


def _hgrn_kernel(q_ref, z_ref, v_ref, gate_ref, lbl_ref, gn_ref, o_ref, st_ref, rb_ref, *, layer, heads, chunks):
    C, c = HGRN_CHUNK, HGRN_SUB
    t = pl.program_id(2)

    @pl.when(t == 0)
    def _():
        st_ref[...] = jnp.zeros_like(st_ref)

    logits = lbl_ref[...]
    e = jnp.exp(logits - jnp.max(logits, axis=0, keepdims=True))
    lb_all = jnp.sum(e[:layer + 1], axis=0, keepdims=True) / jnp.sum(e, axis=0, keepdims=True)

    row = lax.broadcasted_iota(jnp.int32, (C, C), 0)
    col = lax.broadcasted_iota(jnp.int32, (C, C), 1)
    ltri = jnp.where(col <= row, 1.0, 0.0).astype(BF16)
    sizes = [c << l for l in range(1, (C // c).bit_length())]
    xor = row ^ col
    level = jnp.zeros((C, C), jnp.int32)
    for l, m in enumerate(sizes):
        level = jnp.where(xor >= m // 2, l + 1, level)
    level = jnp.where(col > row, -1, level)
    lane = lax.broadcasted_iota(jnp.int32, (c, C), 1)

    def chunk_body(ci, carry):
        r0 = pl.multiple_of(ci * C, C)
        hs = range(heads)
        cols = [slice(hh * HEAD_DIM, (hh + 1) * HEAD_DIM) for hh in hs]
        qs, kk, b, v, o, a = {}, {}, {}, {}, {}, {}
        for hh in hs:
            lb = lb_all[:, cols[hh]]
            q = q_ref[pl.ds(r0, C), cols[hh]].astype(F32)
            z = z_ref[pl.ds(r0, C), cols[hh]].astype(F32)
            v[hh] = v_ref[pl.ds(r0, C), cols[hh]]
            ez = jnp.exp(-jnp.abs(z))
            inv = 1.0 / (1.0 + ez)
            sig_z = jnp.where(z >= 0, 1.0, ez) * inv
            sig_mz = jnp.where(z >= 0, ez, 1.0) * inv
            kk[hh] = (1.0 - lb) * sig_mz
            qs[hh] = q * _sigmoid(q)
            log2f = jnp.concatenate(_split3(jnp.log2(lb + (1.0 - lb) * sig_z)), axis=1).astype(BF16)
            cs = jnp.dot(ltri, log2f, preferred_element_type=F32)
            b[hh] = cs[:, :HEAD_DIM] + cs[:, HEAD_DIM:2 * HEAD_DIM] + cs[:, 2 * HEAD_DIM:]
            rb_ref[2 * hh] = kk[hh]
            rb_ref[2 * hh + 1] = b[hh]

        for hh in hs:
            o[hh] = lax.dot_general((qs[hh] * jnp.exp2(b[hh])).astype(BF16), st_ref[hh].astype(BF16), NT_DIMS,
                                    preferred_element_type=F32)

        blocks = {hh: [] for hh in hs}
        for i in range(C // c):
            rows = slice(i * c, (i + 1) * c)
            blk = {hh: jnp.zeros((c, C), F32) for hh in hs}
            for s in range(i * c, (i + 1) * c):
                at_s = lane == s
                for hh in hs:
                    w = qs[hh][rows] * rb_ref[2 * hh, s:s + 1, :] * jnp.exp2(b[hh][rows] - rb_ref[2 * hh + 1, s:s + 1, :])
                    blk[hh] = jnp.where(at_s, jnp.sum(w, axis=-1, keepdims=True), blk[hh])
            for hh in hs:
                blocks[hh].append(blk[hh])
        in_sub = level == 0
        for hh in hs:
            a[hh] = jnp.where(in_sub, jnp.concatenate(blocks[hh], axis=0), 0.0)

        for l, m in enumerate(sizes):
            at_level = level == l + 1
            for hh in hs:
                bh = b[hh]
                edge = jnp.concatenate(
                    [jnp.broadcast_to(bh[j * m + m // 2 - 1:j * m + m // 2, :], (m, HEAD_DIM)) for j in range(C // m)],
                    axis=0)
                x = jnp.exp2(-jnp.abs(bh - edge))
                pair = lax.dot_general((qs[hh] * x).astype(BF16), (kk[hh] * x).astype(BF16), NT_DIMS,
                                       preferred_element_type=F32)
                a[hh] = jnp.where(at_level, pair, a[hh])

        for hh in hs:
            o[hh] = o[hh] + jnp.dot(a[hh].astype(BF16), v[hh], preferred_element_type=F32)
            b_last = b[hh][C - 1:C, :]
            k_d = (kk[hh] * jnp.exp2(b_last - b[hh])).astype(BF16)
            st_ref[hh] = st_ref[hh] * jnp.exp2(b_last) + lax.dot_general(v[hh], k_d, TN_DIMS,
                                                                         preferred_element_type=F32)

        for hh in hs:
            gate = gate_ref[pl.ds(r0, C), cols[hh]].astype(F32)
            ms = jnp.mean(o[hh] * o[hh], axis=-1, keepdims=True)
            on = o[hh] * lax.rsqrt(ms + RMS_EPS) * gn_ref[...]
            o_ref[pl.ds(r0, C), cols[hh]] = (on * (gate * _sigmoid(gate))).astype(o_ref.dtype)
        return carry

    lax.fori_loop(0, chunks, chunk_body, 0)


def _hgrn2(proj, lb_logits, gn, layer, batch, seq, n_heads, col0, heads=4, rows=512):
    heads = _tile(n_heads, heads)
    rows = _tile(seq, rows)
    assert rows % HGRN_CHUNK == 0 and col0 % heads == 0
    nt = seq // rows
    w = heads * HEAD_DIM
    hblk = n_heads // heads

    def in_spec(group):
        return pl.BlockSpec((rows, w), lambda b, h, t: (b * nt + t, col0 // heads + group * hblk + h))

    return pl.pallas_call(
        functools.partial(_hgrn_kernel, layer=layer, heads=heads, chunks=rows // HGRN_CHUNK),
        grid=(batch, hblk, nt),
        in_specs=[in_spec(0), in_spec(1), in_spec(2), in_spec(3),
                  pl.BlockSpec((lb_logits.shape[0], w), lambda b, h, t: (0, h)),
                  pl.BlockSpec((1, HEAD_DIM), lambda b, h, t: (0, 0))],
        out_specs=pl.BlockSpec((rows, w), lambda b, h, t: (b * nt + t, h)),
        out_shape=jax.ShapeDtypeStruct((batch * seq, n_heads * HEAD_DIM), BF16),
        scratch_shapes=[pltpu.VMEM((heads, HEAD_DIM, HEAD_DIM), F32),
                        pltpu.VMEM((2 * heads, HGRN_CHUNK, HEAD_DIM), F32)],
        compiler_params=_params("parallel", "parallel", "arbitrary"),
        name="hgrn2",
    )(proj, proj, proj, proj, lb_logits, gn)


def _ple_embed_kernel(p_ref, w_ref, g_ref, o_ref):
    e = jnp.dot(p_ref[...].astype(BF16), w_ref[...], preferred_element_type=F32)
    ms = jnp.mean(e * e, axis=-1, keepdims=True)
    o_ref[...] = (e * lax.rsqrt(ms + RMS_EPS) * g_ref[...]).astype(o_ref.dtype)


def _ple_embed(p, w, g):
    m, kd = p.shape
    n = w.shape[1]
    tm = _tile(m, 256)
    return pl.pallas_call(
        _ple_embed_kernel,
        grid=(m // tm,),
        in_specs=[pl.BlockSpec((tm, kd), lambda i: (i, 0)), pl.BlockSpec((kd, n), lambda i: (0, 0)),
                  pl.BlockSpec((1, n), lambda i: (0, 0))],
        out_specs=pl.BlockSpec((tm, n), lambda i: (i, 0)),
        out_shape=jax.ShapeDtypeStruct((m, n), BF16),
        compiler_params=_params("parallel"),
        name="ple_embed",
    )(p, w, g.reshape(1, n).astype(F32))


def _ple_gate_kernel(a_ref, w_ref, e_ref, h_ref, o_ref, rs_ref):
    @pl.when(pl.program_id(1) == 0)
    def _():
        a = a_ref[...].astype(F32)
        rs_ref[...] = lax.rsqrt(jnp.mean(a * a, axis=-1, keepdims=True) + RMS_EPS)

    gate = _sigmoid(jnp.dot(a_ref[...], w_ref[...], preferred_element_type=F32) * rs_ref[...])
    o_ref[...] = h_ref[...] + gate * e_ref[...]


def _ple_gate(a, w, e, h, tm=1024, tn=512):
    m, kd = a.shape
    n = w.shape[1]
    tm, tn = _tile(m, tm), _tile(n, tn)
    return pl.pallas_call(
        _ple_gate_kernel,
        grid=(m // tm, n // tn),
        in_specs=[pl.BlockSpec((tm, kd), lambda i, j: (i, 0)), pl.BlockSpec((kd, tn), lambda i, j: (0, j)),
                  pl.BlockSpec((tm, tn), lambda i, j: (i, j)), pl.BlockSpec((tm, tn), lambda i, j: (i, j))],
        out_specs=pl.BlockSpec((tm, tn), lambda i, j: (i, j)),
        out_shape=jax.ShapeDtypeStruct((m, n), F32),
        scratch_shapes=[pltpu.VMEM((tm, 1), F32)],
        compiler_params=_params("parallel", "arbitrary"),
        name="ple_gate",
    )(a, w, e, h)


def _layer(h, p, layer, batch, seq, norm_mix_g, w_in, fox_f_bias, fox_q_norm_g, fox_k_norm_g, hgrn_lb_logits,
           hgrn_norm_g, w_out, norm_mlp_g, w_up, w_down, ple_norm_g, w_ple_gate, w_ple_proj, ple_post_g):
    d = h.shape[1]
    d_a = d_b = d // 2
    n_a, n_b = d_a // HEAD_DIM, d_b // HEAD_DIM
    assert n_a <= HEAD_DIM and w_in.shape[1] == 3 * d_a + n_a + 4 * d_b

    w_in_t = jnp.swapaxes(w_in, 0, 1)
    w_fox = _cast_rows(w_in_t, 0, 3 * d_a)
    w_fa = _cast_rows(w_in_t, 3 * d_a, HEAD_DIM)
    q_gain = (fox_q_norm_g.astype(F32) * (LOG2E / math.sqrt(HEAD_DIM))).reshape(1, HEAD_DIM)
    k_gain = fox_k_norm_g.astype(F32).reshape(1, HEAD_DIM)

    u = _rmsnorm(h, norm_mix_g)
    proj_a, f_logit, w_hgrn = _inproj(u, w_fox, w_fa, w_in_t, 3 * d_a + n_a, 4 * d_b)
    proj_b, w_up_b, w_out_b = _matmul(u, w_hgrn, BF16, w_transposed=True, side=(w_up, w_out), name="inproj_hgrn")

    f_bias = jnp.pad(fox_f_bias.astype(F32), (0, HEAD_DIM - n_a)).reshape(1, HEAD_DIM)
    kbias = _forget_bias_tiles(f_logit, f_bias, batch, seq, n_a)
    out_a = _fox_attention(proj_a, kbias, q_gain, k_gain, batch, seq, n_a)

    out_b = _hgrn2(proj_b, hgrn_lb_logits.astype(F32), hgrn_norm_g.astype(F32).reshape(1, HEAD_DIM), layer,
                   batch, seq, n_b, col0=0)

    h, hg, part = _outproj(out_a, out_b, w_out_b, h, norm_mlp_g)
    hid, w_down_b, w_gate_g = _matmul(hg, w_up_b, BF16, act="relu2", part=part,
                                      side=(w_down, (w_ple_gate, ple_norm_g)),
                                      vmem_limit=V7X_VMEM_LIMIT_HIGH, name="mlp_up")
    h, hb = _matmul_residual(hid, w_down_b, h)

    e = _ple_embed(p, w_ple_proj.astype(BF16), ple_post_g)
    return _ple_gate(hb, w_gate_g, e, h)


def kernel(x, p, norm_mix_g, w_in, fox_f_bias, fox_q_norm_g, fox_k_norm_g, hgrn_lb_logits, hgrn_norm_g, w_out,
           norm_mlp_g, w_up, w_down, ple_norm_g, w_ple_gate, w_ple_proj, ple_post_g):
    batch, seq, d = x.shape
    h = x.reshape(batch * seq, d)
    for i in range(w_in.shape[0]):
        h = _layer(h, p[i].reshape(batch * seq, -1), i, batch, seq, norm_mix_g[i], w_in[i], fox_f_bias[i],
                   fox_q_norm_g[i], fox_k_norm_g[i], hgrn_lb_logits, hgrn_norm_g[i], w_out[i], norm_mlp_g[i],
                   w_up[i], w_down[i], ple_norm_g[i], w_ple_gate[i], w_ple_proj[i], ple_post_g[i])
    return h.reshape(batch, seq, d)
```

```python
import functools
import math

import jax
import jax.numpy as jnp
from jax import lax
from jax.experimental import pallas as pl
from jax.experimental.pallas import tpu as pltpu

F32, BF16 = jnp.float32, jnp.bfloat16
HEAD_DIM = 128
RMS_EPS = 1e-6
LOG2E = math.log2(math.e)
HGRN_CHUNK = 128
HGRN_SUB = 8
V7X_VMEM_LIMIT = 56 * 1024 * 1024
V7X_VMEM_LIMIT_HIGH = 60 * 1024 * 1024
V7X_VMEM_LIMIT_MAX = 63 * 1024 * 1024

NT_DIMS = (((1,), (1,)), ((), ()))
TN_DIMS = (((0,), (0,)), ((), ()))


def _params(*semantics, vmem_limit=V7X_VMEM_LIMIT):
    return pltpu.CompilerParams(dimension_semantics=semantics, vmem_limit_bytes=vmem_limit)


def _tile(dim, pref):
    t = min(dim, pref)
    assert dim % t == 0, (dim, pref)
    return t


def _sigmoid(x):
    return 1.0 / (1.0 + jnp.exp(-x))


def _rmsnorm_kernel(x_ref, g_ref, o_ref):
    x = x_ref[...]
    ms = jnp.mean(x * x, axis=-1, keepdims=True)
    o_ref[...] = (x * lax.rsqrt(ms + RMS_EPS) * g_ref[...]).astype(o_ref.dtype)


def _rmsnorm(x, g, out_dtype=BF16):
    m, d = x.shape
    tm = _tile(m, 256)
    return pl.pallas_call(
        _rmsnorm_kernel,
        grid=(m // tm,),
        in_specs=[pl.BlockSpec((tm, d), lambda i: (i, 0)), pl.BlockSpec((1, d), lambda i: (0, 0))],
        out_specs=pl.BlockSpec((tm, d), lambda i: (i, 0)),
        out_shape=jax.ShapeDtypeStruct((m, d), out_dtype),
        compiler_params=_params("parallel"),
        name="rmsnorm",
    )(x, g.reshape(1, d).astype(F32))


def _row_scale(part_ref, d):
    return lax.rsqrt(jnp.sum(part_ref[...], axis=-1, keepdims=True) * (1.0 / d) + RMS_EPS)


def _fold_sq(x):
    sq = x * x
    out = sq[:, :HEAD_DIM]
    for t in range(1, x.shape[1] // HEAD_DIM):
        out = out + sq[:, t * HEAD_DIM:(t + 1) * HEAD_DIM]
    return out


def _mm_kernel(*refs, act, w_dims, scaled, side_scaled):
    n_side, n_scale = len(side_scaled), sum(side_scaled)
    a_ref, w_ref = refs[:2]
    side_in = refs[2 + scaled:2 + scaled + n_side]
    scales = iter(refs[2 + scaled + n_side:2 + scaled + n_side + n_scale])
    o_ref = refs[2 + scaled + n_side + n_scale]
    side_out = refs[3 + scaled + n_side + n_scale:3 + scaled + 2 * n_side + n_scale]
    if scaled:
        part_ref, rs_ref = refs[2], refs[-1]

        @pl.when(pl.program_id(1) == 0)
        def _():
            rs_ref[...] = _row_scale(part_ref, a_ref.shape[1])

    for src, dst, has_scale in zip(side_in, side_out, side_scaled):
        slab = src[...] * next(scales)[...] if has_scale else src[...]
        dst[...] = slab.astype(dst.dtype)
    acc = lax.dot_general(a_ref[...], w_ref[...], w_dims, preferred_element_type=F32)
    if scaled:
        acc = acc * rs_ref[...]
    if act == "relu2":
        acc = jnp.square(jnp.maximum(acc, 0.0))
    o_ref[...] = acc.astype(o_ref.dtype)


def _matmul(a, w, out_dtype, act=None, part=None, w_transposed=False, side=(), tm=1024, tn=1024,
            vmem_limit=V7X_VMEM_LIMIT, name="matmul"):
    m, k = a.shape
    n = w.shape[0] if w_transposed else w.shape[1]
    tm, tn = _tile(m, tm), _tile(n, tn)
    ni, nj = m // tm, n // tn
    w_spec = pl.BlockSpec((tn, k), lambda i, j: (j, 0)) if w_transposed else pl.BlockSpec((k, tn), lambda i, j: (0, j))
    in_specs, operands = [pl.BlockSpec((tm, k), lambda i, j: (i, 0)), w_spec], [a, w]
    scratch = []
    if part is not None:
        in_specs.append(pl.BlockSpec((tm, part.shape[1]), lambda i, j: (i, 0)))
        operands.append(part)
        scratch.append(pltpu.VMEM((tm, 1), F32))
    out_specs = [pl.BlockSpec((tm, tn), lambda i, j: (i, j))]
    out_shape = [jax.ShapeDtypeStruct((m, n), out_dtype)]
    side = [sw if isinstance(sw, tuple) else (sw, None) for sw in side]
    scale_specs, scale_operands = [], []
    for sw, row_scale in side:
        rows = sw.shape[0] // (ni * nj)
        assert rows * ni * nj == sw.shape[0]
        slab = pl.BlockSpec((rows, sw.shape[1]), lambda i, j: (i * nj + j, 0))
        in_specs.append(slab)
        operands.append(sw)
        out_specs.append(slab)
        out_shape.append(jax.ShapeDtypeStruct(sw.shape, BF16))
        if row_scale is not None:
            scale_specs.append(pl.BlockSpec((rows, 1), lambda i, j: (i * nj + j, 0)))
            scale_operands.append(row_scale.astype(F32).reshape(-1, 1))
    in_specs += scale_specs
    operands += scale_operands
    w_dims = NT_DIMS if w_transposed else (((1,), (0,)), ((), ()))
    outs = pl.pallas_call(
        functools.partial(_mm_kernel, act=act, w_dims=w_dims, scaled=part is not None,
                          side_scaled=tuple(rs is not None for _, rs in side)),
        grid=(ni, nj),
        in_specs=in_specs,
        out_specs=out_specs,
        out_shape=out_shape,
        scratch_shapes=scratch,
        compiler_params=_params("parallel", "arbitrary", vmem_limit=vmem_limit),
        name=name,
    )(*operands)
    return outs if side else outs[0]


def _cast_rows_kernel(w_ref, o_ref):
    o_ref[...] = w_ref[...].astype(o_ref.dtype)


def _cast_rows(w, row0, nrows, tr=512):
    k = w.shape[1]
    tr = math.gcd(nrows, tr)
    align = math.gcd(math.gcd(row0, tr), 64)
    return pl.pallas_call(
        _cast_rows_kernel,
        grid=(nrows // tr,),
        in_specs=[pl.BlockSpec((pl.Element(tr), pl.Element(k)),
                               lambda i: (pl.multiple_of(row0 + i * tr, align), 0))],
        out_specs=pl.BlockSpec((tr, k), lambda i: (i, 0)),
        out_shape=jax.ShapeDtypeStruct((nrows, k), BF16),
        compiler_params=_params("parallel"),
        name="cast_weight_rows",
    )(w)


def _inproj_kernel(u_ref, wt_ref, wft_ref, side_ref, o_ref, f_ref, side_out_ref):
    @pl.when(pl.program_id(1) == 0)
    def _():
        f_ref[...] = lax.dot_general(u_ref[...], wft_ref[...], NT_DIMS, preferred_element_type=F32)

    side_out_ref[...] = side_ref[...].astype(side_out_ref.dtype)
    o_ref[...] = lax.dot_general(u_ref[...], wt_ref[...], NT_DIMS, preferred_element_type=F32).astype(o_ref.dtype)


def _inproj(u, wt, wft, side_src, side_row0, side_rows, tm=1024, tn=768):
    m, k = u.shape
    n, nf = wt.shape[0], wft.shape[0]
    tm, tn = _tile(m, tm), math.gcd(n, tn)
    ni, nj = m // tm, n // tn
    slab = side_rows // (ni * nj)
    assert slab * ni * nj == side_rows
    align = math.gcd(math.gcd(side_row0, slab), 64)
    return pl.pallas_call(
        _inproj_kernel,
        grid=(ni, nj),
        in_specs=[pl.BlockSpec((tm, k), lambda i, j: (i, 0)), pl.BlockSpec((tn, k), lambda i, j: (j, 0)),
                  pl.BlockSpec((nf, k), lambda i, j: (0, 0)),
                  pl.BlockSpec((pl.Element(slab), pl.Element(side_src.shape[1])),
                               lambda i, j: (pl.multiple_of(side_row0 + (i * nj + j) * slab, align), 0))],
        out_specs=[pl.BlockSpec((tm, tn), lambda i, j: (i, j)), pl.BlockSpec((tm, nf), lambda i, j: (i, 0)),
                   pl.BlockSpec((slab, side_src.shape[1]), lambda i, j: (i * nj + j, 0))],
        out_shape=[jax.ShapeDtypeStruct((m, n), BF16), jax.ShapeDtypeStruct((m, nf), F32),
                   jax.ShapeDtypeStruct((side_rows, side_src.shape[1]), BF16)],
        compiler_params=_params("parallel", "arbitrary"),
        name="inproj_fox",
    )(u, wt, wft, side_src)


def _outproj_kernel(a_ref, b_ref, wa_ref, wb_ref, x_ref, g_ref, o_ref, ob_ref, part_ref):
    acc = jnp.dot(a_ref[...], wa_ref[...], preferred_element_type=F32)
    acc = acc + jnp.dot(b_ref[...], wb_ref[...], preferred_element_type=F32)
    h = x_ref[...] + acc
    o_ref[...] = h
    ob_ref[...] = (h * g_ref[...]).astype(ob_ref.dtype)
    part_ref[...] = _fold_sq(h)


def _outproj(a, b, w, x, g, tm=1024, tn=1024):
    m, ka = a.shape
    kb = b.shape[1]
    n = w.shape[1]
    assert ka == kb and w.shape[0] == ka + kb
    tm, tn = _tile(m, tm), _tile(n, tn)
    tile = pl.BlockSpec((tm, tn), lambda i, j: (i, j))
    return pl.pallas_call(
        _outproj_kernel,
        grid=(m // tm, n // tn),
        in_specs=[pl.BlockSpec((tm, ka), lambda i, j: (i, 0)), pl.BlockSpec((tm, kb), lambda i, j: (i, 0)),
                  pl.BlockSpec((ka, tn), lambda i, j: (0, j)), pl.BlockSpec((kb, tn), lambda i, j: (1, j)), tile,
                  pl.BlockSpec((1, tn), lambda i, j: (0, j))],
        out_specs=[tile, tile, pl.BlockSpec((tm, HEAD_DIM), lambda i, j: (i, j))],
        out_shape=[jax.ShapeDtypeStruct((m, n), F32), jax.ShapeDtypeStruct((m, n), BF16),
                   jax.ShapeDtypeStruct((m, n // tn * HEAD_DIM), F32)],
        compiler_params=_params("parallel", "arbitrary", vmem_limit=V7X_VMEM_LIMIT_MAX),
        name="outproj",
    )(a, b, w, w, x, g.reshape(1, n).astype(F32))


def _mm_res_kernel(a_ref, w_ref, r_ref, o_ref, ob_ref):
    k = pl.program_id(2)

    @pl.when(k == 0)
    def _():
        o_ref[...] = r_ref[...]

    o_ref[...] += jnp.dot(a_ref[...], w_ref[...], preferred_element_type=F32)

    @pl.when(k == pl.num_programs(2) - 1)
    def _():
        ob_ref[...] = o_ref[...].astype(ob_ref.dtype)


def _matmul_residual(a, w, r, tm=1024, tn=1024, tk=4096):
    m, kd = a.shape
    n = w.shape[1]
    tm, tn, tk = _tile(m, tm), _tile(n, tn), _tile(kd, tk)
    return pl.pallas_call(
        _mm_res_kernel,
        grid=(m // tm, n // tn, kd // tk),
        in_specs=[pl.BlockSpec((tm, tk), lambda i, j, k: (i, k)), pl.BlockSpec((tk, tn), lambda i, j, k: (k, j)),
                  pl.BlockSpec((tm, tn), lambda i, j, k: (i, j))],
        out_specs=[pl.BlockSpec((tm, tn), lambda i, j, k: (i, j)), pl.BlockSpec((tm, tn), lambda i, j, k: (i, j))],
        out_shape=[jax.ShapeDtypeStruct((m, n), F32), jax.ShapeDtypeStruct((m, n), BF16)],
        compiler_params=_params("parallel", "parallel", "arbitrary", vmem_limit=V7X_VMEM_LIMIT_MAX),
        name="mlp_down",
    )(a, w, r)


def _split3(x):
    hi = x.astype(BF16).astype(F32)
    r = x - hi
    mid = r.astype(BF16).astype(F32)
    return hi, mid, r - mid


def _fcum_kernel(f_ref, bias_ref, o_ref, carry_ref, *, n_heads):
    t = pl.program_id(1)

    @pl.when(t == 0)
    def _():
        carry_ref[...] = jnp.zeros_like(carry_ref)

    x = f_ref[...] + bias_ref[...]
    ls = jnp.minimum(x, 0.0) - jnp.log1p(jnp.exp(-jnp.abs(x)))
    tc = x.shape[0]
    r = lax.broadcasted_iota(jnp.int32, (tc, tc), 0)
    c = lax.broadcasted_iota(jnp.int32, (tc, tc), 1)
    tri = jnp.where(c <= r, 1.0, 0.0).astype(BF16)
    c3 = jnp.dot(tri, jnp.concatenate(_split3(ls), axis=1).astype(BF16), preferred_element_type=F32)
    cs = c3[:, :HEAD_DIM] + c3[:, HEAD_DIM:2 * HEAD_DIM] + c3[:, 2 * HEAD_DIM:] + carry_ref[...]
    carry_ref[...] = cs[tc - 1:tc, :]
    lane = lax.broadcasted_iota(jnp.int32, (tc, HEAD_DIM), 1)
    for h in range(n_heads):
        hi, mid, lo = _split3(jnp.broadcast_to(cs[:, h:h + 1] * (-LOG2E), (tc, HEAD_DIM)))
        tile = jnp.where(lane == 0, hi, jnp.where(lane == 1, mid, jnp.where(lane == 2, lo, 0.0)))
        o_ref[:, h * HEAD_DIM:(h + 1) * HEAD_DIM] = tile.astype(o_ref.dtype)


def _forget_bias_tiles(f_logit, bias, batch, seq, n_heads):
    tc = _tile(seq, 256)
    nt = seq // tc
    return pl.pallas_call(
        functools.partial(_fcum_kernel, n_heads=n_heads),
        grid=(batch, nt),
        in_specs=[pl.BlockSpec((tc, HEAD_DIM), lambda b, t: (b * nt + t, 0)),
                  pl.BlockSpec((1, HEAD_DIM), lambda b, t: (0, 0))],
        out_specs=pl.BlockSpec((tc, n_heads * HEAD_DIM), lambda b, t: (b * nt + t, 0)),
        out_shape=jax.ShapeDtypeStruct((batch * seq, n_heads * HEAD_DIM), BF16),
        scratch_shapes=[pltpu.VMEM((1, HEAD_DIM), F32)],
        compiler_params=_params("parallel", "arbitrary"),
        name="fox_forget_cumsum",
    )(f_logit, bias)


def _head_rmsnorm(x, g):
    x = x.astype(F32)
    ms = jnp.mean(x * x, axis=-1, keepdims=True)
    return (x * lax.rsqrt(ms + RMS_EPS) * g).astype(BF16)


def _attn_kernel(q_ref, k_ref, kb_ref, v_ref, gq_ref, gk_ref, o_ref, kp_ref, s_ref, m_ref, l_ref, acc_ref, *,
                 blk, heads):
    qi = pl.program_id(2)
    hcols = [slice(hh * HEAD_DIM, (hh + 1) * HEAD_DIM) for hh in range(heads)]
    seq = k_ref.shape[0]

    @pl.when(qi == 0)
    def _():
        def fill(c, carry):
            rows = pl.ds(pl.multiple_of(c * blk, blk), blk)
            for hh, cols in enumerate(hcols):
                kp_ref[hh, rows, :HEAD_DIM] = _head_rmsnorm(k_ref[rows, cols], gk_ref[...])
                kp_ref[hh, rows, HEAD_DIM:] = kb_ref[rows, cols]
            return carry
        lax.fori_loop(0, seq // blk, fill, 0)

    lane = lax.broadcasted_iota(jnp.int32, (blk, HEAD_DIM), 1)
    q_ones = jnp.where(lane < 3, 1.0, 0.0).astype(BF16)
    key_pos = lax.broadcasted_iota(jnp.int32, (blk, blk), 0)
    qry_pos = lax.broadcasted_iota(jnp.int32, (blk, blk), 1)

    q2 = [jnp.concatenate([_head_rmsnorm(q_ref[:, cols], gq_ref[...]), q_ones], axis=1) for cols in hcols]

    def scores(hh, j, slot):
        rows = pl.ds(pl.multiple_of(j * blk, blk), blk)
        s_ref[hh, slot] = lax.dot_general(kp_ref[hh, rows, :], q2[hh], NT_DIMS, preferred_element_type=F32)

    def update(hh, j, slot, diagonal=False):
        rows = pl.ds(pl.multiple_of(j * blk, blk), blk)
        s = s_ref[hh, slot]
        if diagonal:
            s = jnp.where(key_pos <= qry_pos, s, -jnp.inf)
            m_new = jnp.max(s, axis=0, keepdims=True)
        else:
            m = m_ref[hh]
            m_new = jnp.maximum(m, jnp.max(s, axis=0, keepdims=True))
        p = jnp.exp2(s - m_new)
        pv = lax.dot_general(v_ref[rows, hcols[hh]], p.astype(BF16), TN_DIMS, preferred_element_type=F32)
        m_ref[hh] = m_new
        if diagonal:
            l_ref[hh] = jnp.sum(p, axis=0, keepdims=True)
            acc_ref[hh] = pv
        else:
            alpha = jnp.exp2(m - m_new)
            l_ref[hh] = alpha * l_ref[hh] + jnp.sum(p, axis=0, keepdims=True)
            acc_ref[hh] = alpha * acc_ref[hh] + pv

    def each_head(fn, *args, **kwargs):
        for hh in range(heads):
            fn(hh, *args, **kwargs)

    def pair(t0):
        each_head(scores, t0 + 1, 1)
        for hh in range(heads):
            update(hh, t0, 0)
            scores(hh, t0 + 2, 0)
        each_head(update, t0 + 1, 1)

    def quad(k, carry):
        pair(4 * k)
        pair(4 * k + 2)
        return carry

    each_head(scores, qi, 1)
    each_head(scores, 0, 0)
    each_head(update, qi, 1, diagonal=True)
    lax.fori_loop(0, qi // 4, quad, 0)

    @pl.when((qi // 2) % 2 == 1)
    def _():
        pair(4 * (qi // 4))

    @pl.when(qi % 2 == 1)
    def _():
        each_head(update, qi - 1, 0)

    for hh, cols in enumerate(hcols):
        o_ref[:, cols] = (acc_ref[hh] / l_ref[hh]).T.astype(o_ref.dtype)


def _fox_attention(proj, kbias, gq, gk, batch, seq, n_heads, blk=512, heads=2):
    blk = _tile(seq, blk)
    heads = _tile(n_heads, heads)
    nq = seq // blk
    hblk = n_heads // heads
    w = heads * HEAD_DIM
    gain = pl.BlockSpec((1, HEAD_DIM), lambda b, h, i: (0, 0))
    return pl.pallas_call(
        functools.partial(_attn_kernel, blk=blk, heads=heads),
        grid=(batch, hblk, nq),
        in_specs=[pl.BlockSpec((blk, w), lambda b, h, i: (b * nq + i, h)),
                  pl.BlockSpec((seq, w), lambda b, h, i: (b, hblk + h)),
                  pl.BlockSpec((seq, w), lambda b, h, i: (b, h)),
                  pl.BlockSpec((seq, w), lambda b, h, i: (b, 2 * hblk + h)), gain, gain],
        out_specs=pl.BlockSpec((blk, w), lambda b, h, i: (b * nq + i, h)),
        out_shape=jax.ShapeDtypeStruct((batch * seq, n_heads * HEAD_DIM), BF16),
        scratch_shapes=[pltpu.VMEM((heads, seq, 2 * HEAD_DIM), BF16), pltpu.VMEM((heads, 2, blk, blk), F32),
                        pltpu.VMEM((heads, 1, blk), F32), pltpu.VMEM((heads, 1, blk), F32),
                        pltpu.VMEM((heads, HEAD_DIM, blk), F32)],
        compiler_params=_params("parallel", "parallel", "arbitrary"),
        name="fox_attention",
    )(proj, proj, kbias, proj, gq, gk)


def _hgrn_kernel(q_ref, z_ref, v_ref, gate_ref, lbl_ref, gn_ref, o_ref, st_ref, rb_ref, *, layer, heads, chunks):
    C, c = HGRN_CHUNK, HGRN_SUB
    t = pl.program_id(2)

    @pl.when(t == 0)
    def _():
        st_ref[...] = jnp.zeros_like(st_ref)

    logits = lbl_ref[...]
    e = jnp.exp(logits - jnp.max(logits, axis=0, keepdims=True))
    lb_all = jnp.sum(e[:layer + 1], axis=0, keepdims=True) / jnp.sum(e, axis=0, keepdims=True)

    row = lax.broadcasted_iota(jnp.int32, (C, C), 0)
    col = lax.broadcasted_iota(jnp.int32, (C, C), 1)
    ltri = jnp.where(col <= row, 1.0, 0.0).astype(BF16)
    sizes = [c << l for l in range(1, (C // c).bit_length())]
    xor = row ^ col
    level = jnp.zeros((C, C), jnp.int32)
    for l, m in enumerate(sizes):
        level = jnp.where(xor >= m // 2, l + 1, level)
    level = jnp.where(col > row, -1, level)
    lane = lax.broadcasted_iota(jnp.int32, (c, C), 1)

    def chunk_body(ci, carry):
        r0 = pl.multiple_of(ci * C, C)
        hs = range(heads)
        cols = [slice(hh * HEAD_DIM, (hh + 1) * HEAD_DIM) for hh in hs]
        qs, kk, b, v, o, a = {}, {}, {}, {}, {}, {}
        for hh in hs:
            lb = lb_all[:, cols[hh]]
            q = q_ref[pl.ds(r0, C), cols[hh]].astype(F32)
            z = z_ref[pl.ds(r0, C), cols[hh]].astype(F32)
            v[hh] = v_ref[pl.ds(r0, C), cols[hh]]
            ez = jnp.exp(-jnp.abs(z))
            inv = 1.0 / (1.0 + ez)
            sig_z = jnp.where(z >= 0, 1.0, ez) * inv
            sig_mz = jnp.where(z >= 0, ez, 1.0) * inv
            kk[hh] = (1.0 - lb) * sig_mz
            qs[hh] = q * _sigmoid(q)
            log2f = jnp.concatenate(_split3(jnp.log2(lb + (1.0 - lb) * sig_z)), axis=1).astype(BF16)
            cs = jnp.dot(ltri, log2f, preferred_element_type=F32)
            b[hh] = cs[:, :HEAD_DIM] + cs[:, HEAD_DIM:2 * HEAD_DIM] + cs[:, 2 * HEAD_DIM:]
            rb_ref[2 * hh] = kk[hh]
            rb_ref[2 * hh + 1] = b[hh]

        for hh in hs:
            o[hh] = lax.dot_general((qs[hh] * jnp.exp2(b[hh])).astype(BF16), st_ref[hh].astype(BF16), NT_DIMS,
                                    preferred_element_type=F32)

        blocks = {hh: [] for hh in hs}
        for i in range(C // c):
            rows = slice(i * c, (i + 1) * c)
            blk = {hh: jnp.zeros((c, C), F32) for hh in hs}
            for s in range(i * c, (i + 1) * c):
                at_s = lane == s
                for hh in hs:
                    w = qs[hh][rows] * rb_ref[2 * hh, s:s + 1, :] * jnp.exp2(b[hh][rows] - rb_ref[2 * hh + 1, s:s + 1, :])
                    blk[hh] = jnp.where(at_s, jnp.sum(w, axis=-1, keepdims=True), blk[hh])
            for hh in hs:
                blocks[hh].append(blk[hh])
        in_sub = level == 0
        for hh in hs:
            a[hh] = jnp.where(in_sub, jnp.concatenate(blocks[hh], axis=0), 0.0)

        for l, m in enumerate(sizes):
            at_level = level == l + 1
            for hh in hs:
                bh = b[hh]
                edge = jnp.concatenate(
                    [jnp.broadcast_to(bh[j * m + m // 2 - 1:j * m + m // 2, :], (m, HEAD_DIM)) for j in range(C // m)],
                    axis=0)
                x = jnp.exp2(-jnp.abs(bh - edge))
                pair = lax.dot_general((qs[hh] * x).astype(BF16), (kk[hh] * x).astype(BF16), NT_DIMS,
                                       preferred_element_type=F32)
                a[hh] = jnp.where(at_level, pair, a[hh])

        for hh in hs:
            o[hh] = o[hh] + jnp.dot(a[hh].astype(BF16), v[hh], preferred_element_type=F32)
            b_last = b[hh][C - 1:C, :]
            k_d = (kk[hh] * jnp.exp2(b_last - b[hh])).astype(BF16)
            st_ref[hh] = st_ref[hh] * jnp.exp2(b_last) + lax.dot_general(v[hh], k_d, TN_DIMS,
                                                                         preferred_element_type=F32)

        for hh in hs:
            gate = gate_ref[pl.ds(r0, C), cols[hh]].astype(F32)
            ms = jnp.mean(o[hh] * o[hh], axis=-1, keepdims=True)
            on = o[hh] * lax.rsqrt(ms + RMS_EPS) * gn_ref[...]
            o_ref[pl.ds(r0, C), cols[hh]] = (on * (gate * _sigmoid(gate))).astype(o_ref.dtype)
        return carry

    lax.fori_loop(0, chunks, chunk_body, 0)


def _hgrn2(proj, lb_logits, gn, layer, batch, seq, n_heads, col0, heads=4, rows=512):
    heads = _tile(n_heads, heads)
    rows = _tile(seq, rows)
    assert rows % HGRN_CHUNK == 0 and col0 % heads == 0
    nt = seq // rows
    w = heads * HEAD_DIM
    hblk = n_heads // heads

    def in_spec(group):
        return pl.BlockSpec((rows, w), lambda b, h, t: (b * nt + t, col0 // heads + group * hblk + h))

    return pl.pallas_call(
        functools.partial(_hgrn_kernel, layer=layer, heads=heads, chunks=rows // HGRN_CHUNK),
        grid=(batch, hblk, nt),
        in_specs=[in_spec(0), in_spec(1), in_spec(2), in_spec(3),
                  pl.BlockSpec((lb_logits.shape[0], w), lambda b, h, t: (0, h)),
                  pl.BlockSpec((1, HEAD_DIM), lambda b, h, t: (0, 0))],
        out_specs=pl.BlockSpec((rows, w), lambda b, h, t: (b * nt + t, h)),
        out_shape=jax.ShapeDtypeStruct((batch * seq, n_heads * HEAD_DIM), BF16),
        scratch_shapes=[pltpu.VMEM((heads, HEAD_DIM, HEAD_DIM), F32),
                        pltpu.VMEM((2 * heads, HGRN_CHUNK, HEAD_DIM), F32)],
        compiler_params=_params("parallel", "parallel", "arbitrary"),
        name="hgrn2",
    )(proj, proj, proj, proj, lb_logits, gn)


def _ple_embed_kernel(p_ref, w_ref, g_ref, o_ref):
    e = jnp.dot(p_ref[...].astype(BF16), w_ref[...], preferred_element_type=F32)
    ms = jnp.mean(e * e, axis=-1, keepdims=True)
    o_ref[...] = (e * lax.rsqrt(ms + RMS_EPS) * g_ref[...]).astype(o_ref.dtype)


def _ple_embed(p, w, g):
    m, kd = p.shape
    n = w.shape[1]
    tm = _tile(m, 256)
    return pl.pallas_call(
        _ple_embed_kernel,
        grid=(m // tm,),
        in_specs=[pl.BlockSpec((tm, kd), lambda i: (i, 0)), pl.BlockSpec((kd, n), lambda i: (0, 0)),
                  pl.BlockSpec((1, n), lambda i: (0, 0))],
        out_specs=pl.BlockSpec((tm, n), lambda i: (i, 0)),
        out_shape=jax.ShapeDtypeStruct((m, n), BF16),
        compiler_params=_params("parallel"),
        name="ple_embed",
    )(p, w, g.reshape(1, n).astype(F32))


def _ple_gate_kernel(a_ref, w_ref, e_ref, h_ref, o_ref, rs_ref):
    @pl.when(pl.program_id(1) == 0)
    def _():
        a = a_ref[...].astype(F32)
        rs_ref[...] = lax.rsqrt(jnp.mean(a * a, axis=-1, keepdims=True) + RMS_EPS)

    gate = _sigmoid(jnp.dot(a_ref[...], w_ref[...], preferred_element_type=F32) * rs_ref[...])
    o_ref[...] = h_ref[...] + gate * e_ref[...]


def _ple_gate(a, w, e, h, tm=1024, tn=1024):
    m, kd = a.shape
    n = w.shape[1]
    tm, tn = _tile(m, tm), _tile(n, tn)
    return pl.pallas_call(
        _ple_gate_kernel,
        grid=(m // tm, n // tn),
        in_specs=[pl.BlockSpec((tm, kd), lambda i, j: (i, 0)), pl.BlockSpec((kd, tn), lambda i, j: (0, j)),
                  pl.BlockSpec((tm, tn), lambda i, j: (i, j)), pl.BlockSpec((tm, tn), lambda i, j: (i, j))],
        out_specs=pl.BlockSpec((tm, tn), lambda i, j: (i, j)),
        out_shape=jax.ShapeDtypeStruct((m, n), F32),
        scratch_shapes=[pltpu.VMEM((tm, 1), F32)],
        compiler_params=_params("parallel", "arbitrary", vmem_limit=V7X_VMEM_LIMIT_MAX),
        name="ple_gate",
    )(a, w, e, h)


def _layer(h, p, layer, batch, seq, norm_mix_g, w_in, fox_f_bias, fox_q_norm_g, fox_k_norm_g, hgrn_lb_logits,
           hgrn_norm_g, w_out, norm_mlp_g, w_up, w_down, ple_norm_g, w_ple_gate, w_ple_proj, ple_post_g):
    d = h.shape[1]
    d_a = d_b = d // 2
    n_a, n_b = d_a // HEAD_DIM, d_b // HEAD_DIM
    assert n_a <= HEAD_DIM and w_in.shape[1] == 3 * d_a + n_a + 4 * d_b

    w_in_t = jnp.swapaxes(w_in, 0, 1)
    w_fox = _cast_rows(w_in_t, 0, 3 * d_a)
    w_fa = _cast_rows(w_in_t, 3 * d_a, HEAD_DIM)
    q_gain = (fox_q_norm_g.astype(F32) * (LOG2E / math.sqrt(HEAD_DIM))).reshape(1, HEAD_DIM)
    k_gain = fox_k_norm_g.astype(F32).reshape(1, HEAD_DIM)

    u = _rmsnorm(h, norm_mix_g)
    proj_a, f_logit, w_hgrn = _inproj(u, w_fox, w_fa, w_in_t, 3 * d_a + n_a, 4 * d_b)
    proj_b, w_up_b, w_out_b = _matmul(u, w_hgrn, BF16, w_transposed=True, side=(w_up, w_out), name="inproj_hgrn")

    f_bias = jnp.pad(fox_f_bias.astype(F32), (0, HEAD_DIM - n_a)).reshape(1, HEAD_DIM)
    kbias = _forget_bias_tiles(f_logit, f_bias, batch, seq, n_a)
    out_a = _fox_attention(proj_a, kbias, q_gain, k_gain, batch, seq, n_a)

    out_b = _hgrn2(proj_b, hgrn_lb_logits.astype(F32), hgrn_norm_g.astype(F32).reshape(1, HEAD_DIM), layer,
                   batch, seq, n_b, col0=0)

    h, hg, part = _outproj(out_a, out_b, w_out_b, h, norm_mlp_g)
    hid, w_down_b, w_gate_g = _matmul(hg, w_up_b, BF16, act="relu2", part=part,
                                      side=(w_down, (w_ple_gate, ple_norm_g)),
                                      vmem_limit=V7X_VMEM_LIMIT_HIGH, name="mlp_up")
    h, hb = _matmul_residual(hid, w_down_b, h)

    e = _ple_embed(p, w_ple_proj.astype(BF16), ple_post_g)
    return _ple_gate(hb, w_gate_g, e, h)


def kernel(x, p, norm_mix_g, w_in, fox_f_bias, fox_q_norm_g, fox_k_norm_g, hgrn_lb_logits, hgrn_norm_g, w_out,
           norm_mlp_g, w_up, w_down, ple_norm_g, w_ple_gate, w_ple_proj, ple_post_g):
    batch, seq, d = x.shape
    h = x.reshape(batch * seq, d)
    for i in range(w_in.shape[0]):
        h = _layer(h, p[i].reshape(batch * seq, -1), i, batch, seq, norm_mix_g[i], w_in[i], fox_f_bias[i],
                   fox_q_norm_g[i], fox_k_norm_g[i], hgrn_lb_logits, hgrn_norm_g[i], w_out[i], norm_mlp_g[i],
                   w_up[i], w_down[i], ple_norm_g[i], w_ple_gate[i], w_ple_proj[i], ple_post_g[i])
    return h.reshape(batch, seq, d)
```

```python
import functools
import math

import jax
import jax.numpy as jnp
from jax import lax
from jax.experimental import pallas as pl
from jax.experimental.pallas import tpu as pltpu

F32, BF16 = jnp.float32, jnp.bfloat16
HEAD_DIM = 128
RMS_EPS = 1e-6
LOG2E = math.log2(math.e)
HGRN_CHUNK = 128
HGRN_SUB = 8
V7X_VMEM_LIMIT = 56 * 1024 * 1024
V7X_VMEM_LIMIT_HIGH = 60 * 1024 * 1024
V7X_VMEM_LIMIT_MAX = 63 * 1024 * 1024

NT_DIMS = (((1,), (1,)), ((), ()))
TN_DIMS = (((0,), (0,)), ((), ()))


def _params(*semantics, vmem_limit=V7X_VMEM_LIMIT):
    return pltpu.CompilerParams(dimension_semantics=semantics, vmem_limit_bytes=vmem_limit)


def _tile(dim, pref):
    t = min(dim, pref)
    assert dim % t == 0, (dim, pref)
    return t


def _sigmoid(x):
    return 1.0 / (1.0 + jnp.exp(-x))


def _rmsnorm_kernel(x_ref, g_ref, o_ref):
    x = x_ref[...]
    ms = jnp.mean(x * x, axis=-1, keepdims=True)
    o_ref[...] = (x * lax.rsqrt(ms + RMS_EPS) * g_ref[...]).astype(o_ref.dtype)


def _rmsnorm(x, g, out_dtype=BF16):
    m, d = x.shape
    tm = _tile(m, 256)
    return pl.pallas_call(
        _rmsnorm_kernel,
        grid=(m // tm,),
        in_specs=[pl.BlockSpec((tm, d), lambda i: (i, 0)), pl.BlockSpec((1, d), lambda i: (0, 0))],
        out_specs=pl.BlockSpec((tm, d), lambda i: (i, 0)),
        out_shape=jax.ShapeDtypeStruct((m, d), out_dtype),
        compiler_params=_params("parallel"),
        name="rmsnorm",
    )(x, g.reshape(1, d).astype(F32))


def _row_scale(part_ref, d):
    return lax.rsqrt(jnp.sum(part_ref[...], axis=-1, keepdims=True) * (1.0 / d) + RMS_EPS)


def _fold_sq(x):
    sq = x * x
    out = sq[:, :HEAD_DIM]
    for t in range(1, x.shape[1] // HEAD_DIM):
        out = out + sq[:, t * HEAD_DIM:(t + 1) * HEAD_DIM]
    return out


def _mm_kernel(*refs, act, w_dims, scaled, side_scaled):
    n_side, n_scale = len(side_scaled), sum(side_scaled)
    a_ref, w_ref = refs[:2]
    side_in = refs[2 + scaled:2 + scaled + n_side]
    scales = iter(refs[2 + scaled + n_side:2 + scaled + n_side + n_scale])
    o_ref = refs[2 + scaled + n_side + n_scale]
    side_out = refs[3 + scaled + n_side + n_scale:3 + scaled + 2 * n_side + n_scale]
    if scaled:
        part_ref, rs_ref = refs[2], refs[-1]

        @pl.when(pl.program_id(1) == 0)
        def _():
            rs_ref[...] = _row_scale(part_ref, a_ref.shape[1])

    for src, dst, has_scale in zip(side_in, side_out, side_scaled):
        slab = src[...] * next(scales)[...] if has_scale else src[...]
        dst[...] = slab.astype(dst.dtype)
    acc = lax.dot_general(a_ref[...], w_ref[...], w_dims, preferred_element_type=F32)
    if scaled:
        acc = acc * rs_ref[...]
    if act == "relu2":
        acc = jnp.square(jnp.maximum(acc, 0.0))
    o_ref[...] = acc.astype(o_ref.dtype)


def _matmul(a, w, out_dtype, act=None, part=None, w_transposed=False, side=(), tm=1024, tn=1024,
            vmem_limit=V7X_VMEM_LIMIT, name="matmul"):
    m, k = a.shape
    n = w.shape[0] if w_transposed else w.shape[1]
    tm, tn = _tile(m, tm), _tile(n, tn)
    ni, nj = m // tm, n // tn
    w_spec = pl.BlockSpec((tn, k), lambda i, j: (j, 0)) if w_transposed else pl.BlockSpec((k, tn), lambda i, j: (0, j))
    in_specs, operands = [pl.BlockSpec((tm, k), lambda i, j: (i, 0)), w_spec], [a, w]
    scratch = []
    if part is not None:
        in_specs.append(pl.BlockSpec((tm, part.shape[1]), lambda i, j: (i, 0)))
        operands.append(part)
        scratch.append(pltpu.VMEM((tm, 1), F32))
    out_specs = [pl.BlockSpec((tm, tn), lambda i, j: (i, j))]
    out_shape = [jax.ShapeDtypeStruct((m, n), out_dtype)]
    side = [sw if isinstance(sw, tuple) else (sw, None) for sw in side]
    scale_specs, scale_operands = [], []
    for sw, row_scale in side:
        rows = sw.shape[0] // (ni * nj)
        assert rows * ni * nj == sw.shape[0]
        slab = pl.BlockSpec((rows, sw.shape[1]), lambda i, j: (i * nj + j, 0))
        in_specs.append(slab)
        operands.append(sw)
        out_specs.append(slab)
        out_shape.append(jax.ShapeDtypeStruct(sw.shape, BF16))
        if row_scale is not None:
            scale_specs.append(pl.BlockSpec((rows, 1), lambda i, j: (i * nj + j, 0)))
            scale_operands.append(row_scale.astype(F32).reshape(-1, 1))
    in_specs += scale_specs
    operands += scale_operands
    w_dims = NT_DIMS if w_transposed else (((1,), (0,)), ((), ()))
    outs = pl.pallas_call(
        functools.partial(_mm_kernel, act=act, w_dims=w_dims, scaled=part is not None,
                          side_scaled=tuple(rs is not None for _, rs in side)),
        grid=(ni, nj),
        in_specs=in_specs,
        out_specs=out_specs,
        out_shape=out_shape,
        scratch_shapes=scratch,
        compiler_params=_params("parallel", "arbitrary", vmem_limit=vmem_limit),
        name=name,
    )(*operands)
    return outs if side else outs[0]


def _cast_rows_kernel(w_ref, o_ref):
    o_ref[...] = w_ref[...].astype(o_ref.dtype)


def _cast_rows(w, row0, nrows, tr=512):
    k = w.shape[1]
    tr = math.gcd(nrows, tr)
    align = math.gcd(math.gcd(row0, tr), 64)
    return pl.pallas_call(
        _cast_rows_kernel,
        grid=(nrows // tr,),
        in_specs=[pl.BlockSpec((pl.Element(tr), pl.Element(k)),
                               lambda i: (pl.multiple_of(row0 + i * tr, align), 0))],
        out_specs=pl.BlockSpec((tr, k), lambda i: (i, 0)),
        out_shape=jax.ShapeDtypeStruct((nrows, k), BF16),
        compiler_params=_params("parallel"),
        name="cast_weight_rows",
    )(w)


def _inproj_kernel(u_ref, wt_ref, wft_ref, side_ref, o_ref, f_ref, side_out_ref):
    @pl.when(pl.program_id(1) == 0)
    def _():
        f_ref[...] = lax.dot_general(u_ref[...], wft_ref[...], NT_DIMS, preferred_element_type=F32)

    side_out_ref[...] = side_ref[...].astype(side_out_ref.dtype)
    o_ref[...] = lax.dot_general(u_ref[...], wt_ref[...], NT_DIMS, preferred_element_type=F32).astype(o_ref.dtype)


def _inproj(u, wt, wft, side_src, side_row0, side_rows, tm=1024, tn=768):
    m, k = u.shape
    n, nf = wt.shape[0], wft.shape[0]
    tm, tn = _tile(m, tm), math.gcd(n, tn)
    ni, nj = m // tm, n // tn
    slab = side_rows // (ni * nj)
    assert slab * ni * nj == side_rows
    align = math.gcd(math.gcd(side_row0, slab), 64)
    return pl.pallas_call(
        _inproj_kernel,
        grid=(ni, nj),
        in_specs=[pl.BlockSpec((tm, k), lambda i, j: (i, 0)), pl.BlockSpec((tn, k), lambda i, j: (j, 0)),
                  pl.BlockSpec((nf, k), lambda i, j: (0, 0)),
                  pl.BlockSpec((pl.Element(slab), pl.Element(side_src.shape[1])),
                               lambda i, j: (pl.multiple_of(side_row0 + (i * nj + j) * slab, align), 0))],
        out_specs=[pl.BlockSpec((tm, tn), lambda i, j: (i, j)), pl.BlockSpec((tm, nf), lambda i, j: (i, 0)),
                   pl.BlockSpec((slab, side_src.shape[1]), lambda i, j: (i * nj + j, 0))],
        out_shape=[jax.ShapeDtypeStruct((m, n), BF16), jax.ShapeDtypeStruct((m, nf), F32),
                   jax.ShapeDtypeStruct((side_rows, side_src.shape[1]), BF16)],
        compiler_params=_params("parallel", "arbitrary"),
        name="inproj_fox",
    )(u, wt, wft, side_src)


def _outproj_kernel(a_ref, b_ref, wa_ref, wb_ref, x_ref, g_ref, o_ref, ob_ref, part_ref):
    acc = jnp.dot(a_ref[...], wa_ref[...], preferred_element_type=F32)
    acc = acc + jnp.dot(b_ref[...], wb_ref[...], preferred_element_type=F32)
    h = x_ref[...] + acc
    o_ref[...] = h
    ob_ref[...] = (h * g_ref[...]).astype(ob_ref.dtype)
    part_ref[...] = _fold_sq(h)


def _outproj(a, b, w, x, g, tm=1024, tn=1024):
    m, ka = a.shape
    kb = b.shape[1]
    n = w.shape[1]
    assert ka == kb and w.shape[0] == ka + kb
    tm, tn = _tile(m, tm), _tile(n, tn)
    tile = pl.BlockSpec((tm, tn), lambda i, j: (i, j))
    return pl.pallas_call(
        _outproj_kernel,
        grid=(m // tm, n // tn),
        in_specs=[pl.BlockSpec((tm, ka), lambda i, j: (i, 0)), pl.BlockSpec((tm, kb), lambda i, j: (i, 0)),
                  pl.BlockSpec((ka, tn), lambda i, j: (0, j)), pl.BlockSpec((kb, tn), lambda i, j: (1, j)), tile,
                  pl.BlockSpec((1, tn), lambda i, j: (0, j))],
        out_specs=[tile, tile, pl.BlockSpec((tm, HEAD_DIM), lambda i, j: (i, j))],
        out_shape=[jax.ShapeDtypeStruct((m, n), F32), jax.ShapeDtypeStruct((m, n), BF16),
                   jax.ShapeDtypeStruct((m, n // tn * HEAD_DIM), F32)],
        compiler_params=_params("parallel", "arbitrary", vmem_limit=V7X_VMEM_LIMIT_MAX),
        name="outproj",
    )(a, b, w, w, x, g.reshape(1, n).astype(F32))


def _mm_res_kernel(a_ref, w_ref, r_ref, o_ref, ob_ref):
    k = pl.program_id(2)

    @pl.when(k == 0)
    def _():
        o_ref[...] = r_ref[...]

    o_ref[...] += jnp.dot(a_ref[...], w_ref[...], preferred_element_type=F32)

    @pl.when(k == pl.num_programs(2) - 1)
    def _():
        ob_ref[...] = o_ref[...].astype(ob_ref.dtype)


def _matmul_residual(a, w, r, tm=1024, tn=1024, tk=4096):
    m, kd = a.shape
    n = w.shape[1]
    tm, tn, tk = _tile(m, tm), _tile(n, tn), _tile(kd, tk)
    return pl.pallas_call(
        _mm_res_kernel,
        grid=(m // tm, n // tn, kd // tk),
        in_specs=[pl.BlockSpec((tm, tk), lambda i, j, k: (i, k)), pl.BlockSpec((tk, tn), lambda i, j, k: (k, j)),
                  pl.BlockSpec((tm, tn), lambda i, j, k: (i, j))],
        out_specs=[pl.BlockSpec((tm, tn), lambda i, j, k: (i, j)), pl.BlockSpec((tm, tn), lambda i, j, k: (i, j))],
        out_shape=[jax.ShapeDtypeStruct((m, n), F32), jax.ShapeDtypeStruct((m, n), BF16)],
        compiler_params=_params("parallel", "parallel", "arbitrary", vmem_limit=V7X_VMEM_LIMIT_MAX),
        name="mlp_down",
    )(a, w, r)


def _split3(x):
    hi = x.astype(BF16).astype(F32)
    r = x - hi
    mid = r.astype(BF16).astype(F32)
    return hi, mid, r - mid


def _fcum_kernel(f_ref, bias_ref, o_ref, carry_ref, *, n_heads):
    t = pl.program_id(1)

    @pl.when(t == 0)
    def _():
        carry_ref[...] = jnp.zeros_like(carry_ref)

    x = f_ref[...] + bias_ref[...]
    ls = jnp.minimum(x, 0.0) - jnp.log1p(jnp.exp(-jnp.abs(x)))
    tc = x.shape[0]
    r = lax.broadcasted_iota(jnp.int32, (tc, tc), 0)
    c = lax.broadcasted_iota(jnp.int32, (tc, tc), 1)
    tri = jnp.where(c <= r, 1.0, 0.0).astype(BF16)
    c3 = jnp.dot(tri, jnp.concatenate(_split3(ls), axis=1).astype(BF16), preferred_element_type=F32)
    cs = c3[:, :HEAD_DIM] + c3[:, HEAD_DIM:2 * HEAD_DIM] + c3[:, 2 * HEAD_DIM:] + carry_ref[...]
    carry_ref[...] = cs[tc - 1:tc, :]
    lane = lax.broadcasted_iota(jnp.int32, (tc, HEAD_DIM), 1)
    for h in range(n_heads):
        hi, mid, lo = _split3(jnp.broadcast_to(cs[:, h:h + 1] * (-LOG2E), (tc, HEAD_DIM)))
        tile = jnp.where(lane == 0, hi, jnp.where(lane == 1, mid, jnp.where(lane == 2, lo, 0.0)))
        o_ref[:, h * HEAD_DIM:(h + 1) * HEAD_DIM] = tile.astype(o_ref.dtype)


def _forget_bias_tiles(f_logit, bias, batch, seq, n_heads):
    tc = _tile(seq, 256)
    nt = seq // tc
    return pl.pallas_call(
        functools.partial(_fcum_kernel, n_heads=n_heads),
        grid=(batch, nt),
        in_specs=[pl.BlockSpec((tc, HEAD_DIM), lambda b, t: (b * nt + t, 0)),
                  pl.BlockSpec((1, HEAD_DIM), lambda b, t: (0, 0))],
        out_specs=pl.BlockSpec((tc, n_heads * HEAD_DIM), lambda b, t: (b * nt + t, 0)),
        out_shape=jax.ShapeDtypeStruct((batch * seq, n_heads * HEAD_DIM), BF16),
        scratch_shapes=[pltpu.VMEM((1, HEAD_DIM), F32)],
        compiler_params=_params("parallel", "arbitrary"),
        name="fox_forget_cumsum",
    )(f_logit, bias)


def _head_rmsnorm(x, g):
    x = x.astype(F32)
    ms = jnp.mean(x * x, axis=-1, keepdims=True)
    return (x * lax.rsqrt(ms + RMS_EPS) * g).astype(BF16)


def _attn_kernel(q_ref, k_ref, kb_ref, v_ref, gq_ref, gk_ref, o_ref, kp_ref, s_ref, m_ref, l_ref, acc_ref, *,
                 blk, heads):
    qi = pl.program_id(2)
    hcols = [slice(hh * HEAD_DIM, (hh + 1) * HEAD_DIM) for hh in range(heads)]
    seq = k_ref.shape[0]

    @pl.when(qi == 0)
    def _():
        def fill(c, carry):
            rows = pl.ds(pl.multiple_of(c * blk, blk), blk)
            for hh, cols in enumerate(hcols):
                kp_ref[hh, rows, :HEAD_DIM] = _head_rmsnorm(k_ref[rows, cols], gk_ref[...])
                kp_ref[hh, rows, HEAD_DIM:] = kb_ref[rows, cols]
            return carry
        lax.fori_loop(0, seq // blk, fill, 0)

    lane = lax.broadcasted_iota(jnp.int32, (blk, HEAD_DIM), 1)
    q_ones = jnp.where(lane < 3, 1.0, 0.0).astype(BF16)
    key_pos = lax.broadcasted_iota(jnp.int32, (blk, blk), 0)
    qry_pos = lax.broadcasted_iota(jnp.int32, (blk, blk), 1)

    q2 = [jnp.concatenate([_head_rmsnorm(q_ref[:, cols], gq_ref[...]), q_ones], axis=1) for cols in hcols]

    def scores(hh, j, slot):
        rows = pl.ds(pl.multiple_of(j * blk, blk), blk)
        s_ref[hh, slot] = lax.dot_general(kp_ref[hh, rows, :], q2[hh], NT_DIMS, preferred_element_type=F32)

    def update(hh, j, slot, diagonal=False):
        rows = pl.ds(pl.multiple_of(j * blk, blk), blk)
        s = s_ref[hh, slot]
        if diagonal:
            s = jnp.where(key_pos <= qry_pos, s, -jnp.inf)
            m_new = jnp.max(s, axis=0, keepdims=True)
        else:
            m = m_ref[hh]
            m_new = jnp.maximum(m, jnp.max(s, axis=0, keepdims=True))
        p = jnp.exp2(s - m_new)
        pv = lax.dot_general(v_ref[rows, hcols[hh]], p.astype(BF16), TN_DIMS, preferred_element_type=F32)
        m_ref[hh] = m_new
        if diagonal:
            l_ref[hh] = jnp.sum(p, axis=0, keepdims=True)
            acc_ref[hh] = pv
        else:
            alpha = jnp.exp2(m - m_new)
            l_ref[hh] = alpha * l_ref[hh] + jnp.sum(p, axis=0, keepdims=True)
            acc_ref[hh] = alpha * acc_ref[hh] + pv

    def each_head(fn, *args, **kwargs):
        for hh in range(heads):
            fn(hh, *args, **kwargs)

    def pair(t0):
        each_head(scores, t0 + 1, 1)
        for hh in range(heads):
            update(hh, t0, 0)
            scores(hh, t0 + 2, 0)
        each_head(update, t0 + 1, 1)

    def pair_step(k, carry):
        pair(2 * k)
        return carry

    each_head(scores, qi, 1)
    each_head(scores, 0, 0)
    each_head(update, qi, 1, diagonal=True)
    lax.fori_loop(0, qi // 2, pair_step, 0)

    @pl.when(qi % 2 == 1)
    def _():
        each_head(update, qi - 1, 0)

    for hh, cols in enumerate(hcols):
        o_ref[:, cols] = (acc_ref[hh] / l_ref[hh]).T.astype(o_ref.dtype)


def _fox_attention(proj, kbias, gq, gk, batch, seq, n_heads, blk=512, heads=2):
    blk = _tile(seq, blk)
    heads = _tile(n_heads, heads)
    nq = seq // blk
    hblk = n_heads // heads
    w = heads * HEAD_DIM
    gain = pl.BlockSpec((1, HEAD_DIM), lambda b, h, i: (0, 0))
    return pl.pallas_call(
        functools.partial(_attn_kernel, blk=blk, heads=heads),
        grid=(batch, hblk, nq),
        in_specs=[pl.BlockSpec((blk, w), lambda b, h, i: (b * nq + i, h)),
                  pl.BlockSpec((seq, w), lambda b, h, i: (b, hblk + h)),
                  pl.BlockSpec((seq, w), lambda b, h, i: (b, h)),
                  pl.BlockSpec((seq, w), lambda b, h, i: (b, 2 * hblk + h)), gain, gain],
        out_specs=pl.BlockSpec((blk, w), lambda b, h, i: (b * nq + i, h)),
        out_shape=jax.ShapeDtypeStruct((batch * seq, n_heads * HEAD_DIM), BF16),
        scratch_shapes=[pltpu.VMEM((heads, seq, 2 * HEAD_DIM), BF16), pltpu.VMEM((heads, 2, blk, blk), F32),
                        pltpu.VMEM((heads, 1, blk), F32), pltpu.VMEM((heads, 1, blk), F32),
                        pltpu.VMEM((heads, HEAD_DIM, blk), F32)],
        compiler_params=_params("parallel", "parallel", "arbitrary"),
        name="fox_attention",
    )(proj, proj, kbias, proj, gq, gk)


def _hgrn_kernel(q_ref, z_ref, v_ref, gate_ref, lbl_ref, gn_ref, o_ref, st_ref, rb_ref, *, layer, heads, chunks):
    C, c = HGRN_CHUNK, HGRN_SUB
    t = pl.program_id(2)

    @pl.when(t == 0)
    def _():
        st_ref[...] = jnp.zeros_like(st_ref)

    logits = lbl_ref[...]
    e = jnp.exp(logits - jnp.max(logits, axis=0, keepdims=True))
    lb_all = jnp.sum(e[:layer + 1], axis=0, keepdims=True) / jnp.sum(e, axis=0, keepdims=True)

    row = lax.broadcasted_iota(jnp.int32, (C, C), 0)
    col = lax.broadcasted_iota(jnp.int32, (C, C), 1)
    ltri = jnp.where(col <= row, 1.0, 0.0).astype(BF16)
    sizes = [c << l for l in range(1, (C // c).bit_length())]
    xor = row ^ col
    level = jnp.zeros((C, C), jnp.int32)
    for l, m in enumerate(sizes):
        level = jnp.where(xor >= m // 2, l + 1, level)
    level = jnp.where(col > row, -1, level)
    lane = lax.broadcasted_iota(jnp.int32, (c, C), 1)

    def chunk_body(ci, carry):
        r0 = pl.multiple_of(ci * C, C)
        hs = range(heads)
        cols = [slice(hh * HEAD_DIM, (hh + 1) * HEAD_DIM) for hh in hs]
        qs, kk, b, v, o, a = {}, {}, {}, {}, {}, {}
        for hh in hs:
            lb = lb_all[:, cols[hh]]
            q = q_ref[pl.ds(r0, C), cols[hh]].astype(F32)
            z = z_ref[pl.ds(r0, C), cols[hh]].astype(F32)
            v[hh] = v_ref[pl.ds(r0, C), cols[hh]]
            ez = jnp.exp(-jnp.abs(z))
            inv = 1.0 / (1.0 + ez)
            sig_z = jnp.where(z >= 0, 1.0, ez) * inv
            sig_mz = jnp.where(z >= 0, ez, 1.0) * inv
            kk[hh] = (1.0 - lb) * sig_mz
            qs[hh] = q * _sigmoid(q)
            log2f = jnp.concatenate(_split3(jnp.log2(lb + (1.0 - lb) * sig_z)), axis=1).astype(BF16)
            cs = jnp.dot(ltri, log2f, preferred_element_type=F32)
            b[hh] = cs[:, :HEAD_DIM] + cs[:, HEAD_DIM:2 * HEAD_DIM] + cs[:, 2 * HEAD_DIM:]
            rb_ref[2 * hh] = kk[hh]
            rb_ref[2 * hh + 1] = b[hh]

        for hh in hs:
            o[hh] = lax.dot_general((qs[hh] * jnp.exp2(b[hh])).astype(BF16), st_ref[hh].astype(BF16), NT_DIMS,
                                    preferred_element_type=F32)

        blocks = {hh: [] for hh in hs}
        for i in range(C // c):
            rows = slice(i * c, (i + 1) * c)
            blk = {hh: jnp.zeros((c, C), F32) for hh in hs}
            for s in range(i * c, (i + 1) * c):
                at_s = lane == s
                for hh in hs:
                    w = qs[hh][rows] * rb_ref[2 * hh, s:s + 1, :] * jnp.exp2(b[hh][rows] - rb_ref[2 * hh + 1, s:s + 1, :])
                    blk[hh] = jnp.where(at_s, jnp.sum(w, axis=-1, keepdims=True), blk[hh])
            for hh in hs:
                blocks[hh].append(blk[hh])
        in_sub = level == 0
        for hh in hs:
            a[hh] = jnp.where(in_sub, jnp.concatenate(blocks[hh], axis=0), 0.0)

        qs_b = {hh: qs[hh].astype(BF16) for hh in hs}
        kk_b = {hh: kk[hh].astype(BF16) for hh in hs}
        for l, m in enumerate(sizes):
            at_level = level == l + 1
            for hh in hs:
                bh = b[hh]
                edge = jnp.concatenate(
                    [jnp.broadcast_to(bh[j * m + m // 2 - 1:j * m + m // 2, :], (m, HEAD_DIM)) for j in range(C // m)],
                    axis=0)
                x = jnp.exp2(-jnp.abs(bh - edge)).astype(BF16)
                pair = lax.dot_general(qs_b[hh] * x, kk_b[hh] * x, NT_DIMS, preferred_element_type=F32)
                a[hh] = jnp.where(at_level, pair, a[hh])

        for hh in hs:
            o[hh] = o[hh] + jnp.dot(a[hh].astype(BF16), v[hh], preferred_element_type=F32)
            b_last = b[hh][C - 1:C, :]
            k_d = (kk[hh] * jnp.exp2(b_last - b[hh])).astype(BF16)
            st_ref[hh] = st_ref[hh] * jnp.exp2(b_last) + lax.dot_general(v[hh], k_d, TN_DIMS,
                                                                         preferred_element_type=F32)

        for hh in hs:
            gate = gate_ref[pl.ds(r0, C), cols[hh]].astype(F32)
            ms = jnp.mean(o[hh] * o[hh], axis=-1, keepdims=True)
            on = o[hh] * lax.rsqrt(ms + RMS_EPS) * gn_ref[...]
            o_ref[pl.ds(r0, C), cols[hh]] = (on * (gate * _sigmoid(gate))).astype(o_ref.dtype)
        return carry

    lax.fori_loop(0, chunks, chunk_body, 0)


def _hgrn2(proj, lb_logits, gn, layer, batch, seq, n_heads, col0, heads=4, rows=512):
    heads = _tile(n_heads, heads)
    rows = _tile(seq, rows)
    assert rows % HGRN_CHUNK == 0 and col0 % heads == 0
    nt = seq // rows
    w = heads * HEAD_DIM
    hblk = n_heads // heads

    def in_spec(group):
        return pl.BlockSpec((rows, w), lambda b, h, t: (b * nt + t, col0 // heads + group * hblk + h))

    return pl.pallas_call(
        functools.partial(_hgrn_kernel, layer=layer, heads=heads, chunks=rows // HGRN_CHUNK),
        grid=(batch, hblk, nt),
        in_specs=[in_spec(0), in_spec(1), in_spec(2), in_spec(3),
                  pl.BlockSpec((lb_logits.shape[0], w), lambda b, h, t: (0, h)),
                  pl.BlockSpec((1, HEAD_DIM), lambda b, h, t: (0, 0))],
        out_specs=pl.BlockSpec((rows, w), lambda b, h, t: (b * nt + t, h)),
        out_shape=jax.ShapeDtypeStruct((batch * seq, n_heads * HEAD_DIM), BF16),
        scratch_shapes=[pltpu.VMEM((heads, HEAD_DIM, HEAD_DIM), F32),
                        pltpu.VMEM((2 * heads, HGRN_CHUNK, HEAD_DIM), F32)],
        compiler_params=_params("parallel", "parallel", "arbitrary"),
        name="hgrn2",
    )(proj, proj, proj, proj, lb_logits, gn)


def _ple_embed_kernel(p_ref, w_ref, g_ref, o_ref):
    e = jnp.dot(p_ref[...].astype(BF16), w_ref[...], preferred_element_type=F32)
    ms = jnp.mean(e * e, axis=-1, keepdims=True)
    o_ref[...] = (e * lax.rsqrt(ms + RMS_EPS) * g_ref[...]).astype(o_ref.dtype)


def _ple_embed(p, w, g):
    m, kd = p.shape
    n = w.shape[1]
    tm = _tile(m, 256)
    return pl.pallas_call(
        _ple_embed_kernel,
        grid=(m // tm,),
        in_specs=[pl.BlockSpec((tm, kd), lambda i: (i, 0)), pl.BlockSpec((kd, n), lambda i: (0, 0)),
                  pl.BlockSpec((1, n), lambda i: (0, 0))],
        out_specs=pl.BlockSpec((tm, n), lambda i: (i, 0)),
        out_shape=jax.ShapeDtypeStruct((m, n), BF16),
        compiler_params=_params("parallel"),
        name="ple_embed",
    )(p, w, g.reshape(1, n).astype(F32))


def _ple_gate_kernel(a_ref, w_ref, e_ref, h_ref, o_ref, rs_ref):
    @pl.when(pl.program_id(1) == 0)
    def _():
        a = a_ref[...].astype(F32)
        rs_ref[...] = lax.rsqrt(jnp.mean(a * a, axis=-1, keepdims=True) + RMS_EPS)

    gate = _sigmoid(jnp.dot(a_ref[...], w_ref[...], preferred_element_type=F32) * rs_ref[...])
    o_ref[...] = h_ref[...] + gate * e_ref[...]


def _ple_gate(a, w, e, h, tm=1024, tn=1024):
    m, kd = a.shape
    n = w.shape[1]
    tm, tn = _tile(m, tm), _tile(n, tn)
    return pl.pallas_call(
        _ple_gate_kernel,
        grid=(m // tm, n // tn),
        in_specs=[pl.BlockSpec((tm, kd), lambda i, j: (i, 0)), pl.BlockSpec((kd, tn), lambda i, j: (0, j)),
                  pl.BlockSpec((tm, tn), lambda i, j: (i, j)), pl.BlockSpec((tm, tn), lambda i, j: (i, j))],
        out_specs=pl.BlockSpec((tm, tn), lambda i, j: (i, j)),
        out_shape=jax.ShapeDtypeStruct((m, n), F32),
        scratch_shapes=[pltpu.VMEM((tm, 1), F32)],
        compiler_params=_params("parallel", "arbitrary", vmem_limit=V7X_VMEM_LIMIT_MAX),
        name="ple_gate",
    )(a, w, e, h)


def _layer(h, p, layer, batch, seq, norm_mix_g, w_in, fox_f_bias, fox_q_norm_g, fox_k_norm_g, hgrn_lb_logits,
           hgrn_norm_g, w_out, norm_mlp_g, w_up, w_down, ple_norm_g, w_ple_gate, w_ple_proj, ple_post_g):
    d = h.shape[1]
    d_a = d_b = d // 2
    n_a, n_b = d_a // HEAD_DIM, d_b // HEAD_DIM
    assert n_a <= HEAD_DIM and w_in.shape[1] == 3 * d_a + n_a + 4 * d_b

    w_in_t = jnp.swapaxes(w_in, 0, 1)
    w_fox = _cast_rows(w_in_t, 0, 3 * d_a)
    w_fa = _cast_rows(w_in_t, 3 * d_a, HEAD_DIM)
    q_gain = (fox_q_norm_g.astype(F32) * (LOG2E / math.sqrt(HEAD_DIM))).reshape(1, HEAD_DIM)
    k_gain = fox_k_norm_g.astype(F32).reshape(1, HEAD_DIM)

    u = _rmsnorm(h, norm_mix_g)
    proj_a, f_logit, w_hgrn = _inproj(u, w_fox, w_fa, w_in_t, 3 * d_a + n_a, 4 * d_b)
    proj_b, w_up_b, w_out_b = _matmul(u, w_hgrn, BF16, w_transposed=True, side=(w_up, w_out), name="inproj_hgrn")

    f_bias = jnp.pad(fox_f_bias.astype(F32), (0, HEAD_DIM - n_a)).reshape(1, HEAD_DIM)
    kbias = _forget_bias_tiles(f_logit, f_bias, batch, seq, n_a)
    out_a = _fox_attention(proj_a, kbias, q_gain, k_gain, batch, seq, n_a)

    out_b = _hgrn2(proj_b, hgrn_lb_logits.astype(F32), hgrn_norm_g.astype(F32).reshape(1, HEAD_DIM), layer,
                   batch, seq, n_b, col0=0)

    h, hg, part = _outproj(out_a, out_b, w_out_b, h, norm_mlp_g)
    hid, w_down_b, w_gate_g = _matmul(hg, w_up_b, BF16, act="relu2", part=part,
                                      side=(w_down, (w_ple_gate, ple_norm_g)),
                                      vmem_limit=V7X_VMEM_LIMIT_HIGH, name="mlp_up")
    h, hb = _matmul_residual(hid, w_down_b, h)

    e = _ple_embed(p, w_ple_proj.astype(BF16), ple_post_g)
    return _ple_gate(hb, w_gate_g, e, h)


def kernel(x, p, norm_mix_g, w_in, fox_f_bias, fox_q_norm_g, fox_k_norm_g, hgrn_lb_logits, hgrn_norm_g, w_out,
           norm_mlp_g, w_up, w_down, ple_norm_g, w_ple_gate, w_ple_proj, ple_post_g):
    batch, seq, d = x.shape
    h = x.reshape(batch * seq, d)
    for i in range(w_in.shape[0]):
        h = _layer(h, p[i].reshape(batch * seq, -1), i, batch, seq, norm_mix_g[i], w_in[i], fox_f_bias[i],
                   fox_q_norm_g[i], fox_k_norm_g[i], hgrn_lb_logits, hgrn_norm_g[i], w_out[i], norm_mlp_g[i],
                   w_up[i], w_down[i], ple_norm_g[i], w_ple_gate[i], w_ple_proj[i], ple_post_g[i])
    return h.reshape(batch, seq, d)
```

```python
import functools
import math

import jax
import jax.numpy as jnp
from jax import lax
from jax.experimental import pallas as pl
from jax.experimental.pallas import tpu as pltpu

F32, BF16 = jnp.float32, jnp.bfloat16
HEAD_DIM = 128
RMS_EPS = 1e-6
LOG2E = math.log2(math.e)
HGRN_CHUNK = 128
HGRN_SUB = 8
V7X_VMEM_LIMIT = 56 * 1024 * 1024
V7X_VMEM_LIMIT_HIGH = 60 * 1024 * 1024
V7X_VMEM_LIMIT_MAX = 63 * 1024 * 1024

NT_DIMS = (((1,), (1,)), ((), ()))
TN_DIMS = (((0,), (0,)), ((), ()))


def _params(*semantics, vmem_limit=V7X_VMEM_LIMIT):
    return pltpu.CompilerParams(dimension_semantics=semantics, vmem_limit_bytes=vmem_limit)


def _tile(dim, pref):
    t = min(dim, pref)
    assert dim % t == 0, (dim, pref)
    return t


def _sigmoid(x):
    return 1.0 / (1.0 + jnp.exp(-x))


def _rmsnorm_kernel(x_ref, g_ref, o_ref):
    x = x_ref[...]
    ms = jnp.mean(x * x, axis=-1, keepdims=True)
    o_ref[...] = (x * lax.rsqrt(ms + RMS_EPS) * g_ref[...]).astype(o_ref.dtype)


def _rmsnorm(x, g, out_dtype=BF16):
    m, d = x.shape
    tm = _tile(m, 256)
    return pl.pallas_call(
        _rmsnorm_kernel,
        grid=(m // tm,),
        in_specs=[pl.BlockSpec((tm, d), lambda i: (i, 0)), pl.BlockSpec((1, d), lambda i: (0, 0))],
        out_specs=pl.BlockSpec((tm, d), lambda i: (i, 0)),
        out_shape=jax.ShapeDtypeStruct((m, d), out_dtype),
        compiler_params=_params("parallel"),
        name="rmsnorm",
    )(x, g.reshape(1, d).astype(F32))


def _row_scale(part_ref, d):
    return lax.rsqrt(jnp.sum(part_ref[...], axis=-1, keepdims=True) * (1.0 / d) + RMS_EPS)


def _fold_sq(x):
    sq = x * x
    out = sq[:, :HEAD_DIM]
    for t in range(1, x.shape[1] // HEAD_DIM):
        out = out + sq[:, t * HEAD_DIM:(t + 1) * HEAD_DIM]
    return out


def _mm_kernel(*refs, act, w_dims, scaled, side_scaled):
    n_side, n_scale = len(side_scaled), sum(side_scaled)
    a_ref, w_ref = refs[:2]
    side_in = refs[2 + scaled:2 + scaled + n_side]
    scales = iter(refs[2 + scaled + n_side:2 + scaled + n_side + n_scale])
    o_ref = refs[2 + scaled + n_side + n_scale]
    side_out = refs[3 + scaled + n_side + n_scale:3 + scaled + 2 * n_side + n_scale]
    if scaled:
        part_ref, rs_ref = refs[2], refs[-1]

        @pl.when(pl.program_id(1) == 0)
        def _():
            rs_ref[...] = _row_scale(part_ref, a_ref.shape[1])

    for src, dst, has_scale in zip(side_in, side_out, side_scaled):
        slab = src[...] * next(scales)[...] if has_scale else src[...]
        dst[...] = slab.astype(dst.dtype)
    acc = lax.dot_general(a_ref[...], w_ref[...], w_dims, preferred_element_type=F32)
    if scaled:
        acc = acc * rs_ref[...]
    if act == "relu2":
        acc = jnp.square(jnp.maximum(acc, 0.0))
    o_ref[...] = acc.astype(o_ref.dtype)


def _matmul(a, w, out_dtype, act=None, part=None, w_transposed=False, side=(), tm=1024, tn=1024,
            vmem_limit=V7X_VMEM_LIMIT, name="matmul"):
    m, k = a.shape
    n = w.shape[0] if w_transposed else w.shape[1]
    tm, tn = _tile(m, tm), _tile(n, tn)
    ni, nj = m // tm, n // tn
    w_spec = pl.BlockSpec((tn, k), lambda i, j: (j, 0)) if w_transposed else pl.BlockSpec((k, tn), lambda i, j: (0, j))
    in_specs, operands = [pl.BlockSpec((tm, k), lambda i, j: (i, 0)), w_spec], [a, w]
    scratch = []
    if part is not None:
        in_specs.append(pl.BlockSpec((tm, part.shape[1]), lambda i, j: (i, 0)))
        operands.append(part)
        scratch.append(pltpu.VMEM((tm, 1), F32))
    out_specs = [pl.BlockSpec((tm, tn), lambda i, j: (i, j))]
    out_shape = [jax.ShapeDtypeStruct((m, n), out_dtype)]
    side = [sw if isinstance(sw, tuple) else (sw, None) for sw in side]
    scale_specs, scale_operands = [], []
    for sw, row_scale in side:
        rows = sw.shape[0] // (ni * nj)
        assert rows * ni * nj == sw.shape[0]
        slab = pl.BlockSpec((rows, sw.shape[1]), lambda i, j: (i * nj + j, 0))
        in_specs.append(slab)
        operands.append(sw)
        out_specs.append(slab)
        out_shape.append(jax.ShapeDtypeStruct(sw.shape, BF16))
        if row_scale is not None:
            scale_specs.append(pl.BlockSpec((rows, 1), lambda i, j: (i * nj + j, 0)))
            scale_operands.append(row_scale.astype(F32).reshape(-1, 1))
    in_specs += scale_specs
    operands += scale_operands
    w_dims = NT_DIMS if w_transposed else (((1,), (0,)), ((), ()))
    outs = pl.pallas_call(
        functools.partial(_mm_kernel, act=act, w_dims=w_dims, scaled=part is not None,
                          side_scaled=tuple(rs is not None for _, rs in side)),
        grid=(ni, nj),
        in_specs=in_specs,
        out_specs=out_specs,
        out_shape=out_shape,
        scratch_shapes=scratch,
        compiler_params=_params("parallel", "arbitrary", vmem_limit=vmem_limit),
        name=name,
    )(*operands)
    return outs if side else outs[0]


def _cast_rows_kernel(w_ref, o_ref):
    o_ref[...] = w_ref[...].astype(o_ref.dtype)


def _cast_rows(w, row0, nrows, tr=512):
    k = w.shape[1]
    tr = math.gcd(nrows, tr)
    align = math.gcd(math.gcd(row0, tr), 64)
    return pl.pallas_call(
        _cast_rows_kernel,
        grid=(nrows // tr,),
        in_specs=[pl.BlockSpec((pl.Element(tr), pl.Element(k)),
                               lambda i: (pl.multiple_of(row0 + i * tr, align), 0))],
        out_specs=pl.BlockSpec((tr, k), lambda i: (i, 0)),
        out_shape=jax.ShapeDtypeStruct((nrows, k), BF16),
        compiler_params=_params("parallel"),
        name="cast_weight_rows",
    )(w)


def _inproj_kernel(u_ref, wt_ref, wft_ref, side_ref, o_ref, f_ref, side_out_ref):
    @pl.when(pl.program_id(1) == 0)
    def _():
        f_ref[...] = lax.dot_general(u_ref[...], wft_ref[...], NT_DIMS, preferred_element_type=F32)

    side_out_ref[...] = side_ref[...].astype(side_out_ref.dtype)
    o_ref[...] = lax.dot_general(u_ref[...], wt_ref[...], NT_DIMS, preferred_element_type=F32).astype(o_ref.dtype)


def _inproj(u, wt, wft, side_src, side_row0, side_rows, tm=1024, tn=768):
    m, k = u.shape
    n, nf = wt.shape[0], wft.shape[0]
    tm, tn = _tile(m, tm), math.gcd(n, tn)
    ni, nj = m // tm, n // tn
    slab = side_rows // (ni * nj)
    assert slab * ni * nj == side_rows
    align = math.gcd(math.gcd(side_row0, slab), 64)
    return pl.pallas_call(
        _inproj_kernel,
        grid=(ni, nj),
        in_specs=[pl.BlockSpec((tm, k), lambda i, j: (i, 0)), pl.BlockSpec((tn, k), lambda i, j: (j, 0)),
                  pl.BlockSpec((nf, k), lambda i, j: (0, 0)),
                  pl.BlockSpec((pl.Element(slab), pl.Element(side_src.shape[1])),
                               lambda i, j: (pl.multiple_of(side_row0 + (i * nj + j) * slab, align), 0))],
        out_specs=[pl.BlockSpec((tm, tn), lambda i, j: (i, j)), pl.BlockSpec((tm, nf), lambda i, j: (i, 0)),
                   pl.BlockSpec((slab, side_src.shape[1]), lambda i, j: (i * nj + j, 0))],
        out_shape=[jax.ShapeDtypeStruct((m, n), BF16), jax.ShapeDtypeStruct((m, nf), F32),
                   jax.ShapeDtypeStruct((side_rows, side_src.shape[1]), BF16)],
        compiler_params=_params("parallel", "arbitrary"),
        name="inproj_fox",
    )(u, wt, wft, side_src)


def _outproj_kernel(a_ref, b_ref, wa_ref, wb_ref, x_ref, g_ref, o_ref, ob_ref, part_ref):
    acc = jnp.dot(a_ref[...], wa_ref[...], preferred_element_type=F32)
    acc = acc + jnp.dot(b_ref[...], wb_ref[...], preferred_element_type=F32)
    h = x_ref[...] + acc
    o_ref[...] = h
    ob_ref[...] = (h * g_ref[...]).astype(ob_ref.dtype)
    part_ref[...] = _fold_sq(h)


def _outproj(a, b, w, x, g, tm=1024, tn=1024):
    m, ka = a.shape
    kb = b.shape[1]
    n = w.shape[1]
    assert ka == kb and w.shape[0] == ka + kb
    tm, tn = _tile(m, tm), _tile(n, tn)
    tile = pl.BlockSpec((tm, tn), lambda i, j: (i, j))
    return pl.pallas_call(
        _outproj_kernel,
        grid=(m // tm, n // tn),
        in_specs=[pl.BlockSpec((tm, ka), lambda i, j: (i, 0)), pl.BlockSpec((tm, kb), lambda i, j: (i, 0)),
                  pl.BlockSpec((ka, tn), lambda i, j: (0, j)), pl.BlockSpec((kb, tn), lambda i, j: (1, j)), tile,
                  pl.BlockSpec((1, tn), lambda i, j: (0, j))],
        out_specs=[tile, tile, pl.BlockSpec((tm, HEAD_DIM), lambda i, j: (i, j))],
        out_shape=[jax.ShapeDtypeStruct((m, n), F32), jax.ShapeDtypeStruct((m, n), BF16),
                   jax.ShapeDtypeStruct((m, n // tn * HEAD_DIM), F32)],
        compiler_params=_params("parallel", "arbitrary", vmem_limit=V7X_VMEM_LIMIT_MAX),
        name="outproj",
    )(a, b, w, w, x, g.reshape(1, n).astype(F32))


def _mm_res_kernel(a_ref, w_ref, r_ref, o_ref, ob_ref):
    k = pl.program_id(2)

    @pl.when(k == 0)
    def _():
        o_ref[...] = r_ref[...]

    o_ref[...] += jnp.dot(a_ref[...], w_ref[...], preferred_element_type=F32)

    @pl.when(k == pl.num_programs(2) - 1)
    def _():
        ob_ref[...] = o_ref[...].astype(ob_ref.dtype)


def _matmul_residual(a, w, r, tm=1024, tn=1024, tk=4096):
    m, kd = a.shape
    n = w.shape[1]
    tm, tn, tk = _tile(m, tm), _tile(n, tn), _tile(kd, tk)
    return pl.pallas_call(
        _mm_res_kernel,
        grid=(m // tm, n // tn, kd // tk),
        in_specs=[pl.BlockSpec((tm, tk), lambda i, j, k: (i, k)), pl.BlockSpec((tk, tn), lambda i, j, k: (k, j)),
                  pl.BlockSpec((tm, tn), lambda i, j, k: (i, j))],
        out_specs=[pl.BlockSpec((tm, tn), lambda i, j, k: (i, j)), pl.BlockSpec((tm, tn), lambda i, j, k: (i, j))],
        out_shape=[jax.ShapeDtypeStruct((m, n), F32), jax.ShapeDtypeStruct((m, n), BF16)],
        compiler_params=_params("parallel", "parallel", "arbitrary", vmem_limit=V7X_VMEM_LIMIT_MAX),
        name="mlp_down",
    )(a, w, r)


def _split3(x):
    hi = x.astype(BF16).astype(F32)
    r = x - hi
    mid = r.astype(BF16).astype(F32)
    return hi, mid, r - mid


def _fcum_kernel(f_ref, bias_ref, o_ref, carry_ref, *, n_heads):
    t = pl.program_id(1)

    @pl.when(t == 0)
    def _():
        carry_ref[...] = jnp.zeros_like(carry_ref)

    x = f_ref[...] + bias_ref[...]
    ls = jnp.minimum(x, 0.0) - jnp.log1p(jnp.exp(-jnp.abs(x)))
    tc = x.shape[0]
    r = lax.broadcasted_iota(jnp.int32, (tc, tc), 0)
    c = lax.broadcasted_iota(jnp.int32, (tc, tc), 1)
    tri = jnp.where(c <= r, 1.0, 0.0).astype(BF16)
    c3 = jnp.dot(tri, jnp.concatenate(_split3(ls), axis=1).astype(BF16), preferred_element_type=F32)
    cs = c3[:, :HEAD_DIM] + c3[:, HEAD_DIM:2 * HEAD_DIM] + c3[:, 2 * HEAD_DIM:] + carry_ref[...]
    carry_ref[...] = cs[tc - 1:tc, :]
    lane = lax.broadcasted_iota(jnp.int32, (tc, HEAD_DIM), 1)
    for h in range(n_heads):
        hi, mid, lo = _split3(jnp.broadcast_to(cs[:, h:h + 1] * (-LOG2E), (tc, HEAD_DIM)))
        tile = jnp.where(lane == 0, hi, jnp.where(lane == 1, mid, jnp.where(lane == 2, lo, 0.0)))
        o_ref[:, h * HEAD_DIM:(h + 1) * HEAD_DIM] = tile.astype(o_ref.dtype)


def _forget_bias_tiles(f_logit, bias, batch, seq, n_heads):
    tc = _tile(seq, 256)
    nt = seq // tc
    return pl.pallas_call(
        functools.partial(_fcum_kernel, n_heads=n_heads),
        grid=(batch, nt),
        in_specs=[pl.BlockSpec((tc, HEAD_DIM), lambda b, t: (b * nt + t, 0)),
                  pl.BlockSpec((1, HEAD_DIM), lambda b, t: (0, 0))],
        out_specs=pl.BlockSpec((tc, n_heads * HEAD_DIM), lambda b, t: (b * nt + t, 0)),
        out_shape=jax.ShapeDtypeStruct((batch * seq, n_heads * HEAD_DIM), BF16),
        scratch_shapes=[pltpu.VMEM((1, HEAD_DIM), F32)],
        compiler_params=_params("parallel", "arbitrary"),
        name="fox_forget_cumsum",
    )(f_logit, bias)


def _head_rmsnorm(x, g):
    x = x.astype(F32)
    ms = jnp.mean(x * x, axis=-1, keepdims=True)
    return (x * lax.rsqrt(ms + RMS_EPS) * g).astype(BF16)


def _attn_kernel(q_ref, k_ref, kb_ref, v_ref, gq_ref, gk_ref, o_ref, kp_ref, s_ref, m_ref, l_ref, acc_ref, *,
                 blk, heads):
    qi = pl.program_id(2)
    hcols = [slice(hh * HEAD_DIM, (hh + 1) * HEAD_DIM) for hh in range(heads)]
    seq = k_ref.shape[0]

    @pl.when(qi == 0)
    def _():
        def fill(c, carry):
            rows = pl.ds(pl.multiple_of(c * blk, blk), blk)
            for hh, cols in enumerate(hcols):
                kp_ref[hh, rows, :HEAD_DIM] = _head_rmsnorm(k_ref[rows, cols], gk_ref[...])
                kp_ref[hh, rows, HEAD_DIM:] = kb_ref[rows, cols]
            return carry
        lax.fori_loop(0, seq // blk, fill, 0)

    lane = lax.broadcasted_iota(jnp.int32, (blk, HEAD_DIM), 1)
    q_ones = jnp.where(lane < 3, 1.0, 0.0).astype(BF16)
    key_pos = lax.broadcasted_iota(jnp.int32, (blk, blk), 0)
    qry_pos = lax.broadcasted_iota(jnp.int32, (blk, blk), 1)

    q2 = [jnp.concatenate([_head_rmsnorm(q_ref[:, cols], gq_ref[...]), q_ones], axis=1) for cols in hcols]

    def scores(hh, j, slot):
        rows = pl.ds(pl.multiple_of(j * blk, blk), blk)
        s_ref[hh, slot] = lax.dot_general(kp_ref[hh, rows, :], q2[hh], NT_DIMS, preferred_element_type=F32)

    def update(hh, j, slot, diagonal=False):
        rows = pl.ds(pl.multiple_of(j * blk, blk), blk)
        s = s_ref[hh, slot]
        if diagonal:
            s = jnp.where(key_pos <= qry_pos, s, -jnp.inf)
            m_new = jnp.max(s, axis=0, keepdims=True)
        else:
            m = m_ref[hh]
            m_new = jnp.maximum(m, jnp.max(s, axis=0, keepdims=True))
        p = jnp.exp2(s - m_new)
        pv = lax.dot_general(v_ref[rows, hcols[hh]], p.astype(BF16), TN_DIMS, preferred_element_type=F32)
        m_ref[hh] = m_new
        if diagonal:
            l_ref[hh] = jnp.sum(p, axis=0, keepdims=True)
            acc_ref[hh] = pv
        else:
            alpha = jnp.exp2(m - m_new)
            l_ref[hh] = alpha * l_ref[hh] + jnp.sum(p, axis=0, keepdims=True)
            acc_ref[hh] = alpha * acc_ref[hh] + pv

    def each_head(fn, *args, **kwargs):
        for hh in range(heads):
            fn(hh, *args, **kwargs)

    def pair(t0):
        each_head(scores, t0 + 1, 1)
        for hh in range(heads):
            update(hh, t0, 0)
            scores(hh, t0 + 2, 0)
        each_head(update, t0 + 1, 1)

    def quad(k, carry):
        pair(4 * k)
        pair(4 * k + 2)
        return carry

    each_head(scores, qi, 1)
    each_head(scores, 0, 0)
    each_head(update, qi, 1, diagonal=True)
    lax.fori_loop(0, qi // 4, quad, 0)

    @pl.when((qi // 2) % 2 == 1)
    def _():
        pair(4 * (qi // 4))

    @pl.when(qi % 2 == 1)
    def _():
        each_head(update, qi - 1, 0)

    for hh, cols in enumerate(hcols):
        o_ref[:, cols] = (acc_ref[hh] / l_ref[hh]).T.astype(o_ref.dtype)


def _fox_attention(proj, kbias, gq, gk, batch, seq, n_heads, blk=512, heads=2):
    blk = _tile(seq, blk)
    heads = _tile(n_heads, heads)
    nq = seq // blk
    hblk = n_heads // heads
    w = heads * HEAD_DIM
    gain = pl.BlockSpec((1, HEAD_DIM), lambda b, h, i: (0, 0))
    return pl.pallas_call(
        functools.partial(_attn_kernel, blk=blk, heads=heads),
        grid=(batch, hblk, nq),
        in_specs=[pl.BlockSpec((blk, w), lambda b, h, i: (b * nq + i, h)),
                  pl.BlockSpec((seq, w), lambda b, h, i: (b, hblk + h)),
                  pl.BlockSpec((seq, w), lambda b, h, i: (b, h)),
                  pl.BlockSpec((seq, w), lambda b, h, i: (b, 2 * hblk + h)), gain, gain],
        out_specs=pl.BlockSpec((blk, w), lambda b, h, i: (b * nq + i, h)),
        out_shape=jax.ShapeDtypeStruct((batch * seq, n_heads * HEAD_DIM), BF16),
        scratch_shapes=[pltpu.VMEM((heads, seq, 2 * HEAD_DIM), BF16), pltpu.VMEM((heads, 2, blk, blk), F32),
                        pltpu.VMEM((heads, 1, blk), F32), pltpu.VMEM((heads, 1, blk), F32),
                        pltpu.VMEM((heads, HEAD_DIM, blk), F32)],
        compiler_params=_params("parallel", "parallel", "arbitrary"),
        name="fox_attention",
    )(proj, proj, kbias, proj, gq, gk)


def _hgrn_kernel(q_ref, z_ref, v_ref, gate_ref, lbl_ref, gn_ref, o_ref, st_ref, rb_ref, *, layer, heads, chunks):
    C, c = HGRN_CHUNK, HGRN_SUB
    t = pl.program_id(2)

    @pl.when(t == 0)
    def _():
        st_ref[...] = jnp.zeros_like(st_ref)

    logits = lbl_ref[...]
    e = jnp.exp(logits - jnp.max(logits, axis=0, keepdims=True))
    lb_all = jnp.sum(e[:layer + 1], axis=0, keepdims=True) / jnp.sum(e, axis=0, keepdims=True)

    row = lax.broadcasted_iota(jnp.int32, (C, C), 0)
    col = lax.broadcasted_iota(jnp.int32, (C, C), 1)
    ltri = jnp.where(col <= row, 1.0, 0.0).astype(BF16)
    sizes = [c << l for l in range(1, (C // c).bit_length())]
    xor = row ^ col
    level = jnp.zeros((C, C), jnp.int32)
    for l, m in enumerate(sizes):
        level = jnp.where(xor >= m // 2, l + 1, level)
    level = jnp.where(col > row, -1, level)
    lane = lax.broadcasted_iota(jnp.int32, (c, C), 1)

    def chunk_body(ci, carry):
        r0 = pl.multiple_of(ci * C, C)
        hs = range(heads)
        cols = [slice(hh * HEAD_DIM, (hh + 1) * HEAD_DIM) for hh in hs]
        qs, kk, b, v, o, a = {}, {}, {}, {}, {}, {}
        for hh in hs:
            lb = lb_all[:, cols[hh]]
            q = q_ref[pl.ds(r0, C), cols[hh]].astype(F32)
            z = z_ref[pl.ds(r0, C), cols[hh]].astype(F32)
            v[hh] = v_ref[pl.ds(r0, C), cols[hh]]
            ez = jnp.exp(-jnp.abs(z))
            inv = 1.0 / (1.0 + ez)
            sig_z = jnp.where(z >= 0, 1.0, ez) * inv
            sig_mz = jnp.where(z >= 0, ez, 1.0) * inv
            kk[hh] = (1.0 - lb) * sig_mz
            qs[hh] = q * _sigmoid(q)
            log2f = jnp.concatenate(_split3(jnp.log2(lb + (1.0 - lb) * sig_z)), axis=1).astype(BF16)
            cs = jnp.dot(ltri, log2f, preferred_element_type=F32)
            b[hh] = cs[:, :HEAD_DIM] + cs[:, HEAD_DIM:2 * HEAD_DIM] + cs[:, 2 * HEAD_DIM:]
            rb_ref[2 * hh] = kk[hh]
            rb_ref[2 * hh + 1] = b[hh]

        for hh in hs:
            o[hh] = lax.dot_general((qs[hh] * jnp.exp2(b[hh])).astype(BF16), st_ref[hh].astype(BF16), NT_DIMS,
                                    preferred_element_type=F32)

        blocks = {hh: [] for hh in hs}
        for i in range(C // c):
            rows = slice(i * c, (i + 1) * c)
            blk = {hh: jnp.zeros((c, C), F32) for hh in hs}
            for s in range(i * c, (i + 1) * c):
                at_s = lane == s
                for hh in hs:
                    w = qs[hh][rows] * rb_ref[2 * hh, s:s + 1, :] * jnp.exp2(b[hh][rows] - rb_ref[2 * hh + 1, s:s + 1, :])
                    blk[hh] = jnp.where(at_s, jnp.sum(w, axis=-1, keepdims=True), blk[hh])
            for hh in hs:
                blocks[hh].append(blk[hh])
        in_sub = level == 0
        for hh in hs:
            a[hh] = jnp.where(in_sub, jnp.concatenate(blocks[hh], axis=0), 0.0)

        for l, m in enumerate(sizes):
            at_level = level == l + 1
            for hh in hs:
                bh = b[hh]
                edge = jnp.concatenate(
                    [jnp.broadcast_to(bh[j * m + m // 2 - 1:j * m + m // 2, :], (m, HEAD_DIM)) for j in range(C // m)],
                    axis=0)
                x = jnp.exp2(-jnp.abs(bh - edge))
                pair = lax.dot_general((qs[hh] * x).astype(BF16), (kk[hh] * x).astype(BF16), NT_DIMS,
                                       preferred_element_type=F32)
                a[hh] = jnp.where(at_level, pair, a[hh])

        for hh in hs:
            o[hh] = o[hh] + jnp.dot(a[hh].astype(BF16), v[hh], preferred_element_type=F32)
            b_last = b[hh][C - 1:C, :]
            k_d = (kk[hh] * jnp.exp2(b_last - b[hh])).astype(BF16)
            st_ref[hh] = st_ref[hh] * jnp.exp2(b_last) + lax.dot_general(v[hh], k_d, TN_DIMS,
                                                                         preferred_element_type=F32)

        for hh in hs:
            gate = gate_ref[pl.ds(r0, C), cols[hh]].astype(F32)
            ms = jnp.mean(o[hh] * o[hh], axis=-1, keepdims=True)
            on = o[hh] * lax.rsqrt(ms + RMS_EPS) * gn_ref[...]
            o_ref[pl.ds(r0, C), cols[hh]] = (on * (gate * _sigmoid(gate))).astype(o_ref.dtype)
        return carry

    lax.fori_loop(0, chunks, chunk_body, 0)


def _hgrn2(proj, lb_logits, gn, layer, batch, seq, n_heads, col0, heads=4, rows=1024):
    heads = _tile(n_heads, heads)
    rows = _tile(seq, rows)
    assert rows % HGRN_CHUNK == 0 and col0 % heads == 0
    nt = seq // rows
    w = heads * HEAD_DIM
    hblk = n_heads // heads

    def in_spec(group):
        return pl.BlockSpec((rows, w), lambda b, h, t: (b * nt + t, col0 // heads + group * hblk + h))

    return pl.pallas_call(
        functools.partial(_hgrn_kernel, layer=layer, heads=heads, chunks=rows // HGRN_CHUNK),
        grid=(batch, hblk, nt),
        in_specs=[in_spec(0), in_spec(1), in_spec(2), in_spec(3),
                  pl.BlockSpec((lb_logits.shape[0], w), lambda b, h, t: (0, h)),
                  pl.BlockSpec((1, HEAD_DIM), lambda b, h, t: (0, 0))],
        out_specs=pl.BlockSpec((rows, w), lambda b, h, t: (b * nt + t, h)),
        out_shape=jax.ShapeDtypeStruct((batch * seq, n_heads * HEAD_DIM), BF16),
        scratch_shapes=[pltpu.VMEM((heads, HEAD_DIM, HEAD_DIM), F32),
                        pltpu.VMEM((2 * heads, HGRN_CHUNK, HEAD_DIM), F32)],
        compiler_params=_params("parallel", "parallel", "arbitrary"),
        name="hgrn2",
    )(proj, proj, proj, proj, lb_logits, gn)


def _ple_embed_kernel(p_ref, w_ref, g_ref, o_ref):
    e = jnp.dot(p_ref[...].astype(BF16), w_ref[...], preferred_element_type=F32)
    ms = jnp.mean(e * e, axis=-1, keepdims=True)
    o_ref[...] = (e * lax.rsqrt(ms + RMS_EPS) * g_ref[...]).astype(o_ref.dtype)


def _ple_embed(p, w, g):
    m, kd = p.shape
    n = w.shape[1]
    tm = _tile(m, 512)
    return pl.pallas_call(
        _ple_embed_kernel,
        grid=(m // tm,),
        in_specs=[pl.BlockSpec((tm, kd), lambda i: (i, 0)), pl.BlockSpec((kd, n), lambda i: (0, 0)),
                  pl.BlockSpec((1, n), lambda i: (0, 0))],
        out_specs=pl.BlockSpec((tm, n), lambda i: (i, 0)),
        out_shape=jax.ShapeDtypeStruct((m, n), BF16),
        compiler_params=_params("parallel"),
        name="ple_embed",
    )(p, w, g.reshape(1, n).astype(F32))


def _ple_gate_kernel(a_ref, w_ref, e_ref, h_ref, o_ref, rs_ref):
    @pl.when(pl.program_id(1) == 0)
    def _():
        a = a_ref[...].astype(F32)
        rs_ref[...] = lax.rsqrt(jnp.mean(a * a, axis=-1, keepdims=True) + RMS_EPS)

    gate = _sigmoid(jnp.dot(a_ref[...], w_ref[...], preferred_element_type=F32) * rs_ref[...])
    o_ref[...] = h_ref[...] + gate * e_ref[...]


def _ple_gate(a, w, e, h, tm=1024, tn=1024):
    m, kd = a.shape
    n = w.shape[1]
    tm, tn = _tile(m, tm), _tile(n, tn)
    return pl.pallas_call(
        _ple_gate_kernel,
        grid=(m // tm, n // tn),
        in_specs=[pl.BlockSpec((tm, kd), lambda i, j: (i, 0)), pl.BlockSpec((kd, tn), lambda i, j: (0, j)),
                  pl.BlockSpec((tm, tn), lambda i, j: (i, j)), pl.BlockSpec((tm, tn), lambda i, j: (i, j))],
        out_specs=pl.BlockSpec((tm, tn), lambda i, j: (i, j)),
        out_shape=jax.ShapeDtypeStruct((m, n), F32),
        scratch_shapes=[pltpu.VMEM((tm, 1), F32)],
        compiler_params=_params("parallel", "arbitrary", vmem_limit=V7X_VMEM_LIMIT_MAX),
        name="ple_gate",
    )(a, w, e, h)


def _layer(h, p, layer, batch, seq, norm_mix_g, w_in, fox_f_bias, fox_q_norm_g, fox_k_norm_g, hgrn_lb_logits,
           hgrn_norm_g, w_out, norm_mlp_g, w_up, w_down, ple_norm_g, w_ple_gate, w_ple_proj, ple_post_g):
    d = h.shape[1]
    d_a = d_b = d // 2
    n_a, n_b = d_a // HEAD_DIM, d_b // HEAD_DIM
    assert n_a <= HEAD_DIM and w_in.shape[1] == 3 * d_a + n_a + 4 * d_b

    w_in_t = jnp.swapaxes(w_in, 0, 1)
    w_fox = _cast_rows(w_in_t, 0, 3 * d_a)
    w_fa = _cast_rows(w_in_t, 3 * d_a, HEAD_DIM)
    q_gain = (fox_q_norm_g.astype(F32) * (LOG2E / math.sqrt(HEAD_DIM))).reshape(1, HEAD_DIM)
    k_gain = fox_k_norm_g.astype(F32).reshape(1, HEAD_DIM)

    u = _rmsnorm(h, norm_mix_g)
    proj_a, f_logit, w_hgrn = _inproj(u, w_fox, w_fa, w_in_t, 3 * d_a + n_a, 4 * d_b)
    proj_b, w_up_b, w_out_b = _matmul(u, w_hgrn, BF16, w_transposed=True, side=(w_up, w_out), name="inproj_hgrn")

    f_bias = jnp.pad(fox_f_bias.astype(F32), (0, HEAD_DIM - n_a)).reshape(1, HEAD_DIM)
    kbias = _forget_bias_tiles(f_logit, f_bias, batch, seq, n_a)
    out_a = _fox_attention(proj_a, kbias, q_gain, k_gain, batch, seq, n_a)

    out_b = _hgrn2(proj_b, hgrn_lb_logits.astype(F32), hgrn_norm_g.astype(F32).reshape(1, HEAD_DIM), layer,
                   batch, seq, n_b, col0=0)

    h, hg, part = _outproj(out_a, out_b, w_out_b, h, norm_mlp_g)
    hid, w_down_b, w_gate_g = _matmul(hg, w_up_b, BF16, act="relu2", part=part,
                                      side=(w_down, (w_ple_gate, ple_norm_g)),
                                      vmem_limit=V7X_VMEM_LIMIT_HIGH, name="mlp_up")
    h, hb = _matmul_residual(hid, w_down_b, h)

    e = _ple_embed(p, w_ple_proj.astype(BF16), ple_post_g)
    return _ple_gate(hb, w_gate_g, e, h)


def kernel(x, p, norm_mix_g, w_in, fox_f_bias, fox_q_norm_g, fox_k_norm_g, hgrn_lb_logits, hgrn_norm_g, w_out,
           norm_mlp_g, w_up, w_down, ple_norm_g, w_ple_gate, w_ple_proj, ple_post_g):
    batch, seq, d = x.shape
    h = x.reshape(batch * seq, d)
    for i in range(w_in.shape[0]):
        h = _layer(h, p[i].reshape(batch * seq, -1), i, batch, seq, norm_mix_g[i], w_in[i], fox_f_bias[i],
                   fox_q_norm_g[i], fox_k_norm_g[i], hgrn_lb_logits, hgrn_norm_g[i], w_out[i], norm_mlp_g[i],
                   w_up[i], w_down[i], ple_norm_g[i], w_ple_gate[i], w_ple_proj[i], ple_post_g[i])
    return h.reshape(batch, seq, d)
```

```python
import functools
import math

import jax
import jax.numpy as jnp
from jax import lax
from jax.experimental import pallas as pl
from jax.experimental.pallas import tpu as pltpu

F32, BF16 = jnp.float32, jnp.bfloat16
HEAD_DIM = 128
RMS_EPS = 1e-6
LOG2E = math.log2(math.e)
HGRN_CHUNK = 128
HGRN_SUB = 8
V7X_VMEM_LIMIT = 56 * 1024 * 1024
V7X_VMEM_LIMIT_HIGH = 60 * 1024 * 1024
V7X_VMEM_LIMIT_MAX = 63 * 1024 * 1024

NT_DIMS = (((1,), (1,)), ((), ()))
TN_DIMS = (((0,), (0,)), ((), ()))


def _params(*semantics, vmem_limit=V7X_VMEM_LIMIT):
    return pltpu.CompilerParams(dimension_semantics=semantics, vmem_limit_bytes=vmem_limit)


def _tile(dim, pref):
    t = min(dim, pref)
    assert dim % t == 0, (dim, pref)
    return t


def _sigmoid(x):
    return 1.0 / (1.0 + jnp.exp(-x))


def _rmsnorm_kernel(x_ref, g_ref, o_ref):
    x = x_ref[...]
    ms = jnp.mean(x * x, axis=-1, keepdims=True)
    o_ref[...] = (x * lax.rsqrt(ms + RMS_EPS) * g_ref[...]).astype(o_ref.dtype)


def _rmsnorm(x, g, out_dtype=BF16):
    m, d = x.shape
    tm = _tile(m, 256)
    return pl.pallas_call(
        _rmsnorm_kernel,
        grid=(m // tm,),
        in_specs=[pl.BlockSpec((tm, d), lambda i: (i, 0)), pl.BlockSpec((1, d), lambda i: (0, 0))],
        out_specs=pl.BlockSpec((tm, d), lambda i: (i, 0)),
        out_shape=jax.ShapeDtypeStruct((m, d), out_dtype),
        compiler_params=_params("parallel"),
        name="rmsnorm",
    )(x, g.reshape(1, d).astype(F32))


def _row_scale(part_ref, d):
    return lax.rsqrt(jnp.sum(part_ref[...], axis=-1, keepdims=True) * (1.0 / d) + RMS_EPS)


def _fold_sq(x):
    sq = x * x
    out = sq[:, :HEAD_DIM]
    for t in range(1, x.shape[1] // HEAD_DIM):
        out = out + sq[:, t * HEAD_DIM:(t + 1) * HEAD_DIM]
    return out


def _mm_kernel(*refs, act, w_dims, scaled, side_scaled):
    n_side, n_scale = len(side_scaled), sum(side_scaled)
    a_ref, w_ref = refs[:2]
    side_in = refs[2 + scaled:2 + scaled + n_side]
    scales = iter(refs[2 + scaled + n_side:2 + scaled + n_side + n_scale])
    o_ref = refs[2 + scaled + n_side + n_scale]
    side_out = refs[3 + scaled + n_side + n_scale:3 + scaled + 2 * n_side + n_scale]
    if scaled:
        part_ref, rs_ref = refs[2], refs[-1]

        @pl.when(pl.program_id(1) == 0)
        def _():
            rs_ref[...] = _row_scale(part_ref, a_ref.shape[1])

    for src, dst, has_scale in zip(side_in, side_out, side_scaled):
        slab = src[...] * next(scales)[...] if has_scale else src[...]
        dst[...] = slab.astype(dst.dtype)
    acc = lax.dot_general(a_ref[...], w_ref[...], w_dims, preferred_element_type=F32)
    if scaled:
        acc = acc * rs_ref[...]
    if act == "relu2":
        acc = jnp.square(jnp.maximum(acc, 0.0))
    o_ref[...] = acc.astype(o_ref.dtype)


def _matmul(a, w, out_dtype, act=None, part=None, w_transposed=False, side=(), tm=1024, tn=1024,
            vmem_limit=V7X_VMEM_LIMIT, name="matmul"):
    m, k = a.shape
    n = w.shape[0] if w_transposed else w.shape[1]
    tm, tn = _tile(m, tm), _tile(n, tn)
    ni, nj = m // tm, n // tn
    w_spec = pl.BlockSpec((tn, k), lambda i, j: (j, 0)) if w_transposed else pl.BlockSpec((k, tn), lambda i, j: (0, j))
    in_specs, operands = [pl.BlockSpec((tm, k), lambda i, j: (i, 0)), w_spec], [a, w]
    scratch = []
    if part is not None:
        in_specs.append(pl.BlockSpec((tm, part.shape[1]), lambda i, j: (i, 0)))
        operands.append(part)
        scratch.append(pltpu.VMEM((tm, 1), F32))
    out_specs = [pl.BlockSpec((tm, tn), lambda i, j: (i, j))]
    out_shape = [jax.ShapeDtypeStruct((m, n), out_dtype)]
    side = [sw if isinstance(sw, tuple) else (sw, None) for sw in side]
    scale_specs, scale_operands = [], []
    for sw, row_scale in side:
        rows = sw.shape[0] // (ni * nj)
        assert rows * ni * nj == sw.shape[0]
        slab = pl.BlockSpec((rows, sw.shape[1]), lambda i, j: (i * nj + j, 0))
        in_specs.append(slab)
        operands.append(sw)
        out_specs.append(slab)
        out_shape.append(jax.ShapeDtypeStruct(sw.shape, BF16))
        if row_scale is not None:
            scale_specs.append(pl.BlockSpec((rows, 1), lambda i, j: (i * nj + j, 0)))
            scale_operands.append(row_scale.astype(F32).reshape(-1, 1))
    in_specs += scale_specs
    operands += scale_operands
    w_dims = NT_DIMS if w_transposed else (((1,), (0,)), ((), ()))
    outs = pl.pallas_call(
        functools.partial(_mm_kernel, act=act, w_dims=w_dims, scaled=part is not None,
                          side_scaled=tuple(rs is not None for _, rs in side)),
        grid=(ni, nj),
        in_specs=in_specs,
        out_specs=out_specs,
        out_shape=out_shape,
        scratch_shapes=scratch,
        compiler_params=_params("parallel", "arbitrary", vmem_limit=vmem_limit),
        name=name,
    )(*operands)
    return outs if side else outs[0]


def _cast_rows_kernel(w_ref, o_ref):
    o_ref[...] = w_ref[...].astype(o_ref.dtype)


def _cast_rows(w, row0, nrows, tr=512):
    k = w.shape[1]
    tr = math.gcd(nrows, tr)
    align = math.gcd(math.gcd(row0, tr), 64)
    return pl.pallas_call(
        _cast_rows_kernel,
        grid=(nrows // tr,),
        in_specs=[pl.BlockSpec((pl.Element(tr), pl.Element(k)),
                               lambda i: (pl.multiple_of(row0 + i * tr, align), 0))],
        out_specs=pl.BlockSpec((tr, k), lambda i: (i, 0)),
        out_shape=jax.ShapeDtypeStruct((nrows, k), BF16),
        compiler_params=_params("parallel"),
        name="cast_weight_rows",
    )(w)


def _inproj_kernel(u_ref, wt_ref, wft_ref, side_ref, o_ref, f_ref, side_out_ref):
    @pl.when(pl.program_id(1) == 0)
    def _():
        f_ref[...] = lax.dot_general(u_ref[...], wft_ref[...], NT_DIMS, preferred_element_type=F32)

    side_out_ref[...] = side_ref[...].astype(side_out_ref.dtype)
    o_ref[...] = lax.dot_general(u_ref[...], wt_ref[...], NT_DIMS, preferred_element_type=F32).astype(o_ref.dtype)


def _inproj(u, wt, wft, side_src, side_row0, side_rows, tm=1024, tn=768):
    m, k = u.shape
    n, nf = wt.shape[0], wft.shape[0]
    tm, tn = _tile(m, tm), math.gcd(n, tn)
    ni, nj = m // tm, n // tn
    slab = side_rows // (ni * nj)
    assert slab * ni * nj == side_rows
    align = math.gcd(math.gcd(side_row0, slab), 64)
    return pl.pallas_call(
        _inproj_kernel,
        grid=(ni, nj),
        in_specs=[pl.BlockSpec((tm, k), lambda i, j: (i, 0)), pl.BlockSpec((tn, k), lambda i, j: (j, 0)),
                  pl.BlockSpec((nf, k), lambda i, j: (0, 0)),
                  pl.BlockSpec((pl.Element(slab), pl.Element(side_src.shape[1])),
                               lambda i, j: (pl.multiple_of(side_row0 + (i * nj + j) * slab, align), 0))],
        out_specs=[pl.BlockSpec((tm, tn), lambda i, j: (i, j)), pl.BlockSpec((tm, nf), lambda i, j: (i, 0)),
                   pl.BlockSpec((slab, side_src.shape[1]), lambda i, j: (i * nj + j, 0))],
        out_shape=[jax.ShapeDtypeStruct((m, n), BF16), jax.ShapeDtypeStruct((m, nf), F32),
                   jax.ShapeDtypeStruct((side_rows, side_src.shape[1]), BF16)],
        compiler_params=_params("parallel", "arbitrary"),
        name="inproj_fox",
    )(u, wt, wft, side_src)


def _outproj_kernel(a_ref, b_ref, wa_ref, wb_ref, x_ref, g_ref, o_ref, ob_ref, part_ref):
    acc = jnp.dot(a_ref[...], wa_ref[...], preferred_element_type=F32)
    acc = acc + jnp.dot(b_ref[...], wb_ref[...], preferred_element_type=F32)
    h = x_ref[...] + acc
    o_ref[...] = h
    ob_ref[...] = (h * g_ref[...]).astype(ob_ref.dtype)
    part_ref[...] = _fold_sq(h)


def _outproj(a, b, w, x, g, tm=1024, tn=1024):
    m, ka = a.shape
    kb = b.shape[1]
    n = w.shape[1]
    assert ka == kb and w.shape[0] == ka + kb
    tm, tn = _tile(m, tm), _tile(n, tn)
    tile = pl.BlockSpec((tm, tn), lambda i, j: (i, j))
    return pl.pallas_call(
        _outproj_kernel,
        grid=(m // tm, n // tn),
        in_specs=[pl.BlockSpec((tm, ka), lambda i, j: (i, 0)), pl.BlockSpec((tm, kb), lambda i, j: (i, 0)),
                  pl.BlockSpec((ka, tn), lambda i, j: (0, j)), pl.BlockSpec((kb, tn), lambda i, j: (1, j)), tile,
                  pl.BlockSpec((1, tn), lambda i, j: (0, j))],
        out_specs=[tile, tile, pl.BlockSpec((tm, HEAD_DIM), lambda i, j: (i, j))],
        out_shape=[jax.ShapeDtypeStruct((m, n), F32), jax.ShapeDtypeStruct((m, n), BF16),
                   jax.ShapeDtypeStruct((m, n // tn * HEAD_DIM), F32)],
        compiler_params=_params("parallel", "arbitrary", vmem_limit=V7X_VMEM_LIMIT_MAX),
        name="outproj",
    )(a, b, w, w, x, g.reshape(1, n).astype(F32))


def _mm_res_kernel(a_ref, w_ref, r_ref, o_ref, ob_ref):
    k = pl.program_id(2)

    @pl.when(k == 0)
    def _():
        o_ref[...] = r_ref[...] + jnp.dot(a_ref[...], w_ref[...], preferred_element_type=F32)

    @pl.when(k > 0)
    def _():
        o_ref[...] += jnp.dot(a_ref[...], w_ref[...], preferred_element_type=F32)

    @pl.when(k == pl.num_programs(2) - 1)
    def _():
        ob_ref[...] = o_ref[...].astype(ob_ref.dtype)


def _matmul_residual(a, w, r, tm=1024, tn=1024, tk=4096):
    m, kd = a.shape
    n = w.shape[1]
    tm, tn, tk = _tile(m, tm), _tile(n, tn), _tile(kd, tk)
    return pl.pallas_call(
        _mm_res_kernel,
        grid=(m // tm, n // tn, kd // tk),
        in_specs=[pl.BlockSpec((tm, tk), lambda i, j, k: (i, k)), pl.BlockSpec((tk, tn), lambda i, j, k: (k, j)),
                  pl.BlockSpec((tm, tn), lambda i, j, k: (i, j))],
        out_specs=[pl.BlockSpec((tm, tn), lambda i, j, k: (i, j)), pl.BlockSpec((tm, tn), lambda i, j, k: (i, j))],
        out_shape=[jax.ShapeDtypeStruct((m, n), F32), jax.ShapeDtypeStruct((m, n), BF16)],
        compiler_params=_params("parallel", "parallel", "arbitrary", vmem_limit=V7X_VMEM_LIMIT_MAX),
        name="mlp_down",
    )(a, w, r)


def _split3(x):
    hi = x.astype(BF16).astype(F32)
    r = x - hi
    mid = r.astype(BF16).astype(F32)
    return hi, mid, r - mid


def _fcum_kernel(f_ref, bias_ref, o_ref, carry_ref, *, n_heads):
    t = pl.program_id(1)

    @pl.when(t == 0)
    def _():
        carry_ref[...] = jnp.zeros_like(carry_ref)

    x = f_ref[...] + bias_ref[...]
    ls = jnp.minimum(x, 0.0) - jnp.log1p(jnp.exp(-jnp.abs(x)))
    tc = x.shape[0]
    r = lax.broadcasted_iota(jnp.int32, (tc, tc), 0)
    c = lax.broadcasted_iota(jnp.int32, (tc, tc), 1)
    tri = jnp.where(c <= r, 1.0, 0.0).astype(BF16)
    c3 = jnp.dot(tri, jnp.concatenate(_split3(ls), axis=1).astype(BF16), preferred_element_type=F32)
    cs = c3[:, :HEAD_DIM] + c3[:, HEAD_DIM:2 * HEAD_DIM] + c3[:, 2 * HEAD_DIM:] + carry_ref[...]
    carry_ref[...] = cs[tc - 1:tc, :]
    lane = lax.broadcasted_iota(jnp.int32, (tc, HEAD_DIM), 1)
    for h in range(n_heads):
        hi, mid, lo = _split3(jnp.broadcast_to(cs[:, h:h + 1] * (-LOG2E), (tc, HEAD_DIM)))
        tile = jnp.where(lane == 0, hi, jnp.where(lane == 1, mid, jnp.where(lane == 2, lo, 0.0)))
        o_ref[:, h * HEAD_DIM:(h + 1) * HEAD_DIM] = tile.astype(o_ref.dtype)


def _forget_bias_tiles(f_logit, bias, batch, seq, n_heads):
    tc = _tile(seq, 256)
    nt = seq // tc
    return pl.pallas_call(
        functools.partial(_fcum_kernel, n_heads=n_heads),
        grid=(batch, nt),
        in_specs=[pl.BlockSpec((tc, HEAD_DIM), lambda b, t: (b * nt + t, 0)),
                  pl.BlockSpec((1, HEAD_DIM), lambda b, t: (0, 0))],
        out_specs=pl.BlockSpec((tc, n_heads * HEAD_DIM), lambda b, t: (b * nt + t, 0)),
        out_shape=jax.ShapeDtypeStruct((batch * seq, n_heads * HEAD_DIM), BF16),
        scratch_shapes=[pltpu.VMEM((1, HEAD_DIM), F32)],
        compiler_params=_params("parallel", "arbitrary"),
        name="fox_forget_cumsum",
    )(f_logit, bias)


def _head_rmsnorm(x, g):
    x = x.astype(F32)
    ms = jnp.mean(x * x, axis=-1, keepdims=True)
    return (x * lax.rsqrt(ms + RMS_EPS) * g).astype(BF16)


def _attn_kernel(q_ref, k_ref, kb_ref, v_ref, gq_ref, gk_ref, o_ref, kp_ref, s_ref, m_ref, l_ref, acc_ref, *,
                 blk, heads):
    qi = pl.program_id(2)
    hcols = [slice(hh * HEAD_DIM, (hh + 1) * HEAD_DIM) for hh in range(heads)]
    seq = k_ref.shape[0]

    @pl.when(qi == 0)
    def _():
        def fill(c, carry):
            rows = pl.ds(pl.multiple_of(c * blk, blk), blk)
            for hh, cols in enumerate(hcols):
                kp_ref[hh, rows, :HEAD_DIM] = _head_rmsnorm(k_ref[rows, cols], gk_ref[...])
                kp_ref[hh, rows, HEAD_DIM:] = kb_ref[rows, cols]
            return carry
        lax.fori_loop(0, seq // blk, fill, 0)

    lane = lax.broadcasted_iota(jnp.int32, (blk, HEAD_DIM), 1)
    q_ones = jnp.where(lane < 3, 1.0, 0.0).astype(BF16)
    key_pos = lax.broadcasted_iota(jnp.int32, (blk, blk), 0)
    qry_pos = lax.broadcasted_iota(jnp.int32, (blk, blk), 1)

    q2 = [jnp.concatenate([_head_rmsnorm(q_ref[:, cols], gq_ref[...]), q_ones], axis=1) for cols in hcols]

    def scores(hh, j, slot):
        rows = pl.ds(pl.multiple_of(j * blk, blk), blk)
        s_ref[hh, slot] = lax.dot_general(kp_ref[hh, rows, :], q2[hh], NT_DIMS, preferred_element_type=F32)

    def update(hh, j, slot, diagonal=False):
        rows = pl.ds(pl.multiple_of(j * blk, blk), blk)
        s = s_ref[hh, slot]
        if diagonal:
            s = jnp.where(key_pos <= qry_pos, s, -jnp.inf)
            m_new = jnp.max(s, axis=0, keepdims=True)
        else:
            m = m_ref[hh]
            m_new = jnp.maximum(m, jnp.max(s, axis=0, keepdims=True))
        p = jnp.exp2(s - m_new)
        pv = lax.dot_general(v_ref[rows, hcols[hh]], p.astype(BF16), TN_DIMS, preferred_element_type=F32)
        m_ref[hh] = m_new
        if diagonal:
            l_ref[hh] = jnp.sum(p, axis=0, keepdims=True)
            acc_ref[hh] = pv
        else:
            alpha = jnp.exp2(m - m_new)
            l_ref[hh] = alpha * l_ref[hh] + jnp.sum(p, axis=0, keepdims=True)
            acc_ref[hh] = alpha * acc_ref[hh] + pv

    def each_head(fn, *args, **kwargs):
        for hh in range(heads):
            fn(hh, *args, **kwargs)

    def pair(t0):
        each_head(scores, t0 + 1, 1)
        for hh in range(heads):
            update(hh, t0, 0)
            scores(hh, t0 + 2, 0)
        each_head(update, t0 + 1, 1)

    def quad(k, carry):
        pair(4 * k)
        pair(4 * k + 2)
        return carry

    each_head(scores, qi, 1)
    each_head(scores, 0, 0)
    each_head(update, qi, 1, diagonal=True)
    lax.fori_loop(0, qi // 4, quad, 0)

    @pl.when((qi // 2) % 2 == 1)
    def _():
        pair(4 * (qi // 4))

    @pl.when(qi % 2 == 1)
    def _():
        each_head(update, qi - 1, 0)

    for hh, cols in enumerate(hcols):
        o_ref[:, cols] = (acc_ref[hh] / l_ref[hh]).T.astype(o_ref.dtype)


def _fox_attention(proj, kbias, gq, gk, batch, seq, n_heads, blk=512, heads=2):
    blk = _tile(seq, blk)
    heads = _tile(n_heads, heads)
    nq = seq // blk
    hblk = n_heads // heads
    w = heads * HEAD_DIM
    gain = pl.BlockSpec((1, HEAD_DIM), lambda b, h, i: (0, 0))
    return pl.pallas_call(
        functools.partial(_attn_kernel, blk=blk, heads=heads),
        grid=(batch, hblk, nq),
        in_specs=[pl.BlockSpec((blk, w), lambda b, h, i: (b * nq + i, h)),
                  pl.BlockSpec((seq, w), lambda b, h, i: (b, hblk + h)),
                  pl.BlockSpec((seq, w), lambda b, h, i: (b, h)),
                  pl.BlockSpec((seq, w), lambda b, h, i: (b, 2 * hblk + h)), gain, gain],
        out_specs=pl.BlockSpec((blk, w), lambda b, h, i: (b * nq + i, h)),
        out_shape=jax.ShapeDtypeStruct((batch * seq, n_heads * HEAD_DIM), BF16),
        scratch_shapes=[pltpu.VMEM((heads, seq, 2 * HEAD_DIM), BF16), pltpu.VMEM((heads, 2, blk, blk), F32),
                        pltpu.VMEM((heads, 1, blk), F32), pltpu.VMEM((heads, 1, blk), F32),
                        pltpu.VMEM((heads, HEAD_DIM, blk), F32)],
        compiler_params=_params("parallel", "parallel", "arbitrary"),
        name="fox_attention",
    )(proj, proj, kbias, proj, gq, gk)


def _hgrn_kernel(q_ref, z_ref, v_ref, gate_ref, lbl_ref, gn_ref, o_ref, st_ref, rb_ref, *, layer, heads, chunks):
    C, c = HGRN_CHUNK, HGRN_SUB
    t = pl.program_id(2)

    @pl.when(t == 0)
    def _():
        st_ref[...] = jnp.zeros_like(st_ref)

    logits = lbl_ref[...]
    e = jnp.exp(logits - jnp.max(logits, axis=0, keepdims=True))
    lb_all = jnp.sum(e[:layer + 1], axis=0, keepdims=True) / jnp.sum(e, axis=0, keepdims=True)

    row = lax.broadcasted_iota(jnp.int32, (C, C), 0)
    col = lax.broadcasted_iota(jnp.int32, (C, C), 1)
    ltri = jnp.where(col <= row, 1.0, 0.0).astype(BF16)
    sizes = [c << l for l in range(1, (C // c).bit_length())]
    xor = row ^ col
    level = jnp.zeros((C, C), jnp.int32)
    for l, m in enumerate(sizes):
        level = jnp.where(xor >= m // 2, l + 1, level)
    level = jnp.where(col > row, -1, level)
    lane = lax.broadcasted_iota(jnp.int32, (c, C), 1)

    def chunk_body(ci, carry):
        r0 = pl.multiple_of(ci * C, C)
        hs = range(heads)
        cols = [slice(hh * HEAD_DIM, (hh + 1) * HEAD_DIM) for hh in hs]
        qs, kk, b, v, o, a = {}, {}, {}, {}, {}, {}
        for hh in hs:
            lb = lb_all[:, cols[hh]]
            q = q_ref[pl.ds(r0, C), cols[hh]].astype(F32)
            z = z_ref[pl.ds(r0, C), cols[hh]].astype(F32)
            v[hh] = v_ref[pl.ds(r0, C), cols[hh]]
            ez = jnp.exp(-jnp.abs(z))
            inv = 1.0 / (1.0 + ez)
            sig_z = jnp.where(z >= 0, 1.0, ez) * inv
            sig_mz = jnp.where(z >= 0, ez, 1.0) * inv
            kk[hh] = (1.0 - lb) * sig_mz
            qs[hh] = q * _sigmoid(q)
            log2f = jnp.concatenate(_split3(jnp.log2(lb + (1.0 - lb) * sig_z)), axis=1).astype(BF16)
            cs = jnp.dot(ltri, log2f, preferred_element_type=F32)
            b[hh] = cs[:, :HEAD_DIM] + cs[:, HEAD_DIM:2 * HEAD_DIM] + cs[:, 2 * HEAD_DIM:]
            rb_ref[2 * hh] = kk[hh]
            rb_ref[2 * hh + 1] = b[hh]

        for hh in hs:
            o[hh] = lax.dot_general((qs[hh] * jnp.exp2(b[hh])).astype(BF16), st_ref[hh].astype(BF16), NT_DIMS,
                                    preferred_element_type=F32)

        blocks = {hh: [] for hh in hs}
        for i in range(C // c):
            rows = slice(i * c, (i + 1) * c)
            blk = {hh: jnp.zeros((c, C), F32) for hh in hs}
            for s in range(i * c, (i + 1) * c):
                at_s = lane == s
                for hh in hs:
                    w = qs[hh][rows] * rb_ref[2 * hh, s:s + 1, :] * jnp.exp2(b[hh][rows] - rb_ref[2 * hh + 1, s:s + 1, :])
                    blk[hh] = jnp.where(at_s, jnp.sum(w, axis=-1, keepdims=True), blk[hh])
            for hh in hs:
                blocks[hh].append(blk[hh])
        in_sub = level == 0
        for hh in hs:
            a[hh] = jnp.where(in_sub, jnp.concatenate(blocks[hh], axis=0), 0.0)

        for l, m in enumerate(sizes):
            at_level = level == l + 1
            for hh in hs:
                bh = b[hh]
                edge = jnp.concatenate(
                    [jnp.broadcast_to(bh[j * m + m // 2 - 1:j * m + m // 2, :], (m, HEAD_DIM)) for j in range(C // m)],
                    axis=0)
                x = jnp.exp2(-jnp.abs(bh - edge))
                pair = lax.dot_general((qs[hh] * x).astype(BF16), (kk[hh] * x).astype(BF16), NT_DIMS,
                                       preferred_element_type=F32)
                a[hh] = jnp.where(at_level, pair, a[hh])

        for hh in hs:
            o[hh] = o[hh] + jnp.dot(a[hh].astype(BF16), v[hh], preferred_element_type=F32)
            b_last = b[hh][C - 1:C, :]
            k_d = (kk[hh] * jnp.exp2(b_last - b[hh])).astype(BF16)
            st_ref[hh] = st_ref[hh] * jnp.exp2(b_last) + lax.dot_general(v[hh], k_d, TN_DIMS,
                                                                         preferred_element_type=F32)

        for hh in hs:
            gate = gate_ref[pl.ds(r0, C), cols[hh]].astype(F32)
            ms = jnp.mean(o[hh] * o[hh], axis=-1, keepdims=True)
            on = o[hh] * lax.rsqrt(ms + RMS_EPS) * gn_ref[...]
            o_ref[pl.ds(r0, C), cols[hh]] = (on * (gate * _sigmoid(gate))).astype(o_ref.dtype)
        return carry

    lax.fori_loop(0, chunks, chunk_body, 0)


def _hgrn2(proj, lb_logits, gn, layer, batch, seq, n_heads, col0, heads=4, rows=1024):
    heads = _tile(n_heads, heads)
    rows = _tile(seq, rows)
    assert rows % HGRN_CHUNK == 0 and col0 % heads == 0
    nt = seq // rows
    w = heads * HEAD_DIM
    hblk = n_heads // heads

    def in_spec(group):
        return pl.BlockSpec((rows, w), lambda b, h, t: (b * nt + t, col0 // heads + group * hblk + h))

    return pl.pallas_call(
        functools.partial(_hgrn_kernel, layer=layer, heads=heads, chunks=rows // HGRN_CHUNK),
        grid=(batch, hblk, nt),
        in_specs=[in_spec(0), in_spec(1), in_spec(2), in_spec(3),
                  pl.BlockSpec((lb_logits.shape[0], w), lambda b, h, t: (0, h)),
                  pl.BlockSpec((1, HEAD_DIM), lambda b, h, t: (0, 0))],
        out_specs=pl.BlockSpec((rows, w), lambda b, h, t: (b * nt + t, h)),
        out_shape=jax.ShapeDtypeStruct((batch * seq, n_heads * HEAD_DIM), BF16),
        scratch_shapes=[pltpu.VMEM((heads, HEAD_DIM, HEAD_DIM), F32),
                        pltpu.VMEM((2 * heads, HGRN_CHUNK, HEAD_DIM), F32)],
        compiler_params=_params("parallel", "parallel", "arbitrary"),
        name="hgrn2",
    )(proj, proj, proj, proj, lb_logits, gn)


def _ple_embed_kernel(p_ref, w_ref, g_ref, o_ref):
    e = jnp.dot(p_ref[...].astype(BF16), w_ref[...], preferred_element_type=F32)
    ms = jnp.mean(e * e, axis=-1, keepdims=True)
    o_ref[...] = (e * lax.rsqrt(ms + RMS_EPS) * g_ref[...]).astype(o_ref.dtype)


def _ple_embed(p, w, g):
    m, kd = p.shape
    n = w.shape[1]
    tm = _tile(m, 512)
    return pl.pallas_call(
        _ple_embed_kernel,
        grid=(m // tm,),
        in_specs=[pl.BlockSpec((tm, kd), lambda i: (i, 0)), pl.BlockSpec((kd, n), lambda i: (0, 0)),
                  pl.BlockSpec((1, n), lambda i: (0, 0))],
        out_specs=pl.BlockSpec((tm, n), lambda i: (i, 0)),
        out_shape=jax.ShapeDtypeStruct((m, n), BF16),
        compiler_params=_params("parallel"),
        name="ple_embed",
    )(p, w, g.reshape(1, n).astype(F32))


def _ple_gate_kernel(a_ref, w_ref, e_ref, h_ref, o_ref, rs_ref):
    @pl.when(pl.program_id(1) == 0)
    def _():
        a = a_ref[...].astype(F32)
        rs_ref[...] = lax.rsqrt(jnp.mean(a * a, axis=-1, keepdims=True) + RMS_EPS)

    gate = _sigmoid(jnp.dot(a_ref[...], w_ref[...], preferred_element_type=F32) * rs_ref[...])
    o_ref[...] = h_ref[...] + gate * e_ref[...]


def _ple_gate(a, w, e, h, tm=1024, tn=1024):
    m, kd = a.shape
    n = w.shape[1]
    tm, tn = _tile(m, tm), _tile(n, tn)
    return pl.pallas_call(
        _ple_gate_kernel,
        grid=(m // tm, n // tn),
        in_specs=[pl.BlockSpec((tm, kd), lambda i, j: (i, 0)), pl.BlockSpec((kd, tn), lambda i, j: (0, j)),
                  pl.BlockSpec((tm, tn), lambda i, j: (i, j)), pl.BlockSpec((tm, tn), lambda i, j: (i, j))],
        out_specs=pl.BlockSpec((tm, tn), lambda i, j: (i, j)),
        out_shape=jax.ShapeDtypeStruct((m, n), F32),
        scratch_shapes=[pltpu.VMEM((tm, 1), F32)],
        compiler_params=_params("parallel", "arbitrary", vmem_limit=V7X_VMEM_LIMIT_MAX),
        name="ple_gate",
    )(a, w, e, h)


def _layer(h, p, layer, batch, seq, norm_mix_g, w_in, fox_f_bias, fox_q_norm_g, fox_k_norm_g, hgrn_lb_logits,
           hgrn_norm_g, w_out, norm_mlp_g, w_up, w_down, ple_norm_g, w_ple_gate, w_ple_proj, ple_post_g):
    d = h.shape[1]
    d_a = d_b = d // 2
    n_a, n_b = d_a // HEAD_DIM, d_b // HEAD_DIM
    assert n_a <= HEAD_DIM and w_in.shape[1] == 3 * d_a + n_a + 4 * d_b

    w_in_t = jnp.swapaxes(w_in, 0, 1)
    w_fox = _cast_rows(w_in_t, 0, 3 * d_a)
    w_fa = _cast_rows(w_in_t, 3 * d_a, HEAD_DIM)
    q_gain = (fox_q_norm_g.astype(F32) * (LOG2E / math.sqrt(HEAD_DIM))).reshape(1, HEAD_DIM)
    k_gain = fox_k_norm_g.astype(F32).reshape(1, HEAD_DIM)

    u = _rmsnorm(h, norm_mix_g)
    proj_a, f_logit, w_hgrn = _inproj(u, w_fox, w_fa, w_in_t, 3 * d_a + n_a, 4 * d_b)
    proj_b, w_up_b, w_out_b = _matmul(u, w_hgrn, BF16, w_transposed=True, side=(w_up, w_out), name="inproj_hgrn")

    f_bias = jnp.pad(fox_f_bias.astype(F32), (0, HEAD_DIM - n_a)).reshape(1, HEAD_DIM)
    kbias = _forget_bias_tiles(f_logit, f_bias, batch, seq, n_a)
    out_a = _fox_attention(proj_a, kbias, q_gain, k_gain, batch, seq, n_a)

    out_b = _hgrn2(proj_b, hgrn_lb_logits.astype(F32), hgrn_norm_g.astype(F32).reshape(1, HEAD_DIM), layer,
                   batch, seq, n_b, col0=0)

    h, hg, part = _outproj(out_a, out_b, w_out_b, h, norm_mlp_g)
    hid, w_down_b, w_gate_g = _matmul(hg, w_up_b, BF16, act="relu2", part=part,
                                      side=(w_down, (w_ple_gate, ple_norm_g)),
                                      vmem_limit=V7X_VMEM_LIMIT_HIGH, name="mlp_up")
    h, hb = _matmul_residual(hid, w_down_b, h)

    e = _ple_embed(p, w_ple_proj.astype(BF16), ple_post_g)
    return _ple_gate(hb, w_gate_g, e, h)


def kernel(x, p, norm_mix_g, w_in, fox_f_bias, fox_q_norm_g, fox_k_norm_g, hgrn_lb_logits, hgrn_norm_g, w_out,
           norm_mlp_g, w_up, w_down, ple_norm_g, w_ple_gate, w_ple_proj, ple_post_g):
    batch, seq, d = x.shape
    h = x.reshape(batch * seq, d)
    for i in range(w_in.shape[0]):
        h = _layer(h, p[i].reshape(batch * seq, -1), i, batch, seq, norm_mix_g[i], w_in[i], fox_f_bias[i],
                   fox_q_norm_g[i], fox_k_norm_g[i], hgrn_lb_logits, hgrn_norm_g[i], w_out[i], norm_mlp_g[i],
                   w_up[i], w_down[i], ple_norm_g[i], w_ple_gate[i], w_ple_proj[i], ple_post_g[i])
    return h.reshape(batch, seq, d)
```
